```python
import jax, jax.numpy as jnp
from jax import lax
import numpy as np

D_MODEL = 1024
BATCH = 16
SEQ = 256
DEPTH = 4
DEC_BATCH = 2
DEC_SEQ = 4096
PAST_LEN = 256

GRID_W = 64
NA_HEADS = 8
NA_HEAD_DIM = 64
NA_WIDTH = NA_HEADS * NA_HEAD_DIM
NA_ROWS = 8
NA_COLS = 16
NA_SCALE = NA_HEAD_DIM ** -0.5
MLA_HEADS = 8
MLA_NOPE = 64
MLA_ROPE = 32
MLA_V = 64
MLA_WIDTH = MLA_HEADS * MLA_V
Q_LORA = 256
KV_LORA = 128
MLA_SCALE = (MLA_NOPE + MLA_ROPE) ** -0.5
ROPE_THETA = 10000.0
Q_BLOCK = 128
FN_GROUPS = 4
FN_GROUP_W = 128
FN_WIDTH = FN_GROUPS * FN_GROUP_W
EPS = 1e-6
SPLIT_SIZES = (3 * NA_WIDTH, NA_WIDTH, Q_LORA, KV_LORA, MLA_ROPE, MLA_WIDTH, FN_WIDTH, FN_WIDTH, 3 * D_MODEL)
SPLIT_POINTS = tuple(sum(SPLIT_SIZES[:i + 1]) for i in range(len(SPLIT_SIZES) - 1))
D_IN = sum(SPLIT_SIZES)

kernel_name = 'hybrid_flow_natten_mla_fnet_step'


def rms_norm(x, g):
    xf = x.astype(jnp.float32)
    y = xf * lax.rsqrt(jnp.mean(xf * xf, axis=-1, keepdims=True) + EPS)
    return (y * g.astype(jnp.float32)).astype(x.dtype)


def axial_rope_tables(n, dtype):
    t = jnp.arange(n, dtype=jnp.int32)
    row = (t // GRID_W).astype(jnp.float32)
    col = (t % GRID_W).astype(jnp.float32)
    half = MLA_ROPE // 2
    inv_freq = ROPE_THETA ** (-jnp.arange(0, half, 2, dtype=jnp.float32) / half)
    ar = row[:, None] * inv_freq[None, :]
    ac = col[:, None] * inv_freq[None, :]
    ang = jnp.concatenate([ar, ar, ac, ac], axis=-1)
    return jnp.cos(ang).astype(dtype), jnp.sin(ang).astype(dtype)


def _rotate_half(z):
    z1, z2 = jnp.split(z, 2, axis=-1)
    return jnp.concatenate([-z2, z1], axis=-1)


def apply_axial_rope(x, cos, sin):
    xr, xc = jnp.split(x, 2, axis=-1)
    rot = jnp.concatenate([_rotate_half(xr), _rotate_half(xc)], axis=-1)
    return x * cos + rot * sin


def branch_inputs(x, cond, w_ada, b_ada, norm_g, w_in):
    shift, scale, gate = jnp.split(jax.nn.silu(cond) @ w_ada + b_ada, 3, axis=-1)
    xm = rms_norm(x, norm_g) * (1.0 + scale) + shift
    parts = jnp.split(xm @ w_in, SPLIT_POINTS, axis=-1)
    return gate, parts


def merge_branches(o_na, o_mla, o_fn, gate_na, gate_mla, gate_fn, merge_logits,
                   w_o_na, w_o_mla, w_o_fourier, w_out):
    g_na, g_mla, g_fn = jnp.split(jax.nn.sigmoid(merge_logits), 3, axis=-1)
    merged = (g_na * ((o_na * jax.nn.silu(gate_na)) @ w_o_na)
              + g_mla * ((o_mla * jax.nn.silu(gate_mla)) @ w_o_mla)
              + g_fn * ((o_fn * jax.nn.silu(gate_fn)) @ w_o_fourier))
    return merged @ w_out


def dense_attend(q, k, v):
    s = jnp.einsum('bqhd,bkhd->bhqk', q, k).astype(jnp.float32) * NA_SCALE
    p = jax.nn.softmax(s, axis=-1).astype(v.dtype)
    return jnp.einsum('bhqk,bkhd->bqhd', p, v)


def neighbourhood_attend(q, k, v, k_ctx, v_ctx, bias_table):
    B, N = q.shape[0], q.shape[1]
    rows = N // GRID_W
    kr = min(NA_ROWS, rows)
    r = np.arange(rows)
    row_start = np.clip(r - kr // 2, 0, rows - kr)
    row_idx = row_start[:, None] + np.arange(kr)[None, :]
    dr = row_idx - r[:, None]
    col = np.arange(GRID_W)
    col_start = np.clip(col - NA_COLS // 2, 0, GRID_W - NA_COLS)
    dc = col[None, :] - col[:, None]
    col_in = (col[None, :] >= col_start[:, None]) & (col[None, :] < col_start[:, None] + NA_COLS)
    bias = bias_table[:, (dr + NA_ROWS - 1)[:, None, :, None],
                      np.clip(dc + NA_COLS - 1, 0, 2 * NA_COLS - 2)[None, :, None, :]]
    qg = q.reshape(B, rows, GRID_W, NA_HEADS, NA_HEAD_DIM)
    kg = k.reshape(B, rows, GRID_W, NA_HEADS, NA_HEAD_DIM)[:, row_idx]
    vg = v.reshape(B, rows, GRID_W, NA_HEADS, NA_HEAD_DIM)[:, row_idx]
    s_loc = jnp.einsum('brchd,brijhd->bhrcij', qg, kg).astype(jnp.float32) * NA_SCALE + bias.astype(jnp.float32)[None]
    s_loc = jnp.where(col_in[:, None, :], s_loc, -jnp.inf)
    s_loc = s_loc.reshape(B, NA_HEADS, rows, GRID_W, kr * GRID_W)
    s_ctx = jnp.einsum('brchd,blhd->bhrcl', qg, k_ctx).astype(jnp.float32) * NA_SCALE
    p = jax.nn.softmax(jnp.concatenate([s_loc, s_ctx], axis=-1), axis=-1).astype(v.dtype)
    n_loc = kr * GRID_W
    o = (jnp.einsum('bhrck,brkhd->brchd', p[..., :n_loc], vg.reshape(B, rows, n_loc, NA_HEADS, NA_HEAD_DIM))
         + jnp.einsum('bhrcl,blhd->brchd', p[..., n_loc:], v_ctx))
    return o.reshape(B, N, NA_WIDTH)


def mla_queries(q_lat, q_norm_g, w_uq):
    q = rms_norm(q_lat, q_norm_g) @ w_uq
    q = q.reshape(q.shape[0], q.shape[1], MLA_HEADS, MLA_NOPE + MLA_ROPE)
    return q[..., :MLA_NOPE], q[..., MLA_NOPE:]


def mla_keys_values(ckv, w_ukv):
    kv = (ckv @ w_ukv).reshape(ckv.shape[0], ckv.shape[1], MLA_HEADS, MLA_NOPE + MLA_V)
    return kv[..., :MLA_NOPE], kv[..., MLA_NOPE:]


def mla_attend(q_nope, q_rope, k_nope, k_rope, v):
    s = (jnp.einsum('bqhd,bkhd->bhqk', q_nope, k_nope)
         + jnp.einsum('bqhr,bkr->bhqk', q_rope, k_rope))
    p = jax.nn.softmax(s.astype(jnp.float32) * MLA_SCALE, axis=-1).astype(v.dtype)
    return jnp.einsum('bhqk,bkhd->bqhd', p, v)


def mla_blockwise_attend(q_nope, q_rope, k_nope, k_rope, v):
    B, N = q_nope.shape[0], q_nope.shape[1]
    nb = N // Q_BLOCK
    qn = q_nope.reshape(B, nb, Q_BLOCK, MLA_HEADS, MLA_NOPE).transpose(1, 0, 2, 3, 4)
    qr = q_rope.reshape(B, nb, Q_BLOCK, MLA_HEADS, MLA_ROPE).transpose(1, 0, 2, 3, 4)
    o = lax.map(lambda a: mla_attend(a[0], a[1], k_nope, k_rope, v), (qn, qr))
    return o.transpose(1, 0, 2, 3, 4).reshape(B, N, MLA_WIDTH)


def fourier_mix(u):
    B, N = u.shape[0], u.shape[1]
    ug = u.astype(jnp.float32).reshape(B, N, FN_GROUPS, FN_GROUP_W)
    y = jnp.fft.fft2(ug, axes=(1, 3), norm='ortho').real
    return y.reshape(B, N, FN_WIDTH).astype(u.dtype)


def context_layer(x, c_ctx, w_ada, b_ada, norm_g, w_in, q_norm_g, kv_norm_g, w_uq, w_ukv,
                  w_o_na, w_o_mla, w_o_fourier, w_out):
    B, N = x.shape[0], x.shape[1]
    gate, parts = branch_inputs(x, c_ctx, w_ada, b_ada, norm_g, w_in)
    qkv, gate_na, q_lat, ckv_raw, k_rope, gate_mla, u_fn, gate_fn, merge_logits = parts
    q, k, v = [t.reshape(B, N, NA_HEADS, NA_HEAD_DIM) for t in jnp.split(qkv, 3, axis=-1)]
    o_na = dense_attend(q, k, v).reshape(B, N, NA_WIDTH)
    ckv = rms_norm(ckv_raw, kv_norm_g)
    q_nope, q_rope = mla_queries(q_lat, q_norm_g, w_uq)
    k_nope, v_mla = mla_keys_values(ckv, w_ukv)
    o_mla = mla_attend(q_nope, q_rope, k_nope, k_rope, v_mla).reshape(B, N, MLA_WIDTH)
    o_fn = fourier_mix(u_fn)
    out = merge_branches(o_na, o_mla, o_fn, gate_na, gate_mla, gate_fn, merge_logits,
                         w_o_na, w_o_mla, w_o_fourier, w_out)
    return x + gate * out, k, v, ckv, k_rope


def latent_layer(x, c, na_k_ctx, na_v_ctx, ckv_ctx, krope_ctx, cos, sin,
                 w_ada, b_ada, norm_g, w_in, q_norm_g, kv_norm_g, w_uq, w_ukv, na_bias,
                 w_o_na, w_o_mla, w_o_fourier, w_out):
    B, N = x.shape[0], x.shape[1]
    gate, parts = branch_inputs(x, c[:, None, :], w_ada, b_ada, norm_g, w_in)
    qkv, gate_na, q_lat, ckv_raw, k_rope, gate_mla, u_fn, gate_fn, merge_logits = parts
    q, k, v = [t.reshape(B, N, NA_HEADS, NA_HEAD_DIM) for t in jnp.split(qkv, 3, axis=-1)]
    o_na = neighbourhood_attend(q, k, v, na_k_ctx, na_v_ctx, na_bias)
    ckv = rms_norm(ckv_raw, kv_norm_g)
    q_nope, q_rope = mla_queries(q_lat, q_norm_g, w_uq)
    q_rope = apply_axial_rope(q_rope, cos[:, None, :], sin[:, None, :])
    k_rope = apply_axial_rope(k_rope, cos, sin)
    k_nope_lat, v_lat = mla_keys_values(ckv, w_ukv)
    k_nope_ctx, v_ctx = mla_keys_values(ckv_ctx, w_ukv)
    k_nope_all = jnp.concatenate([k_nope_lat, k_nope_ctx], axis=1)
    k_rope_all = jnp.concatenate([k_rope, krope_ctx], axis=1)
    v_all = jnp.concatenate([v_lat, v_ctx], axis=1)
    o_mla = mla_blockwise_attend(q_nope, q_rope, k_nope_all, k_rope_all, v_all)
    o_fn = fourier_mix(u_fn)
    out = merge_branches(o_na, o_mla, o_fn, gate_na, gate_mla, gate_fn, merge_logits,
                         w_o_na, w_o_mla, w_o_fourier, w_out)
    return x + gate * out


def setup_inputs(seed: int = 0) -> dict:
    key = jax.random.key(seed)
    ks = jax.random.split(key, 24)

    def nrm(k, shape, scale):
        return jax.random.normal(k, shape, jnp.float32) * scale

    return {
        'x_prompt': nrm(ks[0], (BATCH, SEQ, D_MODEL), 1.0),
        'x_sample': nrm(ks[1], (DEC_BATCH, DEC_SEQ, D_MODEL), 1.0),
        'cache_na_k': nrm(ks[2], (DEC_BATCH, DEPTH, PAST_LEN, NA_HEADS, NA_HEAD_DIM), 1.0),
        'cache_na_v': nrm(ks[3], (DEC_BATCH, DEPTH, PAST_LEN, NA_HEADS, NA_HEAD_DIM), 1.0),
        'cache_mla_ckv': nrm(ks[4], (DEC_BATCH, DEPTH, PAST_LEN, KV_LORA), 1.0),
        'cache_mla_krope': nrm(ks[5], (DEC_BATCH, DEPTH, PAST_LEN, MLA_ROPE), 1.0),
        'c': nrm(ks[6], (DEC_BATCH, D_MODEL), 1.0),
        'c_ctx': nrm(ks[7], (D_MODEL,), 1.0),
        'w_ada': nrm(ks[8], (DEPTH, D_MODEL, 3 * D_MODEL), 0.2 * D_MODEL ** -0.5),
        'b_ada': nrm(ks[9], (DEPTH, 3 * D_MODEL), 0.01),
        'norm_g': 1.0 + nrm(ks[10], (DEPTH, D_MODEL), 0.05),
        'w_in': nrm(ks[11], (DEPTH, D_MODEL, D_IN), D_MODEL ** -0.5),
        'q_norm_g': 1.0 + nrm(ks[12], (DEPTH, Q_LORA), 0.05),
        'kv_norm_g': 1.0 + nrm(ks[13], (DEPTH, KV_LORA), 0.05),
        'w_uq': nrm(ks[14], (DEPTH, Q_LORA, MLA_HEADS * (MLA_NOPE + MLA_ROPE)), Q_LORA ** -0.5),
        'w_ukv': nrm(ks[15], (DEPTH, KV_LORA, MLA_HEADS * (MLA_NOPE + MLA_V)), KV_LORA ** -0.5),
        'na_bias': nrm(ks[16], (DEPTH, NA_HEADS, 2 * NA_ROWS - 1, 2 * NA_COLS - 1), 0.1),
        'w_o_na': nrm(ks[17], (DEPTH, NA_WIDTH, D_MODEL), NA_WIDTH ** -0.5),
        'w_o_mla': nrm(ks[18], (DEPTH, MLA_WIDTH, D_MODEL), MLA_WIDTH ** -0.5),
        'w_o_fourier': nrm(ks[19], (DEPTH, FN_WIDTH, D_MODEL), FN_WIDTH ** -0.5),
        'w_out': nrm(ks[20], (DEPTH, D_MODEL, D_MODEL), D_MODEL ** -0.5),
        'final_norm_g': 1.0 + nrm(ks[21], (D_MODEL,), 0.05),
    }


def reference(x_prompt, x_sample, cache_na_k, cache_na_v, cache_mla_ckv, cache_mla_krope, c, c_ctx,
              w_ada, b_ada, norm_g, w_in, q_norm_g, kv_norm_g, w_uq, w_ukv, na_bias,
              w_o_na, w_o_mla, w_o_fourier, w_out, final_norm_g):
    h_ctx = x_prompt
    h_lat = x_sample
    cos, sin = axial_rope_tables(x_sample.shape[1], x_sample.dtype)
    ks_na, vs_na, ckvs, krs = [], [], [], []
    for l in range(DEPTH):
        h_ctx, k_l, v_l, ckv_l, kr_l = context_layer(
            h_ctx, c_ctx, w_ada[l], b_ada[l], norm_g[l], w_in[l], q_norm_g[l], kv_norm_g[l],
            w_uq[l], w_ukv[l], w_o_na[l], w_o_mla[l], w_o_fourier[l], w_out[l])
        ks_na.append(k_l)
        vs_na.append(v_l)
        ckvs.append(ckv_l)
        krs.append(kr_l)
        h_lat = latent_layer(
            h_lat, c, cache_na_k[:, l], cache_na_v[:, l], cache_mla_ckv[:, l], cache_mla_krope[:, l],
            cos, sin, w_ada[l], b_ada[l], norm_g[l], w_in[l], q_norm_g[l], kv_norm_g[l],
            w_uq[l], w_ukv[l], na_bias[l], w_o_na[l], w_o_mla[l], w_o_fourier[l], w_out[l])
    y_prompt = rms_norm(h_ctx, final_norm_g)
    y_sample = rms_norm(h_lat, final_norm_g)
    new_na_k = jnp.stack(ks_na, axis=1)
    new_na_v = jnp.stack(vs_na, axis=1)
    new_mla_ckv = jnp.stack(ckvs, axis=1)
    new_mla_krope = jnp.stack(krs, axis=1)
    return (y_prompt, y_sample, new_na_k, new_na_v, new_mla_ckv, new_mla_krope)
```

```python
import functools

import jax
import jax.numpy as jnp
import numpy as np
from jax import lax
from jax.experimental import pallas as pl
from jax.experimental.pallas import tpu as pltpu

F32 = jnp.float32
BF16 = jnp.bfloat16

D_MODEL = 1024
DEPTH = 4
GRID_W = 64
NA_HEADS = 8
NA_HEAD_DIM = 64
NA_WIDTH = 512
NA_ROWS = 8
NA_COLS = 16
MLA_HEADS = 8
MLA_NOPE = 64
MLA_ROPE = 32
MLA_V = 64
MLA_WIDTH = 512
Q_LORA = 256
KV_LORA = 128
MLA_SCALE = (MLA_NOPE + MLA_ROPE) ** -0.5
ROPE_THETA = 10000.0
FN_GROUPS = 4
FN_GROUP_W = 128
FN_WIDTH = 512
EPS = 1e-6
NEG = -1e30

_O_QKV = 0
_O_GATE_NA = 1536
_O_QLAT = 2048
_O_CKV = 2304
_O_KROPE = 2432
_O_GATE_MLA = 2464
_O_UFN = 2976
_O_GATE_FN = 3488
_O_MERGE = 4000
D_IN = 7072

LANES = 128
VMEM_LIMIT = 56 * 1024 * 1024


def _params(sem):
    return pltpu.CompilerParams(dimension_semantics=sem, vmem_limit_bytes=VMEM_LIMIT)


def _sigmoid(x):
    return 1.0 / (1.0 + jnp.exp(-x))


def _silu(x):
    return x * _sigmoid(x)


def _rms(x, g):
    return x * lax.rsqrt(jnp.mean(x * x, axis=-1, keepdims=True) + EPS) * g


def _dot(a, b):
    return jnp.dot(a, b, preferred_element_type=F32)


def _dot_nt(a, b):
    return lax.dot_general(a, b, (((1,), (1,)), ((), ())), preferred_element_type=F32)


def _split(x):
    hi = x.astype(BF16)
    lo = (x - hi.astype(F32)).astype(BF16)
    return hi, lo


def _ada_kernel(cond_ref, w_ref, b_ref, o_ref):
    a = _silu(cond_ref[...]).astype(BF16)
    o_ref[0] = _dot(a, w_ref[0].astype(BF16)) + b_ref[0]


def _ada_mods(cond, w_ada, b_ada):
    nb = 3
    return pl.pallas_call(
        _ada_kernel,
        grid=(DEPTH, nb),
        in_specs=[
            pl.BlockSpec((8, D_MODEL), lambda l, j: (0, 0)),
            pl.BlockSpec((1, D_MODEL, D_MODEL), lambda l, j: (l, 0, j)),
            pl.BlockSpec((1, 1, D_MODEL), lambda l, j: (l, 0, j)),
        ],
        out_specs=pl.BlockSpec((1, 8, D_MODEL), lambda l, j: (l, 0, j)),
        out_shape=jax.ShapeDtypeStruct((DEPTH, 8, 3 * D_MODEL), F32),
        compiler_params=_params(("arbitrary", "arbitrary")),
        name="ada_mods",
    )(cond, w_ada, b_ada.reshape(DEPTH, 1, 3 * D_MODEL))


_A_Q, _A_K, _A_V, _A_QLAT, _A_CKV, _A_KR, _A_KRROT, _A_U, _A_END = (
    0, 512, 1024, 1536, 1792, 1920, 2048, 2176, 2688)


def _modulated(x_ref, mod_ref, ng_ref):
    x = x_ref[...]
    shift = mod_ref[0, 0:1, :]
    scale = mod_ref[0, 1:2, :]
    return x, (_rms(x, ng_ref[...]) * (1.0 + scale) + shift).astype(BF16)


def _inproj_kernel(*refs, rope, emit_cache):
    (x_ref, mod_ref, ng_ref, w_ref, qng_ref, kvng_ref, wuq_ref, wukv_ref) = refs[:8]
    pos = 8
    if rope:
        cos_ref, sin_ref = refs[pos:pos + 2]
        pos += 2
    (naq_ref, nak_ref, nav_ref, qn_ref, qr_ref, kn_ref, kr_ref, mv_ref, u_ref) = refs[pos:pos + 9]
    pos += 9
    if emit_cache:
        kf_ref, vf_ref, ckvf_ref, krf_ref = refs[pos:pos + 4]

    _, xm = _modulated(x_ref, mod_ref, ng_ref)

    def proj(a, b):
        return _dot(xm, w_ref[:, a:b])

    q = proj(_A_Q, _A_K)
    naq_ref[...] = (q * (NA_HEAD_DIM ** -0.5)).astype(BF16)
    k = proj(_A_K, _A_V)
    nak_ref[...] = k.astype(BF16)
    v = proj(_A_V, _A_QLAT)
    nav_ref[...] = v.astype(BF16)
    if emit_cache:
        kf_ref[...] = k
        vf_ref[...] = v

    qlat = _rms(proj(_A_QLAT, _A_CKV), qng_ref[...]).astype(BF16)
    qq = _dot(qlat, wuq_ref[...])
    qn_ref[...] = qq[:, :512].astype(BF16)
    qr = qq[:, 512:768]
    if rope:
        qr = qr * cos_ref[...] + qq[:, 768:1024] * sin_ref[...]
    qr_ref[...] = qr.astype(BF16)

    ckv = _rms(proj(_A_CKV, _A_KR), kvng_ref[...])
    if emit_cache:
        ckvf_ref[...] = ckv
    kv = _dot(ckv.astype(BF16), wukv_ref[...])
    kn_ref[...] = kv[:, :512].astype(BF16)
    mv_ref[...] = kv[:, 512:].astype(BF16)

    kr = proj(_A_KR, _A_KRROT)
    if emit_cache:
        krf_ref[...] = kr[:, :MLA_ROPE]
    if rope:
        kr = kr * cos_ref[:, :LANES] + proj(_A_KRROT, _A_U) * sin_ref[:, :LANES]
    kr_ref[...] = kr.astype(BF16)

    u_ref[...] = proj(_A_U, _A_END)


def _inproj(x, mod, ng, w_a, qng, kvng, wuq, wukv, cos, sin, *, tile, emit_cache):
    n = x.shape[0]
    groups = mod.shape[0]
    steps = n // tile
    per_group = steps // groups
    rope = cos is not None
    tok = lambda w: pl.BlockSpec((tile, w), lambda i: (i, 0))
    full = lambda a: pl.BlockSpec(a.shape, lambda i: (0,) * a.ndim)
    in_specs = [tok(D_MODEL),
                pl.BlockSpec((1, 3, D_MODEL), lambda i: (i // per_group, 0, 0)),
                full(ng), full(w_a), full(qng), full(kvng), full(wuq), full(wukv)]
    args = [x, mod, ng, w_a, qng, kvng, wuq, wukv]
    if rope:
        steps_per_seq = cos.shape[0] // tile
        in_specs += [pl.BlockSpec((tile, 256), lambda i: (i % steps_per_seq, 0))] * 2
        args += [cos, sin]
    widths = [(512, BF16)] * 3 + [(512, BF16), (256, BF16), (512, BF16), (LANES, BF16), (512, BF16),
                                  (512, F32)]
    if emit_cache:
        widths += [(512, F32), (512, F32), (KV_LORA, F32), (MLA_ROPE, F32)]
    out_shape = [jax.ShapeDtypeStruct((n, w), dt) for w, dt in widths]
    out_specs = [tok(w) for w, _ in widths]
    return pl.pallas_call(
        functools.partial(_inproj_kernel, rope=rope, emit_cache=emit_cache),
        grid=(steps,),
        in_specs=in_specs,
        out_specs=out_specs,
        out_shape=out_shape,
        compiler_params=_params(("arbitrary",)),
        name="inproj_ctx" if emit_cache else "inproj_lat",
    )(*args)


def _lane_iota(n=LANES):
    return lax.broadcasted_iota(jnp.int32, (1, n), 1)


def _head_q(q_pair, half):
    lane = _lane_iota()
    keep = (lane < NA_HEAD_DIM) if half == 0 else (lane >= NA_HEAD_DIM)
    return jnp.where(keep, q_pair, jnp.zeros_like(q_pair))


def _mla_q(qn_ref, qr_ref, h):
    p, half = divmod(h, 2)
    qa = _head_q(qn_ref[:, p * LANES:(p + 1) * LANES], half)
    g, slot = divmod(h, 4)
    qb = qr_ref[:, g * LANES:(g + 1) * LANES]
    lane = _lane_iota()
    keep = (lane >= slot * MLA_ROPE) & (lane < (slot + 1) * MLA_ROPE)
    qb = jnp.where(keep, qb, jnp.zeros_like(qb))
    return jnp.concatenate([qa, qb], axis=1)


def _pair_out(o0, o1):
    return jnp.where(_lane_iota() < NA_HEAD_DIM, o0, o1)


def _softmax_pv(s_list, v_list):
    m = s_list[0].max(axis=-1, keepdims=True)
    for s in s_list[1:]:
        m = jnp.maximum(m, s.max(axis=-1, keepdims=True))
    l = None
    o = None
    for s, v in zip(s_list, v_list):
        e = jnp.exp(s - m)
        ls = e.sum(axis=-1, keepdims=True)
        pv = _dot(e.astype(BF16), v)
        l = ls if l is None else l + ls
        o = pv if o is None else o + pv
    return o / l


def _ctx_attn_kernel(q_ref, k_ref, v_ref, qn_ref, qr_ref, kn_ref, kr_ref, mv_ref, ona_ref, omla_ref):
    for p in range(NA_HEADS // 2):
        sl = slice(p * LANES, (p + 1) * LANES)
        kp = k_ref[:, sl]
        vp = v_ref[:, sl]
        outs = []
        for half in range(2):
            s = _dot_nt(_head_q(q_ref[:, sl], half), kp)
            outs.append(_softmax_pv([s], [vp]))
        ona_ref[:, sl] = _pair_out(*outs)

        kk = jnp.concatenate([kn_ref[:, sl], kr_ref[...]], axis=1)
        mvp = mv_ref[:, sl]
        outs = []
        for half in range(2):
            s = _dot_nt(_mla_q(qn_ref, qr_ref, 2 * p + half), kk) * MLA_SCALE
            outs.append(_softmax_pv([s], [mvp]))
        omla_ref[:, sl] = _pair_out(*outs)


def _ctx_attn(q, k, v, qn, qr, kn, kr, mv, *, seq):
    n = q.shape[0]
    tok = lambda w: pl.BlockSpec((seq, w), lambda b: (b, 0))
    return pl.pallas_call(
        _ctx_attn_kernel,
        grid=(n // seq,),
        in_specs=[tok(512), tok(512), tok(512), tok(512), tok(256), tok(512), tok(LANES), tok(512)],
        out_specs=[tok(512), tok(512)],
        out_shape=[jax.ShapeDtypeStruct((n, 512), F32)] * 2,
        compiler_params=_params(("arbitrary",)),
        name="ctx_attn",
    )(q, k, v, qn, qr, kn, kr, mv)


def _na_lat_kernel(q_ref, k_ref, v_ref, kc_ref, vc_ref, bias_ref, o_ref, *, rows):
    r = pl.program_id(1)
    row_start = jnp.clip(r - NA_ROWS // 2, 0, rows - NA_ROWS)
    start = pl.multiple_of(row_start * GRID_W, GRID_W)
    n_loc = NA_ROWS * GRID_W
    for p in range(NA_HEADS // 2):
        sl = slice(p * LANES, (p + 1) * LANES)
        kp = k_ref[pl.ds(start, n_loc), sl]
        vp = v_ref[pl.ds(start, n_loc), sl]
        kc = kc_ref[0, :, sl]
        vc = vc_ref[0, :, sl]
        outs = []
        for half in range(2):
            qh = _head_q(q_ref[:, sl], half)
            s_loc = _dot_nt(qh, kp) + bias_ref[0, 2 * p + half]
            s_ctx = _dot_nt(qh, kc)
            outs.append(_softmax_pv([s_loc, s_ctx], [vp, vc]))
        o_ref[:, sl] = _pair_out(*outs)


def _na_lat(q, k, v, kc, vc, bias, *, batch, seq):
    rows = seq // GRID_W
    n = q.shape[0]

    def bias_idx(b, r):
        off = r - jnp.clip(r - NA_ROWS // 2, 0, rows - NA_ROWS)
        return (off, 0, 0, 0)

    return pl.pallas_call(
        functools.partial(_na_lat_kernel, rows=rows),
        grid=(batch, rows),
        in_specs=[
            pl.BlockSpec((GRID_W, 512), lambda b, r: (b * rows + r, 0)),
            pl.BlockSpec((seq, 512), lambda b, r: (b, 0)),
            pl.BlockSpec((seq, 512), lambda b, r: (b, 0)),
            pl.BlockSpec((1,) + kc.shape[1:], lambda b, r: (b, 0, 0)),
            pl.BlockSpec((1,) + vc.shape[1:], lambda b, r: (b, 0, 0)),
            pl.BlockSpec((1,) + bias.shape[1:], bias_idx),
        ],
        out_specs=pl.BlockSpec((GRID_W, 512), lambda b, r: (b * rows + r, 0)),
        out_shape=jax.ShapeDtypeStruct((n, 512), F32),
        compiler_params=_params(("arbitrary", "arbitrary")),
        name="na_lat",
    )(q, k, v, kc, vc, bias)


def _na_bias_table(bias_l):
    off = np.arange(NA_ROWS)[:, None]
    kr = np.arange(NA_ROWS)[None, :]
    dr = kr - off + NA_ROWS - 1
    col = np.arange(GRID_W)
    col_start = np.clip(col - NA_COLS // 2, 0, GRID_W - NA_COLS)
    dc = np.clip(col[None, :] - col[:, None] + NA_COLS - 1, 0, 2 * NA_COLS - 2)
    col_in = (col[None, :] >= col_start[:, None]) & (col[None, :] < col_start[:, None] + NA_COLS)
    t = bias_l[:, dr[:, None, :, None], dc[None, :, None, :]]
    t = jnp.where(col_in[None, None, :, None, :], t, NEG)
    return t.transpose(1, 0, 2, 3, 4).reshape(NA_ROWS, NA_HEADS, GRID_W, NA_ROWS * GRID_W)


def _mla_lat_kernel(qn_ref, qr_ref, kn_ref, kr_ref, mv_ref, knc_ref, krc_ref, mvc_ref, o_ref, *, tk):
    tq = qn_ref.shape[0]
    n_tiles = kn_ref.shape[0] // tk
    for p in range(MLA_HEADS // 2):
        sl = slice(p * LANES, (p + 1) * LANES)
        q2 = jnp.concatenate([_mla_q(qn_ref, qr_ref, 2 * p), _mla_q(qn_ref, qr_ref, 2 * p + 1)], axis=0)

        def update(carry, kk, vv):
            m, l, acc = carry
            s = _dot_nt(q2, kk) * MLA_SCALE
            m_new = jnp.maximum(m, s.max(axis=-1, keepdims=True))
            alpha = jnp.exp(m - m_new)
            e = jnp.exp(s - m_new)
            l = alpha * l + e.sum(axis=-1, keepdims=True)
            acc = alpha * acc + _dot(e.astype(BF16), vv)
            return m_new, l, acc

        def body(t, carry):
            ks = pl.ds(pl.multiple_of(t * tk, tk), tk)
            kk = jnp.concatenate([kn_ref[ks, sl], kr_ref[ks, :]], axis=1)
            return update(carry, kk, mv_ref[ks, sl])

        init = (jnp.full((2 * tq, 1), NEG, F32), jnp.zeros((2 * tq, 1), F32),
                jnp.zeros((2 * tq, LANES), F32))
        carry = lax.fori_loop(0, n_tiles, body, init)
        kk = jnp.concatenate([knc_ref[0, :, sl], krc_ref[0]], axis=1)
        m, l, acc = update(carry, kk, mvc_ref[0, :, sl])
        o = acc / l
        o_ref[:, sl] = _pair_out(o[:tq], o[tq:])


def _mla_lat(qn, qr, kn, kr, mv, knc, krc, mvc, *, batch, seq, tq, tk):
    n = qn.shape[0]
    steps = seq // tq
    qtok = lambda w: pl.BlockSpec((tq, w), lambda b, i: (b * steps + i, 0))
    ktok = lambda w: pl.BlockSpec((seq, w), lambda b, i: (b, 0))
    ctok = lambda a: pl.BlockSpec((1,) + a.shape[1:], lambda b, i: (b, 0, 0))
    return pl.pallas_call(
        functools.partial(_mla_lat_kernel, tk=tk),
        grid=(batch, steps),
        in_specs=[qtok(512), qtok(256), ktok(512), ktok(LANES), ktok(512), ctok(knc), ctok(krc), ctok(mvc)],
        out_specs=qtok(512),
        out_shape=jax.ShapeDtypeStruct((n, 512), F32),
        compiler_params=_params(("arbitrary", "arbitrary")),
        name="mla_lat",
    )(qn, qr, kn, kr, mv, knc, krc, mvc)


def _cache_kv_kernel(ckv_ref, w_ref, kn_ref, mv_ref):
    kv = _dot(ckv_ref[0].astype(BF16), w_ref[0])
    kn_ref[0] = kv[:, :512].astype(BF16)
    mv_ref[0] = kv[:, 512:].astype(BF16)


def _cache_kv(ckv, wukv):
    rows = ckv.shape[1]
    spec = lambda a, b: pl.BlockSpec((1, a, b), lambda l: (l, 0, 0))
    return pl.pallas_call(
        _cache_kv_kernel,
        grid=(DEPTH,),
        in_specs=[spec(rows, KV_LORA), spec(KV_LORA, 1024)],
        out_specs=[spec(rows, 512), spec(rows, 512)],
        out_shape=[jax.ShapeDtypeStruct((DEPTH, rows, 512), BF16)] * 2,
        compiler_params=_params(("arbitrary",)),
        name="cache_kv",
    )(ckv, wukv)


def _dft_mats(n, scale):
    k = np.arange(n)
    ang = 2.0 * np.pi * ((k[:, None] * k[None, :]) % n) / n
    return np.cos(ang) * scale, np.sin(ang) * scale


def _hi_lo_np(m):
    m = jnp.asarray(m, F32)
    hi = m.astype(BF16)
    lo = (m - hi.astype(F32)).astype(BF16)
    return hi, lo


def _chan_dft(u, wc_ref):
    uh, ul = _split(u)
    wh = wc_ref[0]
    wl = wc_ref[1]
    return _dot(uh, wh) + _dot(ul, wh) + _dot(uh, wl)


def _fn_ctx_kernel(u_ref, wc_ref, fp_ref, o_ref):
    for g in range(FN_GROUPS):
        sl = slice(g * FN_GROUP_W, (g + 1) * FN_GROUP_W)
        a = _chan_dft(u_ref[:, sl], wc_ref)
        ar_h, ar_l = _split(a[:, :FN_GROUP_W])
        ai_h, ai_l = _split(a[:, FN_GROUP_W:])
        rh = jnp.concatenate([ar_h, ai_h], axis=0)
        rl = jnp.concatenate([ar_l, ai_l], axis=0)
        o_ref[:, sl] = _dot(fp_ref[0], rh) + _dot(fp_ref[0], rl) + _dot(fp_ref[1], rh)


def _fn_ctx(u, *, seq):
    n = u.shape[0]
    cc, sc = _dft_mats(FN_GROUP_W, FN_GROUP_W ** -0.5)
    wc = jnp.stack(_hi_lo_np(np.concatenate([cc, -sc], axis=1)))
    cp, sp = _dft_mats(seq, seq ** -0.5)
    fp = jnp.stack(_hi_lo_np(np.concatenate([cp, sp], axis=1)))
    full = lambda a: pl.BlockSpec(a.shape, lambda b: (0,) * a.ndim)
    return pl.pallas_call(
        _fn_ctx_kernel,
        grid=(n // seq,),
        in_specs=[pl.BlockSpec((seq, FN_WIDTH), lambda b: (b, 0)), full(wc), full(fp)],
        out_specs=pl.BlockSpec((seq, FN_WIDTH), lambda b: (b, 0)),
        out_shape=jax.ShapeDtypeStruct((n, FN_WIDTH), F32),
        compiler_params=_params(("arbitrary",)),
        name="fn_ctx",
    )(u, wc, fp)


_FN_G = 8


def _fn_lat_kernel(u_ref, wc_ref, f1_ref, f1l_ref, f2_ref, f2l_ref, twc_ref, tws_ref, o_ref, a_ref, t_ref):
    n = GRID_W
    w = 2 * FN_GROUP_W

    def chan(i, c):
        rows = pl.ds(pl.multiple_of(i * 512, 512), 512)
        a = _chan_dft(u_ref[rows, :], wc_ref)
        a_ref[0, rows, :] = a[:, :FN_GROUP_W]
        a_ref[1, rows, :] = a[:, FN_GROUP_W:]
        return c

    lax.fori_loop(0, u_ref.shape[0] // 512, chan, 0)

    def gather(ref, base):
        tiles = [ref[part, pl.ds(base + j, n, stride=n), :] for j in range(_FN_G) for part in range(2)]
        return jnp.concatenate(tiles, axis=1)

    def cs_products(fh_ref, fl_ref, d):
        dh, dl = _split(d)
        rh = _dot(fh_ref[...], dh)
        rl = _dot(fl_ref[...], dl)
        cd = rh[0:n] + rh[n:2 * n] + rl[0:n]
        sd = rh[2 * n:3 * n] + rh[3 * n:4 * n] + rl[n:2 * n]
        return cd, sd

    def stage1(i, c):
        base = i * _FN_G
        cd, sd = cs_products(f1_ref, f1l_ref, gather(a_ref, base))
        for j in range(_FN_G):
            cr = cd[:, j * w:j * w + FN_GROUP_W]
            ci = cd[:, j * w + FN_GROUP_W:(j + 1) * w]
            sr = sd[:, j * w:j * w + FN_GROUP_W]
            si = sd[:, j * w + FN_GROUP_W:(j + 1) * w]
            br = cr + si
            bi = ci - sr
            rows = pl.ds(pl.multiple_of((base + j) * n, n), n)
            tc = twc_ref[rows, :]
            ts = tws_ref[rows, :]
            t_ref[0, rows, :] = br * tc + bi * ts
            t_ref[1, rows, :] = bi * tc - br * ts
        return c

    lax.fori_loop(0, n // _FN_G, stage1, 0)

    def stage2(i, c):
        base = i * _FN_G
        cd, sd = cs_products(f2_ref, f2l_ref, gather(t_ref, base))
        for j in range(_FN_G):
            y = cd[:, j * w:j * w + FN_GROUP_W] + sd[:, j * w + FN_GROUP_W:(j + 1) * w]
            o_ref[pl.ds(base + j, n, stride=n), :] = y
        return c

    lax.fori_loop(0, n // _FN_G, stage2, 0)


def _fn_lat(u, *, batch, seq):
    n = GRID_W
    assert seq == n * n
    cc, sc = _dft_mats(FN_GROUP_W, FN_GROUP_W ** -0.5)
    wc = jnp.stack(_hi_lo_np(np.concatenate([cc, -sc], axis=1)))
    c1, s1 = _dft_mats(n, 1.0)
    c2, s2 = _dft_mats(n, 1.0 / n)
    c1h, c1l = _hi_lo_np(c1)
    s1h, s1l = _hi_lo_np(s1)
    c2h, c2l = _hi_lo_np(c2)
    s2h, s2l = _hi_lo_np(s2)
    f1 = jnp.concatenate([c1h, c1l, s1h, s1l], axis=0)
    f1l = jnp.concatenate([c1h, s1h], axis=0)
    f2 = jnp.concatenate([c2h, c2l, s2h, s2l], axis=0)
    f2l = jnp.concatenate([c2h, s2h], axis=0)
    n2 = np.arange(n)[:, None]
    k1 = np.arange(n)[None, :]
    ang = (2.0 * np.pi * ((n2 * k1) % seq) / seq).reshape(seq, 1)
    twc = jnp.asarray(np.broadcast_to(np.cos(ang), (seq, FN_GROUP_W)), F32)
    tws = jnp.asarray(np.broadcast_to(np.sin(ang), (seq, FN_GROUP_W)), F32)
    full = lambda a: pl.BlockSpec(a.shape, lambda b, g: (0,) * a.ndim)
    blk = pl.BlockSpec((seq, FN_GROUP_W), lambda b, g: (b, g))
    return pl.pallas_call(
        _fn_lat_kernel,
        grid=(batch, FN_GROUPS),
        in_specs=[blk, full(wc), full(f1), full(f1l), full(f2), full(f2l), full(twc), full(tws)],
        out_specs=blk,
        out_shape=jax.ShapeDtypeStruct(u.shape, F32),
        scratch_shapes=[pltpu.VMEM((2, seq, FN_GROUP_W), F32), pltpu.VMEM((2, seq, FN_GROUP_W), F32)],
        compiler_params=_params(("arbitrary", "arbitrary")),
        name="fn_lat",
    )(u, wc, f1, f1l, f2, f2l, twc, tws)


_E_GNA, _E_GMLA, _E_GFN, _E_MERGE = 0, 512, 1024, 1536


def _merge_kernel(*refs, final):
    (x_ref, mod_ref, ng_ref, wg_ref, ona_ref, omla_ref, ofn_ref, wona_ref, womla_ref, wofn_ref,
     wout_ref) = refs[:11]
    if final:
        fg_ref, o_ref = refs[11:13]
    else:
        o_ref = refs[11]
    x, xm = _modulated(x_ref, mod_ref, ng_ref)
    gate = mod_ref[0, 2:3, :]

    merged = None
    for i, (br_ref, wo_ref) in enumerate(((ona_ref, wona_ref), (omla_ref, womla_ref), (ofn_ref, wofn_ref))):
        g = _dot(xm, wg_ref[:, i * 512:(i + 1) * 512])
        t = _dot((br_ref[...] * _silu(g)).astype(BF16), wo_ref[...])
        ml = _dot(xm, wg_ref[:, _E_MERGE + i * D_MODEL:_E_MERGE + (i + 1) * D_MODEL])
        t = _sigmoid(ml) * t
        merged = t if merged is None else merged + t
    h = x + gate * _dot(merged.astype(BF16), wout_ref[...])
    if final:
        h = _rms(h, fg_ref[...])
    o_ref[...] = h


def _merge(x, mod, ng, w_g, ona, omla, ofn, wona, womla, wofn, wout, fg, *, tile):
    n = x.shape[0]
    groups = mod.shape[0]
    steps = n // tile
    per_group = steps // groups
    final = fg is not None
    tok = lambda w: pl.BlockSpec((tile, w), lambda i: (i, 0))
    full = lambda a: pl.BlockSpec(a.shape, lambda i: (0,) * a.ndim)
    in_specs = [tok(D_MODEL), pl.BlockSpec((1, 3, D_MODEL), lambda i: (i // per_group, 0, 0)),
                full(ng), full(w_g), tok(512), tok(512), tok(512),
                full(wona), full(womla), full(wofn), full(wout)]
    args = [x, mod, ng, w_g, ona, omla, ofn, wona, womla, wofn, wout]
    if final:
        in_specs.append(full(fg))
        args.append(fg)
    return pl.pallas_call(
        functools.partial(_merge_kernel, final=final),
        grid=(steps,),
        in_specs=in_specs,
        out_specs=tok(D_MODEL),
        out_shape=jax.ShapeDtypeStruct((n, D_MODEL), F32),
        compiler_params=_params(("arbitrary",)),
        name="merge",
    )(*args)


def _rope_rot_cols():
    q = MLA_ROPE // 4
    perm = np.concatenate([np.arange(q, 2 * q), np.arange(0, q), np.arange(3 * q, 4 * q), np.arange(2 * q, 3 * q)])
    sign = np.concatenate([-np.ones(q), np.ones(q), -np.ones(q), np.ones(q)]).astype(np.float32)
    return perm, sign


def _prep_w_in(w):
    perm, sign = _rope_rot_cols()
    kr = w[:, _O_KROPE:_O_KROPE + MLA_ROPE]
    kr_rot = kr[:, perm] * sign
    a = jnp.concatenate([
        w[:, _O_QKV:_O_QKV + 1536], w[:, _O_QLAT:_O_QLAT + Q_LORA], w[:, _O_CKV:_O_CKV + KV_LORA],
        jnp.tile(kr, (1, 4)), jnp.tile(kr_rot, (1, 4)), w[:, _O_UFN:_O_UFN + FN_WIDTH]], axis=1)
    e = jnp.concatenate([
        w[:, _O_GATE_NA:_O_GATE_NA + 512], w[:, _O_GATE_MLA:_O_GATE_MLA + 512],
        w[:, _O_GATE_FN:_O_GATE_FN + 512], w[:, _O_MERGE:]], axis=1)
    return a.astype(BF16), e.astype(BF16)


def _prep_w_uq(w):
    perm, sign = _rope_rot_cols()
    w3 = w.reshape(Q_LORA, MLA_HEADS, MLA_NOPE + MLA_ROPE)
    nope = w3[:, :, :MLA_NOPE].reshape(Q_LORA, MLA_HEADS * MLA_NOPE)
    rope = w3[:, :, MLA_NOPE:]
    rot = rope[:, :, perm] * sign
    return jnp.concatenate([nope, rope.reshape(Q_LORA, -1), rot.reshape(Q_LORA, -1)], axis=1).astype(BF16)


def _prep_w_ukv(w):
    w3 = w.reshape(KV_LORA, MLA_HEADS, MLA_NOPE + MLA_V)
    return jnp.concatenate([w3[:, :, :MLA_NOPE].reshape(KV_LORA, -1),
                            w3[:, :, MLA_NOPE:].reshape(KV_LORA, -1)], axis=1).astype(BF16)


def _rope_tables(n):
    t = jnp.arange(n, dtype=jnp.int32)
    row = (t // GRID_W).astype(F32)
    col = (t % GRID_W).astype(F32)
    half = MLA_ROPE // 2
    inv_freq = ROPE_THETA ** (-jnp.arange(0, half, 2, dtype=F32) / half)
    ar = row[:, None] * inv_freq[None, :]
    ac = col[:, None] * inv_freq[None, :]
    ang = jnp.concatenate([ar, ar, ac, ac], axis=-1)
    return jnp.tile(jnp.cos(ang), (1, MLA_HEADS)), jnp.tile(jnp.sin(ang), (1, MLA_HEADS))


def kernel(x_prompt, x_sample, cache_na_k, cache_na_v, cache_mla_ckv, cache_mla_krope, c, c_ctx,
           w_ada, b_ada, norm_g, w_in, q_norm_g, kv_norm_g, w_uq, w_ukv, na_bias,
           w_o_na, w_o_mla, w_o_fourier, w_out, final_norm_g):
    batch, seq, _ = x_prompt.shape
    dbatch, dseq, _ = x_sample.shape
    past = cache_na_k.shape[2]

    cond = jnp.zeros((8, D_MODEL), F32).at[0].set(c_ctx).at[1:1 + dbatch].set(c)
    mods = _ada_mods(cond, w_ada, b_ada).reshape(DEPTH, 8, 3, D_MODEL)

    wukv_all = jnp.stack([_prep_w_ukv(w_ukv[l]) for l in range(DEPTH)])
    ckv_c = cache_mla_ckv.transpose(1, 0, 2, 3).reshape(DEPTH, dbatch * past, KV_LORA)
    knc_all, mvc_all = _cache_kv(ckv_c, wukv_all)
    krc_all = jnp.tile(cache_mla_krope, (1, 1, 1, 4)).astype(BF16)
    nakc_all = cache_na_k.reshape(dbatch, DEPTH, past, NA_WIDTH).astype(BF16)
    navc_all = cache_na_v.reshape(dbatch, DEPTH, past, NA_WIDTH).astype(BF16)
    cos, sin = _rope_tables(dseq)

    h_ctx = x_prompt.reshape(batch * seq, D_MODEL)
    h_lat = x_sample.reshape(dbatch * dseq, D_MODEL)
    ks, vs, ckvs, krs = [], [], [], []
    row = lambda a: a.reshape(1, -1)
    for l in range(DEPTH):
        w_a, w_e = _prep_w_in(w_in[l])
        wuq = _prep_w_uq(w_uq[l])
        wukv = wukv_all[l]
        wona, womla, wofn, wout = (w_o_na[l].astype(BF16), w_o_mla[l].astype(BF16),
                                   w_o_fourier[l].astype(BF16), w_out[l].astype(BF16))
        ng, qng, kvng = row(norm_g[l]), row(q_norm_g[l]), row(kv_norm_g[l])
        fg = row(final_norm_g) if l == DEPTH - 1 else None
        mod_ctx = mods[l, 0:1]
        mod_lat = mods[l, 1:1 + dbatch]

        (q, k, v, qn, qr, kn, kr, mv, u, kf, vf, ckvf, krf) = _inproj(
            h_ctx, mod_ctx, ng, w_a, qng, kvng, wuq, wukv, None, None, tile=512, emit_cache=True)
        ks.append(kf)
        vs.append(vf)
        ckvs.append(ckvf)
        krs.append(krf)
        ona, omla = _ctx_attn(q, k, v, qn, qr, kn, kr, mv, seq=seq)
        ofn = _fn_ctx(u, seq=seq)
        h_ctx = _merge(h_ctx, mod_ctx, ng, w_e, ona, omla, ofn, wona, womla, wofn, wout, fg, tile=512)

        (q, k, v, qn, qr, kn, kr, mv, u) = _inproj(
            h_lat, mod_lat, ng, w_a, qng, kvng, wuq, wukv, cos, sin, tile=512, emit_cache=False)
        ona = _na_lat(q, k, v, nakc_all[:, l], navc_all[:, l], _na_bias_table(na_bias[l]),
                      batch=dbatch, seq=dseq)
        omla = _mla_lat(qn, qr, kn, kr, mv,
                        knc_all[l].reshape(dbatch, past, 512), krc_all[:, l],
                        mvc_all[l].reshape(dbatch, past, 512), batch=dbatch, seq=dseq, tq=256, tk=512)
        ofn = _fn_lat(u, batch=dbatch, seq=dseq)
        h_lat = _merge(h_lat, mod_lat, ng, w_e, ona, omla, ofn, wona, womla, wofn, wout, fg, tile=512)

    y_prompt = h_ctx.reshape(batch, seq, D_MODEL)
    y_sample = h_lat.reshape(dbatch, dseq, D_MODEL)
    stack = lambda xs, shape: jnp.stack([a.reshape(batch, seq, -1) for a in xs], axis=1).reshape(shape)
    new_na_k = stack(ks, (batch, DEPTH, seq, NA_HEADS, NA_HEAD_DIM))
    new_na_v = stack(vs, (batch, DEPTH, seq, NA_HEADS, NA_HEAD_DIM))
    new_mla_ckv = stack(ckvs, (batch, DEPTH, seq, KV_LORA))
    new_mla_krope = stack(krs, (batch, DEPTH, seq, MLA_ROPE))
    return (y_prompt, y_sample, new_na_k, new_na_v, new_mla_ckv, new_mla_krope)
```

```python
import functools

import jax
import jax.numpy as jnp
import numpy as np
from jax import lax
from jax.experimental import pallas as pl
from jax.experimental.pallas import tpu as pltpu

F32 = jnp.float32
BF16 = jnp.bfloat16

D_MODEL = 1024
DEPTH = 4
GRID_W = 64
NA_HEADS = 8
NA_HEAD_DIM = 64
NA_WIDTH = 512
NA_ROWS = 8
NA_COLS = 16
MLA_HEADS = 8
MLA_NOPE = 64
MLA_ROPE = 32
MLA_V = 64
MLA_WIDTH = 512
Q_LORA = 256
KV_LORA = 128
MLA_SCALE = (MLA_NOPE + MLA_ROPE) ** -0.5
ROPE_THETA = 10000.0
FN_GROUPS = 4
FN_GROUP_W = 128
FN_WIDTH = 512
EPS = 1e-6
NEG = -1e30

_O_QKV = 0
_O_GATE_NA = 1536
_O_QLAT = 2048
_O_CKV = 2304
_O_KROPE = 2432
_O_GATE_MLA = 2464
_O_UFN = 2976
_O_GATE_FN = 3488
_O_MERGE = 4000
D_IN = 7072

LANES = 128
VMEM_LIMIT = 56 * 1024 * 1024


def _params(sem):
    return pltpu.CompilerParams(dimension_semantics=sem, vmem_limit_bytes=VMEM_LIMIT)


def _sigmoid(x):
    return 1.0 / (1.0 + jnp.exp(-x))


def _silu(x):
    return x * _sigmoid(x)


def _rms(x, g):
    return x * lax.rsqrt(jnp.mean(x * x, axis=-1, keepdims=True) + EPS) * g


def _dot(a, b):
    return jnp.dot(a, b, preferred_element_type=F32)


def _dot_nt(a, b):
    return lax.dot_general(a, b, (((1,), (1,)), ((), ())), preferred_element_type=F32)


def _split(x):
    hi = x.astype(BF16)
    lo = (x - hi.astype(F32)).astype(BF16)
    return hi, lo


def _ada_kernel(cond_ref, w_ref, b_ref, o_ref):
    a = _silu(cond_ref[...]).astype(BF16)
    o_ref[0] = _dot(a, w_ref[0].astype(BF16)) + b_ref[0]


def _ada_mods(cond, w_ada, b_ada):
    nb = 3
    return pl.pallas_call(
        _ada_kernel,
        grid=(DEPTH, nb),
        in_specs=[
            pl.BlockSpec((8, D_MODEL), lambda l, j: (0, 0)),
            pl.BlockSpec((1, D_MODEL, D_MODEL), lambda l, j: (l, 0, j)),
            pl.BlockSpec((1, 1, D_MODEL), lambda l, j: (l, 0, j)),
        ],
        out_specs=pl.BlockSpec((1, 8, D_MODEL), lambda l, j: (l, 0, j)),
        out_shape=jax.ShapeDtypeStruct((DEPTH, 8, 3 * D_MODEL), F32),
        compiler_params=_params(("arbitrary", "arbitrary")),
        name="ada_mods",
    )(cond, w_ada, b_ada.reshape(DEPTH, 1, 3 * D_MODEL))


_A_Q, _A_K, _A_V, _A_QLAT, _A_CKV, _A_KR, _A_KRROT, _A_U, _A_END = (
    0, 512, 1024, 1536, 1792, 1920, 2048, 2176, 2688)


def _modulated(x_ref, mod_ref, ng_ref):
    x = x_ref[...]
    shift = mod_ref[0, 0:1, :]
    scale = mod_ref[0, 1:2, :]
    return x, (_rms(x, ng_ref[...]) * (1.0 + scale) + shift).astype(BF16)


KV_CHUNK = 256
HEAD_DIM = 64
ONES_ROWS = 16
VT_ROWS = HEAD_DIM + ONES_ROWS
VT_ALL = 8 * VT_ROWS


def _store_vt(vt_ref, vt):
    for c in range(vt_ref.shape[0]):
        cols = slice(c * KV_CHUNK, (c + 1) * KV_CHUNK)
        for h in range(8):
            vt_ref[c, h * VT_ROWS:h * VT_ROWS + HEAD_DIM, :] = vt[h * HEAD_DIM:(h + 1) * HEAD_DIM, cols].astype(BF16)
            vt_ref[c, h * VT_ROWS + HEAD_DIM:(h + 1) * VT_ROWS, :] = jnp.ones((ONES_ROWS, KV_CHUNK), BF16)


def _inproj_kernel(*refs, latent):
    (x_ref, mod_ref, ng_ref, w_ref, qng_ref, kvng_ref, wuq_ref, wukv_ref) = refs[:8]
    if latent:
        (wvt_ref, wmvt_ref, cos_ref, sin_ref,
         naq_ref, nak_ref, navt_ref, qn_ref, qr_ref, kn_ref, kr_ref, mvt_ref, u_ref) = refs[8:]
    else:
        (naq_ref, nak_ref, nav_ref, qn_ref, qr_ref, kn_ref, kr_ref, mv_ref, u_ref,
         kf_ref, vf_ref, ckvf_ref, krf_ref) = refs[8:]

    _, xm = _modulated(x_ref, mod_ref, ng_ref)

    def proj(a, b):
        return _dot(xm, w_ref[:, a:b])

    q = proj(_A_Q, _A_K)
    naq_ref[...] = (q * (NA_HEAD_DIM ** -0.5)).astype(BF16)
    k = proj(_A_K, _A_V)
    nak_ref[...] = k.astype(BF16)
    if latent:
        _store_vt(navt_ref, _dot_nt(wvt_ref[...], xm))
    else:
        v = proj(_A_V, _A_QLAT)
        nav_ref[...] = v.astype(BF16)
        kf_ref[...] = k
        vf_ref[...] = v

    qlat = _rms(proj(_A_QLAT, _A_CKV), qng_ref[...]).astype(BF16)
    qq = _dot(qlat, wuq_ref[...])
    qn_ref[...] = qq[:, :512].astype(BF16)
    qr = qq[:, 512:768]
    if latent:
        qr = qr * cos_ref[...] + qq[:, 768:1024] * sin_ref[...]
    qr_ref[...] = qr.astype(BF16)

    ckv = _rms(proj(_A_CKV, _A_KR), kvng_ref[...])
    ckv_b = ckv.astype(BF16)
    kn_ref[...] = _dot(ckv_b, wukv_ref[:, :512]).astype(BF16)
    if latent:
        _store_vt(mvt_ref, _dot_nt(wmvt_ref[...], ckv_b))
    else:
        ckvf_ref[...] = ckv
        mv_ref[...] = _dot(ckv_b, wukv_ref[:, 512:]).astype(BF16)

    kr = proj(_A_KR, _A_KRROT)
    if latent:
        kr = kr * cos_ref[:, :LANES] + proj(_A_KRROT, _A_U) * sin_ref[:, :LANES]
    else:
        krf_ref[...] = kr[:, :MLA_ROPE]
    kr_ref[...] = kr.astype(BF16)

    u_ref[...] = proj(_A_U, _A_END)


def _inproj(x, mod, ng, w_a, qng, kvng, wuq, wukv, lat_extra, *, tile):
    n = x.shape[0]
    groups = mod.shape[0]
    steps = n // tile
    per_group = steps // groups
    latent = lat_extra is not None
    tok = lambda w: pl.BlockSpec((tile, w), lambda i: (i, 0))
    full = lambda a: pl.BlockSpec(a.shape, lambda i: (0,) * a.ndim)
    in_specs = [tok(D_MODEL),
                pl.BlockSpec((1, 3, D_MODEL), lambda i: (i // per_group, 0, 0)),
                full(ng), full(w_a), full(qng), full(kvng), full(wuq), full(wukv)]
    args = [x, mod, ng, w_a, qng, kvng, wuq, wukv]
    out = lambda w, dt: (jax.ShapeDtypeStruct((n, w), dt), tok(w))
    if latent:
        wvt, wmvt, cos, sin = lat_extra
        steps_per_seq = cos.shape[0] // tile
        in_specs += [full(wvt), full(wmvt)] + [pl.BlockSpec((tile, 256), lambda i: (i % steps_per_seq, 0))] * 2
        args += [wvt, wmvt, cos, sin]
        cpt = tile // KV_CHUNK
        vt = (jax.ShapeDtypeStruct((n // KV_CHUNK, VT_ALL, KV_CHUNK), BF16),
              pl.BlockSpec((cpt, VT_ALL, KV_CHUNK), lambda i: (i, 0, 0)))
        outs = [out(512, BF16), out(512, BF16), vt, out(512, BF16), out(256, BF16), out(512, BF16),
                out(LANES, BF16), vt, out(512, F32)]
    else:
        outs = [out(512, BF16)] * 3 + [out(512, BF16), out(256, BF16), out(512, BF16), out(LANES, BF16),
                                      out(512, BF16), out(512, F32),
                                      out(512, F32), out(512, F32), out(KV_LORA, F32), out(MLA_ROPE, F32)]
    return pl.pallas_call(
        functools.partial(_inproj_kernel, latent=latent),
        grid=(steps,),
        in_specs=in_specs,
        out_specs=[o[1] for o in outs],
        out_shape=[o[0] for o in outs],
        compiler_params=_params(("arbitrary",)),
        name="inproj_lat" if latent else "inproj_ctx",
    )(*args)


def _lane_iota(n=LANES):
    return lax.broadcasted_iota(jnp.int32, (1, n), 1)


def _head_q(q_pair, half):
    lane = _lane_iota()
    keep = (lane < NA_HEAD_DIM) if half == 0 else (lane >= NA_HEAD_DIM)
    return jnp.where(keep, q_pair, jnp.zeros_like(q_pair))


def _mla_q(qn_ref, qr_ref, h):
    p, half = divmod(h, 2)
    qa = _head_q(qn_ref[:, p * LANES:(p + 1) * LANES], half)
    g, slot = divmod(h, 4)
    qb = qr_ref[:, g * LANES:(g + 1) * LANES]
    lane = _lane_iota()
    keep = (lane >= slot * MLA_ROPE) & (lane < (slot + 1) * MLA_ROPE)
    qb = jnp.where(keep, qb, jnp.zeros_like(qb))
    return jnp.concatenate([qa, qb], axis=1)


def _pair_out(o0, o1):
    return jnp.where(_lane_iota() < NA_HEAD_DIM, o0, o1)


def _softmax_pv(s_list, v_list):
    m = s_list[0].max(axis=-1, keepdims=True)
    for s in s_list[1:]:
        m = jnp.maximum(m, s.max(axis=-1, keepdims=True))
    l = None
    o = None
    for s, v in zip(s_list, v_list):
        e = jnp.exp(s - m)
        ls = e.sum(axis=-1, keepdims=True)
        pv = _dot(e.astype(BF16), v)
        l = ls if l is None else l + ls
        o = pv if o is None else o + pv
    return o / l


def _ctx_attn_kernel(q_ref, k_ref, v_ref, qn_ref, qr_ref, kn_ref, kr_ref, mv_ref, ona_ref, omla_ref):
    for p in range(NA_HEADS // 2):
        sl = slice(p * LANES, (p + 1) * LANES)
        kp = k_ref[:, sl]
        vp = v_ref[:, sl]
        outs = []
        for half in range(2):
            s = _dot_nt(_head_q(q_ref[:, sl], half), kp)
            outs.append(_softmax_pv([s], [vp]))
        ona_ref[:, sl] = _pair_out(*outs)

        kk = jnp.concatenate([kn_ref[:, sl], kr_ref[...]], axis=1)
        mvp = mv_ref[:, sl]
        outs = []
        for half in range(2):
            s = _dot_nt(_mla_q(qn_ref, qr_ref, 2 * p + half), kk) * MLA_SCALE
            outs.append(_softmax_pv([s], [mvp]))
        omla_ref[:, sl] = _pair_out(*outs)


def _ctx_attn(q, k, v, qn, qr, kn, kr, mv, *, seq):
    n = q.shape[0]
    tok = lambda w: pl.BlockSpec((seq, w), lambda b: (b, 0))
    return pl.pallas_call(
        _ctx_attn_kernel,
        grid=(n // seq,),
        in_specs=[tok(512), tok(512), tok(512), tok(512), tok(256), tok(512), tok(LANES), tok(512)],
        out_specs=[tok(512), tok(512)],
        out_shape=[jax.ShapeDtypeStruct((n, 512), F32)] * 2,
        compiler_params=_params(("arbitrary",)),
        name="ctx_attn",
    )(q, k, v, qn, qr, kn, kr, mv)


NA_QROWS = 4
NA_KROWS = 12
_NDR = 2 * NA_ROWS - 1
_NDC = 2 * NA_COLS - 1
_Z_NONE = 48
LOG2E = 1.4426950408889634


def _na_tile_descriptors(rows):
    last_q0 = rows - NA_QROWS
    last_k0 = rows - NA_KROWS
    desc = np.full((3, NA_KROWS, NA_QROWS // 2), _Z_NONE, np.int32)
    for case, (q0, k0) in enumerate(((0, 0), (NA_QROWS, 0), (last_q0, last_k0))):
        for kri in range(NA_KROWS):
            for u in range(NA_QROWS // 2):
                info = []
                for ri in (2 * u, 2 * u + 1):
                    qrow, krow = q0 + ri, k0 + kri
                    start = min(max(qrow - NA_ROWS // 2, 0), rows - NA_ROWS)
                    info.append((start <= krow < start + NA_ROWS, krow - qrow))
                (vl, drl), (vr, _) = info
                d = drl + NA_ROWS - 1
                if vl and vr:
                    desc[case, kri, u] = d
                elif vl:
                    desc[case, kri, u] = 16 + d
                elif vr:
                    desc[case, kri, u] = 32 + d
    return desc.reshape(-1)


def _na_bias_consts():
    col = np.arange(GRID_W)
    col_start = np.clip(col - NA_COLS // 2, 0, GRID_W - NA_COLS)
    kc, qc = col[:, None], col[None, :]
    col_in = (kc >= col_start[None, :]) & (kc < col_start[None, :] + NA_COLS)
    dc = kc - qc + NA_COLS - 1
    d = np.stack([((dc == j) & col_in) for j in range(_NDC)]).astype(np.float32)
    negm = np.where(col_in, 0.0, NEG).astype(np.float32)
    return np.concatenate([d, d], axis=2), np.concatenate([negm, negm], axis=1)


def _na_build_bias(bias_ref, d2_ref, negm_ref, z_ref):
    left = _lane_iota() < GRID_W

    def coef(h, dr, j):
        if dr < -(NA_ROWS - 1) or dr > NA_ROWS - 1:
            return 0.0
        return bias_ref[(h * _NDR + dr + NA_ROWS - 1) * _NDC + j]

    def per_head(h, c):
        for d in range(16):
            dr_l = d - (NA_ROWS - 1)
            acc = negm_ref[...]
            for j in range(_NDC):
                acc = acc + jnp.where(left, coef(h, dr_l, j), coef(h, dr_l - 1, j)) * d2_ref[j]
            z_ref[h, d] = acc
            z_ref[h, 16 + d] = jnp.where(left, acc, NEG)
            z_ref[h, 32 + d] = jnp.where(left, NEG, acc)
        z_ref[h, _Z_NONE] = jnp.full((GRID_W, LANES), NEG, F32)
        return c

    lax.fori_loop(0, NA_HEADS, per_head, 0)


def _finish_pair(acc_a, acc_b):
    norm = lambda acc: acc[:HEAD_DIM] * (1.0 / acc[HEAD_DIM:HEAD_DIM + 1])
    return jnp.concatenate([norm(acc_a), norm(acc_b)], axis=0).T


def _na_lat_kernel(bias_ref, desc_ref, q_ref, k_ref, vt_ref, kc_ref, vtc_ref, d2_ref, negm_ref, o_ref, z_ref,
                   *, rows):
    g = pl.program_id(1)
    groups = rows // NA_QROWS

    @pl.when((pl.program_id(0) == 0) & (g == 0))
    def _():
        _na_build_bias(bias_ref, d2_ref, negm_ref, z_ref)

    tq = NA_QROWS * GRID_W
    k_row0 = jnp.clip(g * NA_QROWS - NA_ROWS // 2, 0, rows - NA_KROWS)
    start = pl.multiple_of(k_row0 * GRID_W, KV_CHUNK)
    chunk0 = k_row0 // (KV_CHUNK // GRID_W)
    n_loc = NA_KROWS * GRID_W
    case = jnp.where(g == 0, 0, jnp.where(g == groups - 1, 2, 1))
    upairs = NA_QROWS // 2

    def scores(h):
        sl = slice((h // 2) * LANES, (h // 2 + 1) * LANES)
        qh = _head_q(q_ref[:, sl], h % 2)
        return _dot_nt(k_ref[pl.ds(start, n_loc), sl], qh), _dot_nt(kc_ref[0, :, sl], qh)

    def attend(h, s_loc, s_ctx):
        rows_h = slice(h * VT_ROWS, (h + 1) * VT_ROWS)
        blocks = []
        for kri in range(NA_KROWS):
            tiles = []
            for u in range(upairs):
                idx = desc_ref[(case * NA_KROWS + kri) * upairs + u]
                tiles.append(s_loc[kri * GRID_W:(kri + 1) * GRID_W, u * LANES:(u + 1) * LANES] + z_ref[h, idx])
            blocks.append(jnp.concatenate(tiles, axis=1))
        s_loc = jnp.concatenate(blocks, axis=0)
        m = jnp.maximum(s_loc.max(axis=0, keepdims=True), s_ctx.max(axis=0, keepdims=True))
        p_loc = jnp.exp2((s_loc - m) * LOG2E).astype(BF16)
        p_ctx = jnp.exp2((s_ctx - m) * LOG2E).astype(BF16)
        acc = _dot(vtc_ref[0, rows_h, :], p_ctx)
        for i in range(n_loc // KV_CHUNK):
            acc = acc + _dot(vt_ref[chunk0 + i, rows_h, :], p_loc[i * KV_CHUNK:(i + 1) * KV_CHUNK])
        return acc

    s_next = scores(0)
    accs = []
    for h in range(NA_HEADS):
        s = s_next
        if h + 1 < NA_HEADS:
            s_next = scores(h + 1)
        accs.append(attend(h, *s))
        if h % 2:
            o_ref[:, (h // 2) * LANES:(h // 2 + 1) * LANES] = _finish_pair(accs[h - 1], accs[h])


def _na_lat(q, k, vt, kc, vtc, bias_l, *, batch, seq):
    rows = seq // GRID_W
    n = q.shape[0]
    groups = rows // NA_QROWS
    tq = NA_QROWS * GRID_W
    chunks = seq // KV_CHUNK
    d2, negm = _na_bias_consts()
    desc = jnp.asarray(_na_tile_descriptors(rows))
    smem = pl.BlockSpec(memory_space=pltpu.SMEM)
    full = lambda a: pl.BlockSpec(a.shape, lambda b, g: (0,) * a.ndim)
    return pl.pallas_call(
        functools.partial(_na_lat_kernel, rows=rows),
        grid=(batch, groups),
        in_specs=[
            smem, smem,
            pl.BlockSpec((tq, 512), lambda b, g: (b * groups + g, 0)),
            pl.BlockSpec((seq, 512), lambda b, g: (b, 0)),
            pl.BlockSpec((chunks, VT_ALL, KV_CHUNK), lambda b, g: (b, 0, 0)),
            pl.BlockSpec((1,) + kc.shape[1:], lambda b, g: (b, 0, 0)),
            pl.BlockSpec((1,) + vtc.shape[1:], lambda b, g: (b, 0, 0)),
            full(d2), full(negm),
        ],
        out_specs=pl.BlockSpec((tq, 512), lambda b, g: (b * groups + g, 0)),
        out_shape=jax.ShapeDtypeStruct((n, 512), F32),
        scratch_shapes=[pltpu.VMEM((NA_HEADS, _Z_NONE + 1, GRID_W, LANES), F32)],
        compiler_params=_params(("arbitrary", "arbitrary")),
        name="na_lat",
    )(bias_l.reshape(-1), desc, q, k, vt, kc, vtc, jnp.asarray(d2), jnp.asarray(negm))


_MLA_C = MLA_SCALE * LOG2E


def _mla_lat_kernel(qn_ref, qr_ref, kn_ref, kr_ref, vt_ref, knc_ref, krc_ref, vtc_ref, o_ref,
                    q2_ref, m_ref, acc_ref, s_ref, *, tk):
    tq = qn_ref.shape[0]
    n_tiles = kn_ref.shape[0] // tk
    cpt = tk // KV_CHUNK
    pairs = MLA_HEADS // 2
    for p in range(pairs):
        q2_ref[p] = jnp.concatenate([_mla_q(qn_ref, qr_ref, 2 * p), _mla_q(qn_ref, qr_ref, 2 * p + 1)], axis=0)
        m_ref[p] = jnp.full((1, 2 * tq), NEG, F32)
        acc_ref[p] = jnp.zeros((2 * VT_ROWS, 2 * tq), F32)

    def scores(p, kk):
        return _dot_nt(kk, q2_ref[p])

    def update(p, s, vts):
        m_prev = m_ref[p]
        m_new = jnp.maximum(m_prev, s.max(axis=0, keepdims=True))
        alpha = jnp.exp2((m_prev - m_new) * _MLA_C)
        pt = jnp.exp2((s - m_new) * _MLA_C).astype(BF16)
        acc = acc_ref[p] * alpha
        for i, vt in enumerate(vts):
            acc = acc + _dot(vt, pt[i * KV_CHUNK:(i + 1) * KV_CHUNK])
        return m_new, acc

    def store(new):
        for p, (m_new, acc) in enumerate(new):
            acc_ref[p] = acc
            m_ref[p] = m_new

    def lanes(p):
        return slice(p * LANES, (p + 1) * LANES)

    def rows(p):
        return slice(2 * p * VT_ROWS, 2 * (p + 1) * VT_ROWS)

    def keys(t, p):
        ks = pl.ds(pl.multiple_of(t * tk, tk), tk)
        return jnp.concatenate([kn_ref[ks, lanes(p)], kr_ref[ks, :]], axis=1)

    s_ref[...] = scores(0, keys(0, 0))

    def body(t, c):
        new = []
        s_next = s_ref[...]
        for p in range(pairs):
            s = s_next
            if p + 1 < pairs:
                s_next = scores(p + 1, keys(t, p + 1))
            else:
                s_next = scores(0, keys(jnp.minimum(t + 1, n_tiles - 1), 0))
            new.append(update(p, s, [vt_ref[t * cpt + i, rows(p), :] for i in range(cpt)]))
        s_ref[...] = s_next
        store(new)
        return c

    lax.fori_loop(0, n_tiles, body, 0)

    ctx_keys = lambda p: jnp.concatenate([knc_ref[0, :, lanes(p)], krc_ref[0]], axis=1)
    new = []
    s_next = scores(0, ctx_keys(0))
    for p in range(pairs):
        s = s_next
        if p + 1 < pairs:
            s_next = scores(p + 1, ctx_keys(p + 1))
        new.append(update(p, s, [vtc_ref[0, rows(p), :]]))
    for p, (_, acc) in enumerate(new):
        o_ref[:, lanes(p)] = _finish_pair(acc[:VT_ROWS, :tq], acc[VT_ROWS:, tq:])


def _mla_lat(qn, qr, kn, kr, vt, knc, krc, vtc, *, batch, seq, tq, tk):
    n = qn.shape[0]
    steps = seq // tq
    chunks = seq // KV_CHUNK
    assert knc.shape[1] == KV_CHUNK
    qtok = lambda w: pl.BlockSpec((tq, w), lambda b, i: (b * steps + i, 0))
    ktok = lambda w: pl.BlockSpec((seq, w), lambda b, i: (b, 0))
    ctok = lambda a: pl.BlockSpec((1,) + a.shape[1:], lambda b, i: (b, 0, 0))
    pairs = MLA_HEADS // 2
    return pl.pallas_call(
        functools.partial(_mla_lat_kernel, tk=tk),
        grid=(batch, steps),
        in_specs=[qtok(512), qtok(256), ktok(512), ktok(LANES),
                  pl.BlockSpec((chunks, VT_ALL, KV_CHUNK), lambda b, i: (b, 0, 0)),
                  ctok(knc), ctok(krc), ctok(vtc)],
        out_specs=qtok(512),
        out_shape=jax.ShapeDtypeStruct((n, 512), F32),
        scratch_shapes=[pltpu.VMEM((pairs, 2 * tq, 2 * LANES), BF16),
                        pltpu.VMEM((pairs, 1, 2 * tq), F32),
                        pltpu.VMEM((pairs, 2 * VT_ROWS, 2 * tq), F32),
                        pltpu.VMEM((tk, 2 * tq), F32)],
        compiler_params=_params(("arbitrary", "arbitrary")),
        name="mla_lat",
    )(qn, qr, kn, kr, vt, knc, krc, vtc)


def _cache_kv_kernel(ckv_ref, w_ref, wvt_ref, kn_ref, vt_ref):
    ckv = ckv_ref[0, 0].astype(BF16)
    kn_ref[0, 0] = _dot(ckv, w_ref[0]).astype(BF16)
    _store_vt(vt_ref.at[0], _dot_nt(wvt_ref[0], ckv))


def _cache_kv(ckv, wukv_k, wukv_vt):
    _, nb, rows, _ = ckv.shape
    assert rows == KV_CHUNK
    spec = lambda a, b: pl.BlockSpec((1, a, b), lambda l, j: (l, 0, 0))
    return pl.pallas_call(
        _cache_kv_kernel,
        grid=(DEPTH, nb),
        in_specs=[pl.BlockSpec((1, 1, rows, KV_LORA), lambda l, j: (l, j, 0, 0)),
                  spec(KV_LORA, 512), spec(512, KV_LORA)],
        out_specs=[pl.BlockSpec((1, 1, rows, 512), lambda l, j: (l, j, 0, 0)),
                   pl.BlockSpec((1, 1, VT_ALL, KV_CHUNK), lambda l, j: (l, j, 0, 0))],
        out_shape=[jax.ShapeDtypeStruct((DEPTH, nb, rows, 512), BF16),
                   jax.ShapeDtypeStruct((DEPTH, nb, VT_ALL, KV_CHUNK), BF16)],
        compiler_params=_params(("arbitrary", "arbitrary")),
        name="cache_kv",
    )(ckv, wukv_k, wukv_vt)


def _dft_mats(n, scale):
    k = np.arange(n)
    ang = 2.0 * np.pi * ((k[:, None] * k[None, :]) % n) / n
    return np.cos(ang) * scale, np.sin(ang) * scale


def _hi_lo_np(m):
    m = jnp.asarray(m, F32)
    hi = m.astype(BF16)
    lo = (m - hi.astype(F32)).astype(BF16)
    return hi, lo


def _chan_dft(u, wc_ref):
    uh, ul = _split(u)
    wh = wc_ref[0]
    wl = wc_ref[1]
    return _dot(uh, wh) + _dot(ul, wh) + _dot(uh, wl)


def _fn_ctx_kernel(u_ref, wc_ref, fp_ref, o_ref):
    for g in range(FN_GROUPS):
        sl = slice(g * FN_GROUP_W, (g + 1) * FN_GROUP_W)
        a = _chan_dft(u_ref[:, sl], wc_ref)
        ar_h, ar_l = _split(a[:, :FN_GROUP_W])
        ai_h, ai_l = _split(a[:, FN_GROUP_W:])
        rh = jnp.concatenate([ar_h, ai_h], axis=0)
        rl = jnp.concatenate([ar_l, ai_l], axis=0)
        o_ref[:, sl] = _dot(fp_ref[0], rh) + _dot(fp_ref[0], rl) + _dot(fp_ref[1], rh)


def _fn_ctx(u, *, seq):
    n = u.shape[0]
    cc, sc = _dft_mats(FN_GROUP_W, FN_GROUP_W ** -0.5)
    wc = jnp.stack(_hi_lo_np(np.concatenate([cc, -sc], axis=1)))
    cp, sp = _dft_mats(seq, seq ** -0.5)
    fp = jnp.stack(_hi_lo_np(np.concatenate([cp, sp], axis=1)))
    full = lambda a: pl.BlockSpec(a.shape, lambda b: (0,) * a.ndim)
    return pl.pallas_call(
        _fn_ctx_kernel,
        grid=(n // seq,),
        in_specs=[pl.BlockSpec((seq, FN_WIDTH), lambda b: (b, 0)), full(wc), full(fp)],
        out_specs=pl.BlockSpec((seq, FN_WIDTH), lambda b: (b, 0)),
        out_shape=jax.ShapeDtypeStruct((n, FN_WIDTH), F32),
        compiler_params=_params(("arbitrary",)),
        name="fn_ctx",
    )(u, wc, fp)


_FN_G = 8


def _fn_lat_kernel(u_ref, wc_ref, f1_ref, f1l_ref, f2_ref, f2l_ref, twc_ref, tws_ref, o_ref, a_ref, t_ref):
    n = GRID_W
    w = 2 * FN_GROUP_W

    def chan(i, c):
        rows = pl.ds(pl.multiple_of(i * 512, 512), 512)
        a = _chan_dft(u_ref[rows, :], wc_ref)
        a_ref[0, rows, :] = a[:, :FN_GROUP_W]
        a_ref[1, rows, :] = a[:, FN_GROUP_W:]
        return c

    lax.fori_loop(0, u_ref.shape[0] // 512, chan, 0)

    def gather(ref, base):
        tiles = [ref[part, pl.ds(base + j, n, stride=n), :] for j in range(_FN_G) for part in range(2)]
        return jnp.concatenate(tiles, axis=1)

    def cs_products(fh_ref, fl_ref, d):
        dh, dl = _split(d)
        rh = _dot(fh_ref[...], dh)
        rl = _dot(fl_ref[...], dl)
        cd = rh[0:n] + rh[n:2 * n] + rl[0:n]
        sd = rh[2 * n:3 * n] + rh[3 * n:4 * n] + rl[n:2 * n]
        return cd, sd

    def stage1(i, c):
        base = i * _FN_G
        cd, sd = cs_products(f1_ref, f1l_ref, gather(a_ref, base))
        for j in range(_FN_G):
            cr = cd[:, j * w:j * w + FN_GROUP_W]
            ci = cd[:, j * w + FN_GROUP_W:(j + 1) * w]
            sr = sd[:, j * w:j * w + FN_GROUP_W]
            si = sd[:, j * w + FN_GROUP_W:(j + 1) * w]
            br = cr + si
            bi = ci - sr
            rows = pl.ds(pl.multiple_of((base + j) * n, n), n)
            tc = twc_ref[rows, :]
            ts = tws_ref[rows, :]
            t_ref[0, rows, :] = br * tc + bi * ts
            t_ref[1, rows, :] = bi * tc - br * ts
        return c

    lax.fori_loop(0, n // _FN_G, stage1, 0)

    def stage2(i, c):
        base = i * _FN_G
        cd, sd = cs_products(f2_ref, f2l_ref, gather(t_ref, base))
        for j in range(_FN_G):
            y = cd[:, j * w:j * w + FN_GROUP_W] + sd[:, j * w + FN_GROUP_W:(j + 1) * w]
            o_ref[pl.ds(base + j, n, stride=n), :] = y
        return c

    lax.fori_loop(0, n // _FN_G, stage2, 0)


def _fn_lat(u, *, batch, seq):
    n = GRID_W
    assert seq == n * n
    cc, sc = _dft_mats(FN_GROUP_W, FN_GROUP_W ** -0.5)
    wc = jnp.stack(_hi_lo_np(np.concatenate([cc, -sc], axis=1)))
    c1, s1 = _dft_mats(n, 1.0)
    c2, s2 = _dft_mats(n, 1.0 / n)
    c1h, c1l = _hi_lo_np(c1)
    s1h, s1l = _hi_lo_np(s1)
    c2h, c2l = _hi_lo_np(c2)
    s2h, s2l = _hi_lo_np(s2)
    f1 = jnp.concatenate([c1h, c1l, s1h, s1l], axis=0)
    f1l = jnp.concatenate([c1h, s1h], axis=0)
    f2 = jnp.concatenate([c2h, c2l, s2h, s2l], axis=0)
    f2l = jnp.concatenate([c2h, s2h], axis=0)
    n2 = np.arange(n)[:, None]
    k1 = np.arange(n)[None, :]
    ang = (2.0 * np.pi * ((n2 * k1) % seq) / seq).reshape(seq, 1)
    twc = jnp.asarray(np.broadcast_to(np.cos(ang), (seq, FN_GROUP_W)), F32)
    tws = jnp.asarray(np.broadcast_to(np.sin(ang), (seq, FN_GROUP_W)), F32)
    full = lambda a: pl.BlockSpec(a.shape, lambda b, g: (0,) * a.ndim)
    blk = pl.BlockSpec((seq, FN_GROUP_W), lambda b, g: (b, g))
    return pl.pallas_call(
        _fn_lat_kernel,
        grid=(batch, FN_GROUPS),
        in_specs=[blk, full(wc), full(f1), full(f1l), full(f2), full(f2l), full(twc), full(tws)],
        out_specs=blk,
        out_shape=jax.ShapeDtypeStruct(u.shape, F32),
        scratch_shapes=[pltpu.VMEM((2, seq, FN_GROUP_W), F32), pltpu.VMEM((2, seq, FN_GROUP_W), F32)],
        compiler_params=_params(("arbitrary", "arbitrary")),
        name="fn_lat",
    )(u, wc, f1, f1l, f2, f2l, twc, tws)


_E_GNA, _E_GMLA, _E_GFN, _E_MERGE = 0, 512, 1024, 1536


def _merge_kernel(*refs, final):
    (x_ref, mod_ref, ng_ref, wg_ref, ona_ref, omla_ref, ofn_ref, wona_ref, womla_ref, wofn_ref,
     wout_ref) = refs[:11]
    if final:
        fg_ref, o_ref = refs[11:13]
    else:
        o_ref = refs[11]
    x, xm = _modulated(x_ref, mod_ref, ng_ref)
    gate = mod_ref[0, 2:3, :]

    merged = None
    for i, (br_ref, wo_ref) in enumerate(((ona_ref, wona_ref), (omla_ref, womla_ref), (ofn_ref, wofn_ref))):
        g = _dot(xm, wg_ref[:, i * 512:(i + 1) * 512])
        t = _dot((br_ref[...] * _silu(g)).astype(BF16), wo_ref[...])
        ml = _dot(xm, wg_ref[:, _E_MERGE + i * D_MODEL:_E_MERGE + (i + 1) * D_MODEL])
        t = _sigmoid(ml) * t
        merged = t if merged is None else merged + t
    h = x + gate * _dot(merged.astype(BF16), wout_ref[...])
    if final:
        h = _rms(h, fg_ref[...])
    o_ref[...] = h


def _merge(x, mod, ng, w_g, ona, omla, ofn, wona, womla, wofn, wout, fg, *, tile):
    n = x.shape[0]
    groups = mod.shape[0]
    steps = n // tile
    per_group = steps // groups
    final = fg is not None
    tok = lambda w: pl.BlockSpec((tile, w), lambda i: (i, 0))
    full = lambda a: pl.BlockSpec(a.shape, lambda i: (0,) * a.ndim)
    in_specs = [tok(D_MODEL), pl.BlockSpec((1, 3, D_MODEL), lambda i: (i // per_group, 0, 0)),
                full(ng), full(w_g), tok(512), tok(512), tok(512),
                full(wona), full(womla), full(wofn), full(wout)]
    args = [x, mod, ng, w_g, ona, omla, ofn, wona, womla, wofn, wout]
    if final:
        in_specs.append(full(fg))
        args.append(fg)
    return pl.pallas_call(
        functools.partial(_merge_kernel, final=final),
        grid=(steps,),
        in_specs=in_specs,
        out_specs=tok(D_MODEL),
        out_shape=jax.ShapeDtypeStruct((n, D_MODEL), F32),
        compiler_params=_params(("arbitrary",)),
        name="merge",
    )(*args)


def _rope_rot_cols():
    q = MLA_ROPE // 4
    perm = np.concatenate([np.arange(q, 2 * q), np.arange(0, q), np.arange(3 * q, 4 * q), np.arange(2 * q, 3 * q)])
    sign = np.concatenate([-np.ones(q), np.ones(q), -np.ones(q), np.ones(q)]).astype(np.float32)
    return perm, sign


def _prep_w_in(w):
    perm, sign = _rope_rot_cols()
    kr = w[:, _O_KROPE:_O_KROPE + MLA_ROPE]
    kr_rot = kr[:, perm] * sign
    a = jnp.concatenate([
        w[:, _O_QKV:_O_QKV + 1536], w[:, _O_QLAT:_O_QLAT + Q_LORA], w[:, _O_CKV:_O_CKV + KV_LORA],
        jnp.tile(kr, (1, 4)), jnp.tile(kr_rot, (1, 4)), w[:, _O_UFN:_O_UFN + FN_WIDTH]], axis=1)
    e = jnp.concatenate([
        w[:, _O_GATE_NA:_O_GATE_NA + 512], w[:, _O_GATE_MLA:_O_GATE_MLA + 512],
        w[:, _O_GATE_FN:_O_GATE_FN + 512], w[:, _O_MERGE:]], axis=1)
    return a.astype(BF16), e.astype(BF16)


def _prep_w_uq(w):
    perm, sign = _rope_rot_cols()
    w3 = w.reshape(Q_LORA, MLA_HEADS, MLA_NOPE + MLA_ROPE)
    nope = w3[:, :, :MLA_NOPE].reshape(Q_LORA, MLA_HEADS * MLA_NOPE)
    rope = w3[:, :, MLA_NOPE:]
    rot = rope[:, :, perm] * sign
    return jnp.concatenate([nope, rope.reshape(Q_LORA, -1), rot.reshape(Q_LORA, -1)], axis=1).astype(BF16)


def _prep_w_ukv(w):
    w3 = w.reshape(KV_LORA, MLA_HEADS, MLA_NOPE + MLA_V)
    return jnp.concatenate([w3[:, :, :MLA_NOPE].reshape(KV_LORA, -1),
                            w3[:, :, MLA_NOPE:].reshape(KV_LORA, -1)], axis=1).astype(BF16)


def _rope_tables(n):
    t = jnp.arange(n, dtype=jnp.int32)
    row = (t // GRID_W).astype(F32)
    col = (t % GRID_W).astype(F32)
    half = MLA_ROPE // 2
    inv_freq = ROPE_THETA ** (-jnp.arange(0, half, 2, dtype=F32) / half)
    ar = row[:, None] * inv_freq[None, :]
    ac = col[:, None] * inv_freq[None, :]
    ang = jnp.concatenate([ar, ar, ac, ac], axis=-1)
    return jnp.tile(jnp.cos(ang), (1, MLA_HEADS)), jnp.tile(jnp.sin(ang), (1, MLA_HEADS))


def kernel(x_prompt, x_sample, cache_na_k, cache_na_v, cache_mla_ckv, cache_mla_krope, c, c_ctx,
           w_ada, b_ada, norm_g, w_in, q_norm_g, kv_norm_g, w_uq, w_ukv, na_bias,
           w_o_na, w_o_mla, w_o_fourier, w_out, final_norm_g):
    batch, seq, _ = x_prompt.shape
    dbatch, dseq, _ = x_sample.shape
    past = cache_na_k.shape[2]

    cond = jnp.zeros((8, D_MODEL), F32).at[0].set(c_ctx).at[1:1 + dbatch].set(c)
    mods = _ada_mods(cond, w_ada, b_ada).reshape(DEPTH, 8, 3, D_MODEL)

    wukv_all = jnp.stack([_prep_w_ukv(w_ukv[l]) for l in range(DEPTH)])
    wukv_vt_all = wukv_all[:, :, 512:].transpose(0, 2, 1)
    knc_all, mvtc_all = _cache_kv(cache_mla_ckv.transpose(1, 0, 2, 3), wukv_all[:, :, :512], wukv_vt_all)
    krc_all = jnp.tile(cache_mla_krope, (1, 1, 1, 4)).astype(BF16)
    nakc_all = cache_na_k.reshape(dbatch, DEPTH, past, NA_WIDTH).astype(BF16)
    navt = cache_na_v.transpose(0, 1, 3, 4, 2).astype(BF16)
    navtc_all = jnp.concatenate([navt, jnp.ones((dbatch, DEPTH, NA_HEADS, ONES_ROWS, past), BF16)],
                                axis=3).reshape(dbatch, DEPTH, VT_ALL, past)
    cos, sin = _rope_tables(dseq)

    h_ctx = x_prompt.reshape(batch * seq, D_MODEL)
    h_lat = x_sample.reshape(dbatch * dseq, D_MODEL)
    ks, vs, ckvs, krs = [], [], [], []
    row = lambda a: a.reshape(1, -1)
    for l in range(DEPTH):
        w_a, w_e = _prep_w_in(w_in[l])
        wuq = _prep_w_uq(w_uq[l])
        wukv = wukv_all[l]
        wona, womla, wofn, wout = (w_o_na[l].astype(BF16), w_o_mla[l].astype(BF16),
                                   w_o_fourier[l].astype(BF16), w_out[l].astype(BF16))
        ng, qng, kvng = row(norm_g[l]), row(q_norm_g[l]), row(kv_norm_g[l])
        fg = row(final_norm_g) if l == DEPTH - 1 else None
        mod_ctx = mods[l, 0:1]
        mod_lat = mods[l, 1:1 + dbatch]

        (q, k, v, qn, qr, kn, kr, mv, u, kf, vf, ckvf, krf) = _inproj(
            h_ctx, mod_ctx, ng, w_a, qng, kvng, wuq, wukv, None, tile=512)
        ks.append(kf)
        vs.append(vf)
        ckvs.append(ckvf)
        krs.append(krf)
        ona, omla = _ctx_attn(q, k, v, qn, qr, kn, kr, mv, seq=seq)
        ofn = _fn_ctx(u, seq=seq)
        h_ctx = _merge(h_ctx, mod_ctx, ng, w_e, ona, omla, ofn, wona, womla, wofn, wout, fg, tile=512)

        wvt = w_in[l][:, _O_QKV + 1024:_O_QKV + 1536].T.astype(BF16)
        (q, k, vt, qn, qr, kn, kr, mvt, u) = _inproj(
            h_lat, mod_lat, ng, w_a, qng, kvng, wuq, wukv, (wvt, wukv_vt_all[l], cos, sin), tile=512)
        ona = _na_lat(q, k, vt, nakc_all[:, l], navtc_all[:, l], na_bias[l], batch=dbatch, seq=dseq)
        omla = _mla_lat(qn, qr, kn, kr, mvt, knc_all[l], krc_all[:, l], mvtc_all[l],
                        batch=dbatch, seq=dseq, tq=256, tk=512)
        ofn = _fn_lat(u, batch=dbatch, seq=dseq)
        h_lat = _merge(h_lat, mod_lat, ng, w_e, ona, omla, ofn, wona, womla, wofn, wout, fg, tile=512)

    y_prompt = h_ctx.reshape(batch, seq, D_MODEL)
    y_sample = h_lat.reshape(dbatch, dseq, D_MODEL)
    stack = lambda xs, shape: jnp.stack([a.reshape(batch, seq, -1) for a in xs], axis=1).reshape(shape)
    new_na_k = stack(ks, (batch, DEPTH, seq, NA_HEADS, NA_HEAD_DIM))
    new_na_v = stack(vs, (batch, DEPTH, seq, NA_HEADS, NA_HEAD_DIM))
    new_mla_ckv = stack(ckvs, (batch, DEPTH, seq, KV_LORA))
    new_mla_krope = stack(krs, (batch, DEPTH, seq, MLA_ROPE))
    return (y_prompt, y_sample, new_na_k, new_na_v, new_mla_ckv, new_mla_krope)
```

```python
import functools

import jax
import jax.numpy as jnp
import numpy as np
from jax import lax
from jax.experimental import pallas as pl
from jax.experimental.pallas import tpu as pltpu

F32 = jnp.float32
BF16 = jnp.bfloat16

D_MODEL = 1024
DEPTH = 4
GRID_W = 64
NA_HEADS = 8
NA_HEAD_DIM = 64
NA_WIDTH = 512
NA_ROWS = 8
NA_COLS = 16
MLA_HEADS = 8
MLA_NOPE = 64
MLA_ROPE = 32
MLA_V = 64
MLA_WIDTH = 512
Q_LORA = 256
KV_LORA = 128
MLA_SCALE = (MLA_NOPE + MLA_ROPE) ** -0.5
ROPE_THETA = 10000.0
FN_GROUPS = 4
FN_GROUP_W = 128
FN_WIDTH = 512
EPS = 1e-6
NEG = -1e30

_O_QKV = 0
_O_GATE_NA = 1536
_O_QLAT = 2048
_O_CKV = 2304
_O_KROPE = 2432
_O_GATE_MLA = 2464
_O_UFN = 2976
_O_GATE_FN = 3488
_O_MERGE = 4000
D_IN = 7072

LANES = 128
VMEM_LIMIT = 56 * 1024 * 1024


def _params(sem):
    return pltpu.CompilerParams(dimension_semantics=sem, vmem_limit_bytes=VMEM_LIMIT)


def _sigmoid(x):
    return 1.0 / (1.0 + jnp.exp(-x))


def _silu(x):
    return x * _sigmoid(x)


def _rms(x, g):
    return x * lax.rsqrt(jnp.mean(x * x, axis=-1, keepdims=True) + EPS) * g


def _dot(a, b):
    return jnp.dot(a, b, preferred_element_type=F32)


def _dot_nt(a, b):
    return lax.dot_general(a, b, (((1,), (1,)), ((), ())), preferred_element_type=F32)


def _split(x):
    hi = x.astype(BF16)
    lo = (x - hi.astype(F32)).astype(BF16)
    return hi, lo


def _ada_kernel(cond_ref, w_ref, b_ref, o_ref):
    a = _silu(cond_ref[...]).astype(BF16)
    o_ref[0] = _dot(a, w_ref[0].astype(BF16)) + b_ref[0]


def _ada_mods(cond, w_ada, b_ada):
    nb = 3
    return pl.pallas_call(
        _ada_kernel,
        grid=(DEPTH, nb),
        in_specs=[
            pl.BlockSpec((8, D_MODEL), lambda l, j: (0, 0)),
            pl.BlockSpec((1, D_MODEL, D_MODEL), lambda l, j: (l, 0, j)),
            pl.BlockSpec((1, 1, D_MODEL), lambda l, j: (l, 0, j)),
        ],
        out_specs=pl.BlockSpec((1, 8, D_MODEL), lambda l, j: (l, 0, j)),
        out_shape=jax.ShapeDtypeStruct((DEPTH, 8, 3 * D_MODEL), F32),
        compiler_params=_params(("arbitrary", "arbitrary")),
        name="ada_mods",
    )(cond, w_ada, b_ada.reshape(DEPTH, 1, 3 * D_MODEL))


_A_Q, _A_K, _A_V, _A_QLAT, _A_CKV, _A_KR, _A_KRROT, _A_U, _A_END = (
    0, 512, 1024, 1536, 1792, 1920, 2048, 2176, 2688)


def _modulated(x_ref, mod_ref, ng_ref):
    x = x_ref[...]
    shift = mod_ref[0, 0:1, :]
    scale = mod_ref[0, 1:2, :]
    return x, (_rms(x, ng_ref[...]) * (1.0 + scale) + shift).astype(BF16)


KV_CHUNK = 256
HEAD_DIM = 64
ONES_ROWS = 16
VT_ROWS = HEAD_DIM + ONES_ROWS
VT_ALL = 8 * VT_ROWS


def _store_vt(vt_ref, vt):
    for c in range(vt_ref.shape[0]):
        cols = slice(c * KV_CHUNK, (c + 1) * KV_CHUNK)
        for h in range(8):
            vt_ref[c, h * VT_ROWS:h * VT_ROWS + HEAD_DIM, :] = vt[h * HEAD_DIM:(h + 1) * HEAD_DIM, cols].astype(BF16)
            vt_ref[c, h * VT_ROWS + HEAD_DIM:(h + 1) * VT_ROWS, :] = jnp.ones((ONES_ROWS, KV_CHUNK), BF16)


def _inproj_kernel(*refs, latent):
    (x_ref, mod_ref, ng_ref, w_ref, qng_ref, kvng_ref, wuq_ref, wukv_ref) = refs[:8]
    if latent:
        (wvt_ref, wmvt_ref, cos_ref, sin_ref,
         naq_ref, nak_ref, navt_ref, qn_ref, qr_ref, kn_ref, kr_ref, mvt_ref, u_ref) = refs[8:]
    else:
        (naq_ref, nak_ref, nav_ref, qn_ref, qr_ref, kn_ref, kr_ref, mv_ref, u_ref,
         kf_ref, vf_ref, ckvf_ref, krf_ref) = refs[8:]

    _, xm = _modulated(x_ref, mod_ref, ng_ref)

    def proj(a, b):
        return _dot(xm, w_ref[0, :, a:b])

    q = proj(_A_Q, _A_K)
    naq_ref[...] = (q * (NA_HEAD_DIM ** -0.5)).astype(BF16)
    k = proj(_A_K, _A_V)
    nak_ref[...] = k.astype(BF16)
    if latent:
        _store_vt(navt_ref, _dot_nt(wvt_ref[0], xm))
    else:
        v = proj(_A_V, _A_QLAT)
        nav_ref[...] = v.astype(BF16)
        kf_ref[...] = k
        vf_ref[...] = v

    qlat = _rms(proj(_A_QLAT, _A_CKV), qng_ref[...]).astype(BF16)
    qq = _dot(qlat, wuq_ref[...])
    qn_ref[...] = qq[:, :512].astype(BF16)
    qr = qq[:, 512:768]
    if latent:
        qr = qr * cos_ref[...] + qq[:, 768:1024] * sin_ref[...]
    qr_ref[...] = qr.astype(BF16)

    ckv = _rms(proj(_A_CKV, _A_KR), kvng_ref[...])
    ckv_b = ckv.astype(BF16)
    kn_ref[...] = _dot(ckv_b, wukv_ref[:, :512]).astype(BF16)
    if latent:
        _store_vt(mvt_ref, _dot_nt(wmvt_ref[...], ckv_b))
    else:
        ckvf_ref[...] = ckv
        mv_ref[...] = _dot(ckv_b, wukv_ref[:, 512:]).astype(BF16)

    kr = proj(_A_KR, _A_KRROT)
    if latent:
        kr = kr * cos_ref[:, :LANES] + proj(_A_KRROT, _A_U) * sin_ref[:, :LANES]
    else:
        krf_ref[...] = kr[:, :MLA_ROPE]
    kr_ref[...] = kr.astype(BF16)

    u_ref[...] = proj(_A_U, _A_END)


def _inproj(x, mod, ng, w_a, qng, kvng, wuq, wukv, lat_extra, *, tile, layer):
    n = x.shape[0]
    groups = mod.shape[0]
    steps = n // tile
    per_group = steps // groups
    latent = lat_extra is not None
    tok = lambda w: pl.BlockSpec((tile, w), lambda i: (i, 0))
    full = lambda a: pl.BlockSpec(a.shape, lambda i: (0,) * a.ndim)
    in_specs = [tok(D_MODEL),
                pl.BlockSpec((1, 3, D_MODEL), lambda i: (i // per_group, 0, 0)),
                full(ng), _layer_spec(w_a, layer), full(qng), full(kvng), full(wuq), full(wukv)]
    args = [x, mod, ng, w_a, qng, kvng, wuq, wukv]
    out = lambda w, dt: (jax.ShapeDtypeStruct((n, w), dt), tok(w))
    if latent:
        wvt, wmvt, cos, sin = lat_extra
        steps_per_seq = cos.shape[0] // tile
        in_specs += ([_layer_spec(wvt, layer), full(wmvt)]
                     + [pl.BlockSpec((tile, 256), lambda i: (i % steps_per_seq, 0))] * 2)
        args += [wvt, wmvt, cos, sin]
        cpt = tile // KV_CHUNK
        vt = (jax.ShapeDtypeStruct((n // KV_CHUNK, VT_ALL, KV_CHUNK), BF16),
              pl.BlockSpec((cpt, VT_ALL, KV_CHUNK), lambda i: (i, 0, 0)))
        outs = [out(512, BF16), out(512, BF16), vt, out(512, BF16), out(256, BF16), out(512, BF16),
                out(LANES, BF16), vt, out(512, F32)]
    else:
        outs = [out(512, BF16)] * 3 + [out(512, BF16), out(256, BF16), out(512, BF16), out(LANES, BF16),
                                      out(512, BF16), out(512, F32),
                                      out(512, F32), out(512, F32), out(KV_LORA, F32), out(MLA_ROPE, F32)]
    return pl.pallas_call(
        functools.partial(_inproj_kernel, latent=latent),
        grid=(steps,),
        in_specs=in_specs,
        out_specs=[o[1] for o in outs],
        out_shape=[o[0] for o in outs],
        compiler_params=_params(("arbitrary",)),
        name="inproj_lat" if latent else "inproj_ctx",
    )(*args)


def _lane_iota(n=LANES):
    return lax.broadcasted_iota(jnp.int32, (1, n), 1)


def _head_q(q_pair, half):
    lane = _lane_iota()
    keep = (lane < NA_HEAD_DIM) if half == 0 else (lane >= NA_HEAD_DIM)
    return jnp.where(keep, q_pair, jnp.zeros_like(q_pair))


def _mla_q(qn_ref, qr_ref, h):
    p, half = divmod(h, 2)
    qa = _head_q(qn_ref[:, p * LANES:(p + 1) * LANES], half)
    g, slot = divmod(h, 4)
    qb = qr_ref[:, g * LANES:(g + 1) * LANES]
    lane = _lane_iota()
    keep = (lane >= slot * MLA_ROPE) & (lane < (slot + 1) * MLA_ROPE)
    qb = jnp.where(keep, qb, jnp.zeros_like(qb))
    return jnp.concatenate([qa, qb], axis=1)


def _pair_out(o0, o1):
    return jnp.where(_lane_iota() < NA_HEAD_DIM, o0, o1)


def _softmax_pv(s_list, v_list):
    m = s_list[0].max(axis=-1, keepdims=True)
    for s in s_list[1:]:
        m = jnp.maximum(m, s.max(axis=-1, keepdims=True))
    l = None
    o = None
    for s, v in zip(s_list, v_list):
        e = jnp.exp(s - m)
        ls = e.sum(axis=-1, keepdims=True)
        pv = _dot(e.astype(BF16), v)
        l = ls if l is None else l + ls
        o = pv if o is None else o + pv
    return o / l


def _ctx_attn_kernel(q_ref, k_ref, v_ref, qn_ref, qr_ref, kn_ref, kr_ref, mv_ref, ona_ref, omla_ref):
    for p in range(NA_HEADS // 2):
        sl = slice(p * LANES, (p + 1) * LANES)
        kp = k_ref[:, sl]
        vp = v_ref[:, sl]
        outs = []
        for half in range(2):
            s = _dot_nt(_head_q(q_ref[:, sl], half), kp)
            outs.append(_softmax_pv([s], [vp]))
        ona_ref[:, sl] = _pair_out(*outs)

        kk = jnp.concatenate([kn_ref[:, sl], kr_ref[...]], axis=1)
        mvp = mv_ref[:, sl]
        outs = []
        for half in range(2):
            s = _dot_nt(_mla_q(qn_ref, qr_ref, 2 * p + half), kk) * MLA_SCALE
            outs.append(_softmax_pv([s], [mvp]))
        omla_ref[:, sl] = _pair_out(*outs)


def _ctx_attn(q, k, v, qn, qr, kn, kr, mv, *, seq):
    n = q.shape[0]
    tok = lambda w: pl.BlockSpec((seq, w), lambda b: (b, 0))
    return pl.pallas_call(
        _ctx_attn_kernel,
        grid=(n // seq,),
        in_specs=[tok(512), tok(512), tok(512), tok(512), tok(256), tok(512), tok(LANES), tok(512)],
        out_specs=[tok(512), tok(512)],
        out_shape=[jax.ShapeDtypeStruct((n, 512), F32)] * 2,
        compiler_params=_params(("arbitrary",)),
        name="ctx_attn",
    )(q, k, v, qn, qr, kn, kr, mv)


NA_QROWS = 4
NA_KROWS = 12
_NDR = 2 * NA_ROWS - 1
_NDC = 2 * NA_COLS - 1
_Z_NONE = 48
LOG2E = 1.4426950408889634


def _na_tile_descriptors(rows):
    last_q0 = rows - NA_QROWS
    last_k0 = rows - NA_KROWS
    desc = np.full((3, NA_KROWS, NA_QROWS // 2), _Z_NONE, np.int32)
    for case, (q0, k0) in enumerate(((0, 0), (NA_QROWS, 0), (last_q0, last_k0))):
        for kri in range(NA_KROWS):
            for u in range(NA_QROWS // 2):
                info = []
                for ri in (2 * u, 2 * u + 1):
                    qrow, krow = q0 + ri, k0 + kri
                    start = min(max(qrow - NA_ROWS // 2, 0), rows - NA_ROWS)
                    info.append((start <= krow < start + NA_ROWS, krow - qrow))
                (vl, drl), (vr, _) = info
                d = drl + NA_ROWS - 1
                if vl and vr:
                    desc[case, kri, u] = d
                elif vl:
                    desc[case, kri, u] = 16 + d
                elif vr:
                    desc[case, kri, u] = 32 + d
    return desc.reshape(-1)


def _na_bias_consts():
    col = np.arange(GRID_W)
    col_start = np.clip(col - NA_COLS // 2, 0, GRID_W - NA_COLS)
    kc, qc = col[:, None], col[None, :]
    col_in = (kc >= col_start[None, :]) & (kc < col_start[None, :] + NA_COLS)
    dc = kc - qc + NA_COLS - 1
    d = np.stack([((dc == j) & col_in) for j in range(_NDC)]).astype(np.float32)
    negm = np.where(col_in, 0.0, NEG).astype(np.float32)
    return np.concatenate([d, d], axis=2), np.concatenate([negm, negm], axis=1)


def _na_build_bias(bias_ref, d2_ref, negm_ref, z_ref):
    left = _lane_iota() < GRID_W

    def coef(h, dr, j):
        if dr < -(NA_ROWS - 1) or dr > NA_ROWS - 1:
            return 0.0
        return bias_ref[(h * _NDR + dr + NA_ROWS - 1) * _NDC + j]

    def per_head(h, c):
        for d in range(16):
            dr_l = d - (NA_ROWS - 1)
            acc = negm_ref[...]
            for j in range(_NDC):
                acc = acc + jnp.where(left, coef(h, dr_l, j), coef(h, dr_l - 1, j)) * d2_ref[j]
            z_ref[h, d] = acc
            z_ref[h, 16 + d] = jnp.where(left, acc, NEG)
            z_ref[h, 32 + d] = jnp.where(left, NEG, acc)
        z_ref[h, _Z_NONE] = jnp.full((GRID_W, LANES), NEG, F32)
        return c

    lax.fori_loop(0, NA_HEADS, per_head, 0)


def _finish_pair(acc_a, acc_b):
    norm = lambda acc: acc[:HEAD_DIM] * (1.0 / acc[HEAD_DIM:HEAD_DIM + 1])
    return jnp.concatenate([norm(acc_a), norm(acc_b)], axis=0).T


def _na_lat_kernel(bias_ref, desc_ref, q_ref, k_ref, vt_ref, kc_ref, vtc_ref, d2_ref, negm_ref, o_ref, z_ref,
                   *, rows):
    g = pl.program_id(1)
    groups = rows // NA_QROWS

    @pl.when((pl.program_id(0) == 0) & (g == 0))
    def _():
        _na_build_bias(bias_ref, d2_ref, negm_ref, z_ref)

    tq = NA_QROWS * GRID_W
    k_row0 = jnp.clip(g * NA_QROWS - NA_ROWS // 2, 0, rows - NA_KROWS)
    start = pl.multiple_of(k_row0 * GRID_W, KV_CHUNK)
    chunk0 = k_row0 // (KV_CHUNK // GRID_W)
    n_loc = NA_KROWS * GRID_W
    case = jnp.where(g == 0, 0, jnp.where(g == groups - 1, 2, 1))
    upairs = NA_QROWS // 2

    def scores(h):
        sl = slice((h // 2) * LANES, (h // 2 + 1) * LANES)
        qh = _head_q(q_ref[:, sl], h % 2)
        return _dot_nt(k_ref[pl.ds(start, n_loc), sl], qh), _dot_nt(kc_ref[0, :, sl], qh)

    def attend(h, s_loc, s_ctx):
        rows_h = slice(h * VT_ROWS, (h + 1) * VT_ROWS)
        blocks = []
        for kri in range(NA_KROWS):
            tiles = []
            for u in range(upairs):
                idx = desc_ref[(case * NA_KROWS + kri) * upairs + u]
                tiles.append(s_loc[kri * GRID_W:(kri + 1) * GRID_W, u * LANES:(u + 1) * LANES] + z_ref[h, idx])
            blocks.append(jnp.concatenate(tiles, axis=1))
        s_loc = jnp.concatenate(blocks, axis=0)
        m = jnp.maximum(s_loc.max(axis=0, keepdims=True), s_ctx.max(axis=0, keepdims=True))
        p_loc = jnp.exp2((s_loc - m) * LOG2E).astype(BF16)
        p_ctx = jnp.exp2((s_ctx - m) * LOG2E).astype(BF16)
        acc = _dot(vtc_ref[0, rows_h, :], p_ctx)
        for i in range(n_loc // KV_CHUNK):
            acc = acc + _dot(vt_ref[chunk0 + i, rows_h, :], p_loc[i * KV_CHUNK:(i + 1) * KV_CHUNK])
        return acc

    s_next = scores(0)
    accs = []
    for h in range(NA_HEADS):
        s = s_next
        if h + 1 < NA_HEADS:
            s_next = scores(h + 1)
        accs.append(attend(h, *s))
        if h % 2:
            o_ref[:, (h // 2) * LANES:(h // 2 + 1) * LANES] = _finish_pair(accs[h - 1], accs[h])


def _na_lat(q, k, vt, kc, vtc, bias_l, *, batch, seq):
    rows = seq // GRID_W
    n = q.shape[0]
    groups = rows // NA_QROWS
    tq = NA_QROWS * GRID_W
    chunks = seq // KV_CHUNK
    d2, negm = _na_bias_consts()
    desc = jnp.asarray(_na_tile_descriptors(rows))
    smem = pl.BlockSpec(memory_space=pltpu.SMEM)
    full = lambda a: pl.BlockSpec(a.shape, lambda b, g: (0,) * a.ndim)
    return pl.pallas_call(
        functools.partial(_na_lat_kernel, rows=rows),
        grid=(batch, groups),
        in_specs=[
            smem, smem,
            pl.BlockSpec((tq, 512), lambda b, g: (b * groups + g, 0)),
            pl.BlockSpec((seq, 512), lambda b, g: (b, 0)),
            pl.BlockSpec((chunks, VT_ALL, KV_CHUNK), lambda b, g: (b, 0, 0)),
            pl.BlockSpec((1,) + kc.shape[1:], lambda b, g: (b, 0, 0)),
            pl.BlockSpec((1,) + vtc.shape[1:], lambda b, g: (b, 0, 0)),
            full(d2), full(negm),
        ],
        out_specs=pl.BlockSpec((tq, 512), lambda b, g: (b * groups + g, 0)),
        out_shape=jax.ShapeDtypeStruct((n, 512), F32),
        scratch_shapes=[pltpu.VMEM((NA_HEADS, _Z_NONE + 1, GRID_W, LANES), F32)],
        compiler_params=_params(("arbitrary", "arbitrary")),
        name="na_lat",
    )(bias_l.reshape(-1), desc, q, k, vt, kc, vtc, jnp.asarray(d2), jnp.asarray(negm))


_MLA_C = MLA_SCALE * LOG2E


def _mla_lat_kernel(qn_ref, qr_ref, kn_ref, kr_ref, vt_ref, knc_ref, krc_ref, vtc_ref, o_ref,
                    q2_ref, m_ref, acc_ref, sa_ref, sb_ref, *, tk):
    tq = qn_ref.shape[0]
    n_tiles = kn_ref.shape[0] // tk
    cpt = tk // KV_CHUNK
    pairs = MLA_HEADS // 2
    for p in range(pairs):
        q2_ref[p] = jnp.concatenate([_mla_q(qn_ref, qr_ref, 2 * p), _mla_q(qn_ref, qr_ref, 2 * p + 1)], axis=0)
        m_ref[p] = jnp.full((1, 2 * tq), NEG, F32)
        acc_ref[p] = jnp.zeros((2 * VT_ROWS, 2 * tq), F32)

    def scores(p, kk):
        return _dot_nt(kk, q2_ref[p])

    def update(p, s, vts):
        m_prev = m_ref[p]
        m_new = jnp.maximum(m_prev, s.max(axis=0, keepdims=True))
        alpha = jnp.exp2((m_prev - m_new) * _MLA_C)
        pt = jnp.exp2((s - m_new) * _MLA_C).astype(BF16)
        acc = acc_ref[p] * alpha
        for i, vt in enumerate(vts):
            acc = acc + _dot(vt, pt[i * KV_CHUNK:(i + 1) * KV_CHUNK])
        return m_new, acc

    def store(new):
        for p, (m_new, acc) in enumerate(new):
            acc_ref[p] = acc
            m_ref[p] = m_new

    def lanes(p):
        return slice(p * LANES, (p + 1) * LANES)

    def rows(p):
        return slice(2 * p * VT_ROWS, 2 * (p + 1) * VT_ROWS)

    def keys(t, p):
        ks = pl.ds(pl.multiple_of(t * tk, tk), tk)
        return jnp.concatenate([kn_ref[ks, lanes(p)], kr_ref[ks, :]], axis=1)

    for p in range(pairs):
        sa_ref[p] = scores(p, keys(0, p))

    def tile_step(t, cur_ref, nxt_ref):
        t_next = jnp.minimum(t + 1, n_tiles - 1)
        ahead = 2
        for p in range(ahead):
            nxt_ref[p] = scores(p, keys(t_next, p))
        new = []
        for p in range(pairs):
            new.append(update(p, cur_ref[p], [vt_ref[t * cpt + i, rows(p), :] for i in range(cpt)]))
            if p + ahead < pairs:
                nxt_ref[p + ahead] = scores(p + ahead, keys(t_next, p + ahead))
        store(new)

    def body(i, c):
        tile_step(2 * i, sa_ref, sb_ref)
        tile_step(2 * i + 1, sb_ref, sa_ref)
        return c

    lax.fori_loop(0, n_tiles // 2, body, 0)

    ctx_keys = lambda p: jnp.concatenate([knc_ref[0, :, lanes(p)], krc_ref[0]], axis=1)
    new = []
    s_next = scores(0, ctx_keys(0))
    for p in range(pairs):
        s = s_next
        if p + 1 < pairs:
            s_next = scores(p + 1, ctx_keys(p + 1))
        new.append(update(p, s, [vtc_ref[0, rows(p), :]]))
    for p, (_, acc) in enumerate(new):
        o_ref[:, lanes(p)] = _finish_pair(acc[:VT_ROWS, :tq], acc[VT_ROWS:, tq:])


def _mla_lat(qn, qr, kn, kr, vt, knc, krc, vtc, *, batch, seq, tq, tk):
    n = qn.shape[0]
    steps = seq // tq
    chunks = seq // KV_CHUNK
    assert knc.shape[1] == KV_CHUNK
    qtok = lambda w: pl.BlockSpec((tq, w), lambda b, i: (b * steps + i, 0))
    ktok = lambda w: pl.BlockSpec((seq, w), lambda b, i: (b, 0))
    ctok = lambda a: pl.BlockSpec((1,) + a.shape[1:], lambda b, i: (b, 0, 0))
    pairs = MLA_HEADS // 2
    return pl.pallas_call(
        functools.partial(_mla_lat_kernel, tk=tk),
        grid=(batch, steps),
        in_specs=[qtok(512), qtok(256), ktok(512), ktok(LANES),
                  pl.BlockSpec((chunks, VT_ALL, KV_CHUNK), lambda b, i: (b, 0, 0)),
                  ctok(knc), ctok(krc), ctok(vtc)],
        out_specs=qtok(512),
        out_shape=jax.ShapeDtypeStruct((n, 512), F32),
        scratch_shapes=[pltpu.VMEM((pairs, 2 * tq, 2 * LANES), BF16),
                        pltpu.VMEM((pairs, 1, 2 * tq), F32),
                        pltpu.VMEM((pairs, 2 * VT_ROWS, 2 * tq), F32),
                        pltpu.VMEM((pairs, tk, 2 * tq), F32),
                        pltpu.VMEM((pairs, tk, 2 * tq), F32)],
        compiler_params=_params(("arbitrary", "arbitrary")),
        name="mla_lat",
    )(qn, qr, kn, kr, vt, knc, krc, vtc)


def _cache_kv_kernel(ckv_ref, w_ref, wvt_ref, kn_ref, vt_ref):
    ckv = ckv_ref[0, 0].astype(BF16)
    kn_ref[0, 0] = _dot(ckv, w_ref[0]).astype(BF16)
    _store_vt(vt_ref.at[0], _dot_nt(wvt_ref[0], ckv))


def _cache_kv(ckv, wukv_k, wukv_vt):
    _, nb, rows, _ = ckv.shape
    assert rows == KV_CHUNK
    spec = lambda a, b: pl.BlockSpec((1, a, b), lambda l, j: (l, 0, 0))
    return pl.pallas_call(
        _cache_kv_kernel,
        grid=(DEPTH, nb),
        in_specs=[pl.BlockSpec((1, 1, rows, KV_LORA), lambda l, j: (l, j, 0, 0)),
                  spec(KV_LORA, 512), spec(512, KV_LORA)],
        out_specs=[pl.BlockSpec((1, 1, rows, 512), lambda l, j: (l, j, 0, 0)),
                   pl.BlockSpec((1, 1, VT_ALL, KV_CHUNK), lambda l, j: (l, j, 0, 0))],
        out_shape=[jax.ShapeDtypeStruct((DEPTH, nb, rows, 512), BF16),
                   jax.ShapeDtypeStruct((DEPTH, nb, VT_ALL, KV_CHUNK), BF16)],
        compiler_params=_params(("arbitrary", "arbitrary")),
        name="cache_kv",
    )(ckv, wukv_k, wukv_vt)


def _dft_mats(n, scale):
    k = np.arange(n)
    ang = 2.0 * np.pi * ((k[:, None] * k[None, :]) % n) / n
    return np.cos(ang) * scale, np.sin(ang) * scale


def _hi_lo_np(m):
    m = jnp.asarray(m, F32)
    hi = m.astype(BF16)
    lo = (m - hi.astype(F32)).astype(BF16)
    return hi, lo


def _chan_dft(u, wc_ref):
    uh, ul = _split(u)
    wh = wc_ref[0]
    wl = wc_ref[1]
    return _dot(uh, wh) + _dot(ul, wh) + _dot(uh, wl)


def _fn_ctx_kernel(u_ref, wc_ref, fp_ref, o_ref):
    for g in range(FN_GROUPS):
        sl = slice(g * FN_GROUP_W, (g + 1) * FN_GROUP_W)
        a = _chan_dft(u_ref[:, sl], wc_ref)
        ar_h, ar_l = _split(a[:, :FN_GROUP_W])
        ai_h, ai_l = _split(a[:, FN_GROUP_W:])
        rh = jnp.concatenate([ar_h, ai_h], axis=0)
        rl = jnp.concatenate([ar_l, ai_l], axis=0)
        o_ref[:, sl] = _dot(fp_ref[0], rh) + _dot(fp_ref[0], rl) + _dot(fp_ref[1], rh)


def _fn_ctx(u, *, seq):
    n = u.shape[0]
    cc, sc = _dft_mats(FN_GROUP_W, FN_GROUP_W ** -0.5)
    wc = jnp.stack(_hi_lo_np(np.concatenate([cc, -sc], axis=1)))
    cp, sp = _dft_mats(seq, seq ** -0.5)
    fp = jnp.stack(_hi_lo_np(np.concatenate([cp, sp], axis=1)))
    full = lambda a: pl.BlockSpec(a.shape, lambda b: (0,) * a.ndim)
    return pl.pallas_call(
        _fn_ctx_kernel,
        grid=(n // seq,),
        in_specs=[pl.BlockSpec((seq, FN_WIDTH), lambda b: (b, 0)), full(wc), full(fp)],
        out_specs=pl.BlockSpec((seq, FN_WIDTH), lambda b: (b, 0)),
        out_shape=jax.ShapeDtypeStruct((n, FN_WIDTH), F32),
        compiler_params=_params(("arbitrary",)),
        name="fn_ctx",
    )(u, wc, fp)


_FN_G = 8


def _fn_lat_kernel(u_ref, wc_ref, f1_ref, f2_ref, twc_ref, tws_ref, o_ref, a_ref, t_ref):
    n = GRID_W
    w = FN_GROUP_W

    def chan(i, c):
        rows = pl.ds(pl.multiple_of(i * 512, 512), 512)
        a = _chan_dft(u_ref[rows, :], wc_ref)
        a_ref[0, rows, :] = a[:, :FN_GROUP_W]
        a_ref[1, rows, :] = a[:, FN_GROUP_W:]
        return c

    lax.fori_loop(0, u_ref.shape[0] // 512, chan, 0)

    def gather(ref, base):
        cols = [jnp.concatenate([ref[0, pl.ds(base + j, n, stride=n), :],
                                 ref[1, pl.ds(base + j, n, stride=n), :]], axis=0) for j in range(_FN_G)]
        return jnp.concatenate(cols, axis=1)

    def dft(f_ref, d):
        rows = f_ref.shape[0] // 2
        dh, dl = _split(d)
        rh = _dot(f_ref[...], dh)
        return rh[:rows] + rh[rows:] + _dot(f_ref[:rows, :], dl)

    def stage1(i, c):
        base = i * _FN_G
        b = dft(f1_ref, gather(a_ref, base))
        for j in range(_FN_G):
            br = b[:n, j * w:(j + 1) * w]
            bi = b[n:, j * w:(j + 1) * w]
            rows = pl.ds(pl.multiple_of((base + j) * n, n), n)
            tc = twc_ref[rows, :]
            ts = tws_ref[rows, :]
            t_ref[0, rows, :] = br * tc + bi * ts
            t_ref[1, rows, :] = bi * tc - br * ts
        return c

    lax.fori_loop(0, n // _FN_G, stage1, 0)

    def stage2(i, c):
        base = i * _FN_G
        y = dft(f2_ref, gather(t_ref, base))
        for j in range(_FN_G):
            o_ref[pl.ds(base + j, n, stride=n), :] = y[:, j * w:(j + 1) * w]
        return c

    lax.fori_loop(0, n // _FN_G, stage2, 0)


def _fn_lat(u, *, batch, seq):
    n = GRID_W
    assert seq == n * n
    cc, sc = _dft_mats(FN_GROUP_W, FN_GROUP_W ** -0.5)
    wc = jnp.stack(_hi_lo_np(np.concatenate([cc, -sc], axis=1)))
    c1, s1 = _dft_mats(n, 1.0)
    c2, s2 = _dft_mats(n, 1.0 / n)
    f1 = jnp.concatenate(_hi_lo_np(np.block([[c1, s1], [-s1, c1]])), axis=0)
    f2 = jnp.concatenate(_hi_lo_np(np.concatenate([c2, s2], axis=1)), axis=0)
    n2 = np.arange(n)[:, None]
    k1 = np.arange(n)[None, :]
    ang = (2.0 * np.pi * ((n2 * k1) % seq) / seq).reshape(seq, 1)
    twc = jnp.asarray(np.broadcast_to(np.cos(ang), (seq, FN_GROUP_W)), F32)
    tws = jnp.asarray(np.broadcast_to(np.sin(ang), (seq, FN_GROUP_W)), F32)
    full = lambda a: pl.BlockSpec(a.shape, lambda b, g: (0,) * a.ndim)
    blk = pl.BlockSpec((seq, FN_GROUP_W), lambda b, g: (b, g))
    return pl.pallas_call(
        _fn_lat_kernel,
        grid=(batch, FN_GROUPS),
        in_specs=[blk, full(wc), full(f1), full(f2), full(twc), full(tws)],
        out_specs=blk,
        out_shape=jax.ShapeDtypeStruct(u.shape, F32),
        scratch_shapes=[pltpu.VMEM((2, seq, FN_GROUP_W), F32), pltpu.VMEM((2, seq, FN_GROUP_W), F32)],
        compiler_params=_params(("arbitrary", "arbitrary")),
        name="fn_lat",
    )(u, wc, f1, f2, twc, tws)


_E_GNA, _E_GMLA, _E_GFN, _E_MERGE = 0, 512, 1024, 1536


def _merge_kernel(*refs, final):
    (x_ref, mod_ref, ng_ref, wg_ref, ona_ref, omla_ref, ofn_ref, wona_ref, womla_ref, wofn_ref,
     wout_ref) = refs[:11]
    if final:
        fg_ref, o_ref = refs[11:13]
    else:
        o_ref = refs[11]
    x, xm = _modulated(x_ref, mod_ref, ng_ref)
    gate = mod_ref[0, 2:3, :]

    merged = None
    for i, (br_ref, wo_ref) in enumerate(((ona_ref, wona_ref), (omla_ref, womla_ref), (ofn_ref, wofn_ref))):
        g = _dot(xm, wg_ref[0, :, i * 512:(i + 1) * 512])
        t = _dot((br_ref[...] * _silu(g)).astype(BF16), wo_ref[0])
        ml = _dot(xm, wg_ref[0, :, _E_MERGE + i * D_MODEL:_E_MERGE + (i + 1) * D_MODEL])
        t = _sigmoid(ml) * t
        merged = t if merged is None else merged + t
    h = x + gate * _dot(merged.astype(BF16), wout_ref[0])
    if final:
        h = _rms(h, fg_ref[...])
    o_ref[...] = h


def _layer_spec(a, layer):
    return pl.BlockSpec((1,) + a.shape[1:], lambda *_: (layer,) + (0,) * (a.ndim - 1))


def _merge(x, mod, ng, w_g, ona, omla, ofn, wona, womla, wofn, wout, fg, *, tile, layer):
    n = x.shape[0]
    groups = mod.shape[0]
    steps = n // tile
    per_group = steps // groups
    final = fg is not None
    tok = lambda w: pl.BlockSpec((tile, w), lambda i: (i, 0))
    full = lambda a: pl.BlockSpec(a.shape, lambda i: (0,) * a.ndim)
    lay = lambda a: _layer_spec(a, layer)
    in_specs = [tok(D_MODEL), pl.BlockSpec((1, 3, D_MODEL), lambda i: (i // per_group, 0, 0)),
                full(ng), lay(w_g), tok(512), tok(512), tok(512),
                lay(wona), lay(womla), lay(wofn), lay(wout)]
    args = [x, mod, ng, w_g, ona, omla, ofn, wona, womla, wofn, wout]
    if final:
        in_specs.append(full(fg))
        args.append(fg)
    return pl.pallas_call(
        functools.partial(_merge_kernel, final=final),
        grid=(steps,),
        in_specs=in_specs,
        out_specs=tok(D_MODEL),
        out_shape=jax.ShapeDtypeStruct((n, D_MODEL), F32),
        compiler_params=_params(("arbitrary",)),
        name="merge",
    )(*args)


def _rope_rot_cols():
    q = MLA_ROPE // 4
    perm = np.concatenate([np.arange(q, 2 * q), np.arange(0, q), np.arange(3 * q, 4 * q), np.arange(2 * q, 3 * q)])
    sign = np.concatenate([-np.ones(q), np.ones(q), -np.ones(q), np.ones(q)]).astype(np.float32)
    return perm, sign


_PREP_ROWS = 256


def _lane_window(ref, start, width):
    a0 = start // LANES * LANES
    a1 = min(-(-(start + width) // LANES) * LANES, ref.shape[-1])
    return ref[0, :, a0:a1][:, start - a0:start - a0 + width]


def _prep_w_in_kernel(w_ref, wkr_ref, wa_ref, we_ref, wvt_ref):
    cast = lambda start, width: _lane_window(w_ref, start, width).astype(BF16)
    wa_ref[0, :, _A_Q:_A_QLAT] = cast(_O_QKV, 1536)
    wa_ref[0, :, _A_QLAT:_A_KR] = cast(_O_QLAT, Q_LORA + KV_LORA)
    wa_ref[0, :, _A_KR:_A_U] = wkr_ref[0]
    wa_ref[0, :, _A_U:_A_END] = cast(_O_UFN, FN_WIDTH)
    we_ref[0, :, _E_GNA:_E_GMLA] = cast(_O_GATE_NA, 512)
    we_ref[0, :, _E_GMLA:_E_GFN] = cast(_O_GATE_MLA, 512)
    we_ref[0, :, _E_GFN:_E_MERGE] = cast(_O_GATE_FN, 512)
    we_ref[0, :, _E_MERGE:] = cast(_O_MERGE, 3 * D_MODEL)
    wvt_ref[0] = _lane_window(w_ref, _O_QKV + 2 * NA_WIDTH, NA_WIDTH).T.astype(BF16)


def _prep_w_in(w_in):
    perm, sign = _rope_rot_cols()
    kr = w_in[:, :, _O_KROPE:_O_KROPE + MLA_ROPE]
    wkr = jnp.concatenate([jnp.tile(kr, (1, 1, 4)), jnp.tile(kr[:, :, perm] * sign, (1, 1, 4))],
                          axis=2).astype(BF16)
    rows = pl.BlockSpec((1, _PREP_ROWS, D_IN), lambda l, r: (l, r, 0))
    out = lambda w: pl.BlockSpec((1, _PREP_ROWS, w), lambda l, r: (l, r, 0))
    return pl.pallas_call(
        _prep_w_in_kernel,
        grid=(DEPTH, D_MODEL // _PREP_ROWS),
        in_specs=[rows, out(2 * LANES)],
        out_specs=[out(_A_END), out(_E_MERGE + 3 * D_MODEL),
                   pl.BlockSpec((1, NA_WIDTH, _PREP_ROWS), lambda l, r: (l, 0, r))],
        out_shape=[jax.ShapeDtypeStruct((DEPTH, D_MODEL, _A_END), BF16),
                   jax.ShapeDtypeStruct((DEPTH, D_MODEL, _E_MERGE + 3 * D_MODEL), BF16),
                   jax.ShapeDtypeStruct((DEPTH, NA_WIDTH, D_MODEL), BF16)],
        compiler_params=_params(("arbitrary", "arbitrary")),
        name="prep_w_in",
    )(w_in, wkr)


def _prep_w_uq(w):
    perm, sign = _rope_rot_cols()
    w3 = w.reshape(Q_LORA, MLA_HEADS, MLA_NOPE + MLA_ROPE)
    nope = w3[:, :, :MLA_NOPE].reshape(Q_LORA, MLA_HEADS * MLA_NOPE)
    rope = w3[:, :, MLA_NOPE:]
    rot = rope[:, :, perm] * sign
    return jnp.concatenate([nope, rope.reshape(Q_LORA, -1), rot.reshape(Q_LORA, -1)], axis=1).astype(BF16)


def _prep_w_ukv(w):
    w3 = w.reshape(KV_LORA, MLA_HEADS, MLA_NOPE + MLA_V)
    return jnp.concatenate([w3[:, :, :MLA_NOPE].reshape(KV_LORA, -1),
                            w3[:, :, MLA_NOPE:].reshape(KV_LORA, -1)], axis=1).astype(BF16)


def _rope_tables(n):
    t = jnp.arange(n, dtype=jnp.int32)
    row = (t // GRID_W).astype(F32)
    col = (t % GRID_W).astype(F32)
    half = MLA_ROPE // 2
    inv_freq = ROPE_THETA ** (-jnp.arange(0, half, 2, dtype=F32) / half)
    ar = row[:, None] * inv_freq[None, :]
    ac = col[:, None] * inv_freq[None, :]
    ang = jnp.concatenate([ar, ar, ac, ac], axis=-1)
    return jnp.tile(jnp.cos(ang), (1, MLA_HEADS)), jnp.tile(jnp.sin(ang), (1, MLA_HEADS))


def kernel(x_prompt, x_sample, cache_na_k, cache_na_v, cache_mla_ckv, cache_mla_krope, c, c_ctx,
           w_ada, b_ada, norm_g, w_in, q_norm_g, kv_norm_g, w_uq, w_ukv, na_bias,
           w_o_na, w_o_mla, w_o_fourier, w_out, final_norm_g):
    batch, seq, _ = x_prompt.shape
    dbatch, dseq, _ = x_sample.shape
    past = cache_na_k.shape[2]

    cond = jnp.zeros((8, D_MODEL), F32).at[0].set(c_ctx).at[1:1 + dbatch].set(c)
    mods = _ada_mods(cond, w_ada, b_ada).reshape(DEPTH, 8, 3, D_MODEL)

    wukv_all = jnp.stack([_prep_w_ukv(w_ukv[l]) for l in range(DEPTH)])
    wukv_vt_all = wukv_all[:, :, 512:].transpose(0, 2, 1)
    knc_all, mvtc_all = _cache_kv(cache_mla_ckv.transpose(1, 0, 2, 3), wukv_all[:, :, :512], wukv_vt_all)
    krc_all = jnp.tile(cache_mla_krope, (1, 1, 1, 4)).astype(BF16)
    nakc_all = cache_na_k.reshape(dbatch, DEPTH, past, NA_WIDTH).astype(BF16)
    navt = cache_na_v.transpose(0, 1, 3, 4, 2).astype(BF16)
    navtc_all = jnp.concatenate([navt, jnp.ones((dbatch, DEPTH, NA_HEADS, ONES_ROWS, past), BF16)],
                                axis=3).reshape(dbatch, DEPTH, VT_ALL, past)
    cos, sin = _rope_tables(dseq)

    h_ctx = x_prompt.reshape(batch * seq, D_MODEL)
    h_lat = x_sample.reshape(dbatch * dseq, D_MODEL)
    ks, vs, ckvs, krs = [], [], [], []
    row = lambda a: a.reshape(1, -1)
    w_a, w_e, wvt = _prep_w_in(w_in)
    wona, womla, wofn, wout = (w_o_na.astype(BF16), w_o_mla.astype(BF16),
                               w_o_fourier.astype(BF16), w_out.astype(BF16))
    for l in range(DEPTH):
        wuq = _prep_w_uq(w_uq[l])
        wukv = wukv_all[l]
        ng, qng, kvng = row(norm_g[l]), row(q_norm_g[l]), row(kv_norm_g[l])
        fg = row(final_norm_g) if l == DEPTH - 1 else None
        mod_ctx = mods[l, 0:1]
        mod_lat = mods[l, 1:1 + dbatch]

        (q, k, v, qn, qr, kn, kr, mv, u, kf, vf, ckvf, krf) = _inproj(
            h_ctx, mod_ctx, ng, w_a, qng, kvng, wuq, wukv, None, tile=512, layer=l)
        ks.append(kf)
        vs.append(vf)
        ckvs.append(ckvf)
        krs.append(krf)
        ona, omla = _ctx_attn(q, k, v, qn, qr, kn, kr, mv, seq=seq)
        ofn = _fn_ctx(u, seq=seq)
        h_ctx = _merge(h_ctx, mod_ctx, ng, w_e, ona, omla, ofn, wona, womla, wofn, wout, fg, tile=512, layer=l)

        (q, k, vt, qn, qr, kn, kr, mvt, u) = _inproj(
            h_lat, mod_lat, ng, w_a, qng, kvng, wuq, wukv, (wvt, wukv_vt_all[l], cos, sin), tile=512, layer=l)
        ona = _na_lat(q, k, vt, nakc_all[:, l], navtc_all[:, l], na_bias[l], batch=dbatch, seq=dseq)
        omla = _mla_lat(qn, qr, kn, kr, mvt, knc_all[l], krc_all[:, l], mvtc_all[l],
                        batch=dbatch, seq=dseq, tq=256, tk=512)
        ofn = _fn_lat(u, batch=dbatch, seq=dseq)
        h_lat = _merge(h_lat, mod_lat, ng, w_e, ona, omla, ofn, wona, womla, wofn, wout, fg, tile=512, layer=l)

    y_prompt = h_ctx.reshape(batch, seq, D_MODEL)
    y_sample = h_lat.reshape(dbatch, dseq, D_MODEL)
    stack = lambda xs, shape: jnp.stack([a.reshape(batch, seq, -1) for a in xs], axis=1).reshape(shape)
    new_na_k = stack(ks, (batch, DEPTH, seq, NA_HEADS, NA_HEAD_DIM))
    new_na_v = stack(vs, (batch, DEPTH, seq, NA_HEADS, NA_HEAD_DIM))
    new_mla_ckv = stack(ckvs, (batch, DEPTH, seq, KV_LORA))
    new_mla_krope = stack(krs, (batch, DEPTH, seq, MLA_ROPE))
    return (y_prompt, y_sample, new_na_k, new_na_v, new_mla_ckv, new_mla_krope)
```

```python
import functools

import jax
import jax.numpy as jnp
import numpy as np
from jax import lax
from jax.experimental import pallas as pl
from jax.experimental.pallas import tpu as pltpu

F32 = jnp.float32
BF16 = jnp.bfloat16

D_MODEL = 1024
DEPTH = 4
GRID_W = 64
NA_HEADS = 8
NA_HEAD_DIM = 64
NA_WIDTH = 512
NA_ROWS = 8
NA_COLS = 16
MLA_HEADS = 8
MLA_NOPE = 64
MLA_ROPE = 32
MLA_V = 64
MLA_WIDTH = 512
Q_LORA = 256
KV_LORA = 128
MLA_SCALE = (MLA_NOPE + MLA_ROPE) ** -0.5
ROPE_THETA = 10000.0
FN_GROUPS = 4
FN_GROUP_W = 128
FN_WIDTH = 512
EPS = 1e-6
NEG = -1e30

_O_QKV = 0
_O_GATE_NA = 1536
_O_QLAT = 2048
_O_CKV = 2304
_O_KROPE = 2432
_O_GATE_MLA = 2464
_O_UFN = 2976
_O_GATE_FN = 3488
_O_MERGE = 4000
D_IN = 7072

LANES = 128
VMEM_LIMIT = 56 * 1024 * 1024


def _params(sem):
    return pltpu.CompilerParams(dimension_semantics=sem, vmem_limit_bytes=VMEM_LIMIT)


def _sigmoid(x):
    return 1.0 / (1.0 + jnp.exp(-x))


def _silu(x):
    return x * _sigmoid(x)


def _rms(x, g):
    return x * lax.rsqrt(jnp.mean(x * x, axis=-1, keepdims=True) + EPS) * g


def _dot(a, b):
    return jnp.dot(a, b, preferred_element_type=F32)


def _dot_nt(a, b):
    return lax.dot_general(a, b, (((1,), (1,)), ((), ())), preferred_element_type=F32)


def _split(x):
    hi = x.astype(BF16)
    lo = (x - hi.astype(F32)).astype(BF16)
    return hi, lo


def _ada_kernel(cond_ref, w_ref, b_ref, o_ref):
    a = _silu(cond_ref[...]).astype(BF16)
    o_ref[0] = _dot(a, w_ref[0].astype(BF16)) + b_ref[0]


def _ada_mods(cond, w_ada, b_ada):
    nb = 3
    return pl.pallas_call(
        _ada_kernel,
        grid=(DEPTH, nb),
        in_specs=[
            pl.BlockSpec((8, D_MODEL), lambda l, j: (0, 0)),
            pl.BlockSpec((1, D_MODEL, D_MODEL), lambda l, j: (l, 0, j)),
            pl.BlockSpec((1, 1, D_MODEL), lambda l, j: (l, 0, j)),
        ],
        out_specs=pl.BlockSpec((1, 8, D_MODEL), lambda l, j: (l, 0, j)),
        out_shape=jax.ShapeDtypeStruct((DEPTH, 8, 3 * D_MODEL), F32),
        compiler_params=_params(("arbitrary", "arbitrary")),
        name="ada_mods",
    )(cond, w_ada, b_ada.reshape(DEPTH, 1, 3 * D_MODEL))


_A_Q, _A_K, _A_V, _A_QLAT, _A_CKV, _A_KR, _A_KRROT, _A_U, _A_END = (
    0, 512, 1024, 1536, 1792, 1920, 2048, 2176, 2688)


def _modulated(x_ref, mod_ref, ng_ref):
    x = x_ref[...]
    shift = mod_ref[0, 0:1, :]
    scale = mod_ref[0, 1:2, :]
    return x, (_rms(x, ng_ref[...]) * (1.0 + scale) + shift).astype(BF16)


KV_CHUNK = 256
HEAD_DIM = 64
ONES_ROWS = 16
VT_ROWS = HEAD_DIM + ONES_ROWS
VT_ALL = 8 * VT_ROWS


def _store_vt(vt_ref, vt):
    for c in range(vt_ref.shape[0]):
        cols = slice(c * KV_CHUNK, (c + 1) * KV_CHUNK)
        for h in range(8):
            vt_ref[c, h * VT_ROWS:h * VT_ROWS + HEAD_DIM, :] = vt[h * HEAD_DIM:(h + 1) * HEAD_DIM, cols].astype(BF16)
            vt_ref[c, h * VT_ROWS + HEAD_DIM:(h + 1) * VT_ROWS, :] = jnp.ones((ONES_ROWS, KV_CHUNK), BF16)


def _store_seq_t(ref, xt):
    seq = ref.shape[2]
    for j in range(ref.shape[0]):
        ref[j] = xt[:, j * seq:(j + 1) * seq]


def _inproj_kernel(*refs, latent):
    (x_ref, mod_ref, ng_ref, w_ref, qng_ref, kvng_ref, wuq_ref, wukv_ref) = refs[:8]
    if latent:
        (wmvt_ref, cos_ref, sin_ref,
         naq_ref, nak_ref, navt_ref, qn_ref, qr_ref, kn_ref, kr_ref, mvt_ref, u_ref) = refs[8:]
    else:
        (naq_ref, nak_ref, nav_ref, qn_ref, qr_ref, kn_ref, kr_ref, mv_ref, u_ref,
         kf_ref, vf_ref, ckvf_ref, krf_ref) = refs[8:]

    _, xm = _modulated(x_ref, mod_ref, ng_ref)

    def proj(a, b):
        return _dot_nt(xm, w_ref[0, a:b, :])

    q = proj(_A_Q, _A_K)
    naq_ref[...] = (q * (NA_HEAD_DIM ** -0.5)).astype(BF16)
    k = proj(_A_K, _A_V)
    nak_ref[...] = k.astype(BF16)
    if latent:
        _store_vt(navt_ref, _dot_nt(w_ref[0, _A_V:_A_QLAT, :], xm))
    else:
        v = proj(_A_V, _A_QLAT)
        nav_ref[...] = v.astype(BF16)
        _store_seq_t(kf_ref, k.T)
        _store_seq_t(vf_ref, v.T)

    qlat = _rms(proj(_A_QLAT, _A_CKV), qng_ref[...]).astype(BF16)
    qq = _dot(qlat, wuq_ref[...])
    qn_ref[...] = qq[:, :512].astype(BF16)
    qr = qq[:, 512:768]
    if latent:
        qr = qr * cos_ref[...] + qq[:, 768:1024] * sin_ref[...]
    qr_ref[...] = qr.astype(BF16)

    ckv = _rms(proj(_A_CKV, _A_KR), kvng_ref[...])
    ckv_b = ckv.astype(BF16)
    kn_ref[...] = _dot(ckv_b, wukv_ref[:, :512]).astype(BF16)
    if latent:
        _store_vt(mvt_ref, _dot_nt(wmvt_ref[...], ckv_b))
    else:
        ckvf_ref[...] = ckv
        mv_ref[...] = _dot(ckv_b, wukv_ref[:, 512:]).astype(BF16)

    kr = proj(_A_KR, _A_KRROT)
    if latent:
        kr = kr * cos_ref[:, :LANES] + proj(_A_KRROT, _A_U) * sin_ref[:, :LANES]
    else:
        _store_seq_t(krf_ref, kr.T[:MLA_ROPE])
    kr_ref[...] = kr.astype(BF16)

    u_ref[...] = proj(_A_U, _A_END)


def _inproj(x, mod, ng, w_a, qng, kvng, wuq, wukv, lat_extra, *, tile, layer, seq=None):
    n = x.shape[0]
    groups = mod.shape[0]
    steps = n // tile
    per_group = steps // groups
    latent = lat_extra is not None
    tok = lambda w: pl.BlockSpec((tile, w), lambda i: (i, 0))
    full = lambda a: pl.BlockSpec(a.shape, lambda i: (0,) * a.ndim)
    in_specs = [tok(D_MODEL),
                pl.BlockSpec((1, 3, D_MODEL), lambda i: (i // per_group, 0, 0)),
                full(ng), _layer_spec(w_a, layer), full(qng), full(kvng), full(wuq), full(wukv)]
    args = [x, mod, ng, w_a, qng, kvng, wuq, wukv]
    out = lambda w, dt: (jax.ShapeDtypeStruct((n, w), dt), tok(w))
    if latent:
        wmvt, cos, sin = lat_extra
        steps_per_seq = cos.shape[0] // tile
        in_specs += [full(wmvt)] + [pl.BlockSpec((tile, 256), lambda i: (i % steps_per_seq, 0))] * 2
        args += [wmvt, cos, sin]
        cpt = tile // KV_CHUNK
        vt = (jax.ShapeDtypeStruct((n // KV_CHUNK, VT_ALL, KV_CHUNK), BF16),
              pl.BlockSpec((cpt, VT_ALL, KV_CHUNK), lambda i: (i, 0, 0)))
        outs = [out(512, BF16), out(512, BF16), vt, out(512, BF16), out(256, BF16), out(512, BF16),
                out(LANES, BF16), vt, out(512, F32)]
    else:
        out_t = lambda w: (jax.ShapeDtypeStruct((n // seq, w, seq), F32),
                           pl.BlockSpec((tile // seq, w, seq), lambda i: (i, 0, 0)))
        outs = [out(512, BF16)] * 3 + [out(512, BF16), out(256, BF16), out(512, BF16), out(LANES, BF16),
                                      out(512, BF16), out(512, F32),
                                      out_t(512), out_t(512), out(KV_LORA, F32), out_t(MLA_ROPE)]
    return pl.pallas_call(
        functools.partial(_inproj_kernel, latent=latent),
        grid=(steps,),
        in_specs=in_specs,
        out_specs=[o[1] for o in outs],
        out_shape=[o[0] for o in outs],
        compiler_params=_params(("arbitrary",)),
        name="inproj_lat" if latent else "inproj_ctx",
    )(*args)


def _lane_iota(n=LANES):
    return lax.broadcasted_iota(jnp.int32, (1, n), 1)


def _head_q(q_pair, half):
    lane = _lane_iota()
    keep = (lane < NA_HEAD_DIM) if half == 0 else (lane >= NA_HEAD_DIM)
    return jnp.where(keep, q_pair, jnp.zeros_like(q_pair))


def _mla_q(qn_ref, qr_ref, h):
    p, half = divmod(h, 2)
    qa = _head_q(qn_ref[:, p * LANES:(p + 1) * LANES], half)
    g, slot = divmod(h, 4)
    qb = qr_ref[:, g * LANES:(g + 1) * LANES]
    lane = _lane_iota()
    keep = (lane >= slot * MLA_ROPE) & (lane < (slot + 1) * MLA_ROPE)
    qb = jnp.where(keep, qb, jnp.zeros_like(qb))
    return jnp.concatenate([qa, qb], axis=1)


def _pair_out(o0, o1):
    return jnp.where(_lane_iota() < NA_HEAD_DIM, o0, o1)


def _softmax_pv(s_list, v_list):
    m = s_list[0].max(axis=-1, keepdims=True)
    for s in s_list[1:]:
        m = jnp.maximum(m, s.max(axis=-1, keepdims=True))
    l = None
    o = None
    for s, v in zip(s_list, v_list):
        e = jnp.exp(s - m)
        ls = e.sum(axis=-1, keepdims=True)
        pv = _dot(e.astype(BF16), v)
        l = ls if l is None else l + ls
        o = pv if o is None else o + pv
    return o / l


def _ctx_attn_kernel(q_ref, k_ref, v_ref, qn_ref, qr_ref, kn_ref, kr_ref, mv_ref, ona_ref, omla_ref):
    for p in range(NA_HEADS // 2):
        sl = slice(p * LANES, (p + 1) * LANES)
        kp = k_ref[:, sl]
        vp = v_ref[:, sl]
        outs = []
        for half in range(2):
            s = _dot_nt(_head_q(q_ref[:, sl], half), kp)
            outs.append(_softmax_pv([s], [vp]))
        ona_ref[:, sl] = _pair_out(*outs)

        kk = jnp.concatenate([kn_ref[:, sl], kr_ref[...]], axis=1)
        mvp = mv_ref[:, sl]
        outs = []
        for half in range(2):
            s = _dot_nt(_mla_q(qn_ref, qr_ref, 2 * p + half), kk) * MLA_SCALE
            outs.append(_softmax_pv([s], [mvp]))
        omla_ref[:, sl] = _pair_out(*outs)


def _ctx_attn(q, k, v, qn, qr, kn, kr, mv, *, seq):
    n = q.shape[0]
    tok = lambda w: pl.BlockSpec((seq, w), lambda b: (b, 0))
    return pl.pallas_call(
        _ctx_attn_kernel,
        grid=(n // seq,),
        in_specs=[tok(512), tok(512), tok(512), tok(512), tok(256), tok(512), tok(LANES), tok(512)],
        out_specs=[tok(512), tok(512)],
        out_shape=[jax.ShapeDtypeStruct((n, 512), F32)] * 2,
        compiler_params=_params(("arbitrary",)),
        name="ctx_attn",
    )(q, k, v, qn, qr, kn, kr, mv)


NA_QROWS = 4
NA_KROWS = 12
_NDR = 2 * NA_ROWS - 1
_NDC = 2 * NA_COLS - 1
_Z_NONE = 48
LOG2E = 1.4426950408889634


def _na_tile_descriptors(rows):
    last_q0 = rows - NA_QROWS
    last_k0 = rows - NA_KROWS
    desc = np.full((3, NA_KROWS, NA_QROWS // 2), _Z_NONE, np.int32)
    for case, (q0, k0) in enumerate(((0, 0), (NA_QROWS, 0), (last_q0, last_k0))):
        for kri in range(NA_KROWS):
            for u in range(NA_QROWS // 2):
                info = []
                for ri in (2 * u, 2 * u + 1):
                    qrow, krow = q0 + ri, k0 + kri
                    start = min(max(qrow - NA_ROWS // 2, 0), rows - NA_ROWS)
                    info.append((start <= krow < start + NA_ROWS, krow - qrow))
                (vl, drl), (vr, _) = info
                d = drl + NA_ROWS - 1
                if vl and vr:
                    desc[case, kri, u] = d
                elif vl:
                    desc[case, kri, u] = 16 + d
                elif vr:
                    desc[case, kri, u] = 32 + d
    return desc.reshape(-1)


def _na_bias_consts():
    col = np.arange(GRID_W)
    col_start = np.clip(col - NA_COLS // 2, 0, GRID_W - NA_COLS)
    kc, qc = col[:, None], col[None, :]
    col_in = (kc >= col_start[None, :]) & (kc < col_start[None, :] + NA_COLS)
    dc = kc - qc + NA_COLS - 1
    d = np.stack([((dc == j) & col_in) for j in range(_NDC)]).astype(np.float32)
    negm = np.where(col_in, 0.0, NEG).astype(np.float32)
    return np.concatenate([d, d], axis=2), np.concatenate([negm, negm], axis=1)


def _na_build_bias(bias_ref, d2_ref, negm_ref, z_ref):
    left = _lane_iota() < GRID_W

    def coef(h, dr, j):
        if dr < -(NA_ROWS - 1) or dr > NA_ROWS - 1:
            return 0.0
        return bias_ref[(h * _NDR + dr + NA_ROWS - 1) * _NDC + j]

    def per_head(h, c):
        for d in range(16):
            dr_l = d - (NA_ROWS - 1)
            acc = negm_ref[...]
            for j in range(_NDC):
                acc = acc + jnp.where(left, coef(h, dr_l, j), coef(h, dr_l - 1, j)) * d2_ref[j]
            z_ref[h, d] = acc
            z_ref[h, 16 + d] = jnp.where(left, acc, NEG)
            z_ref[h, 32 + d] = jnp.where(left, NEG, acc)
        z_ref[h, _Z_NONE] = jnp.full((GRID_W, LANES), NEG, F32)
        return c

    lax.fori_loop(0, NA_HEADS, per_head, 0)


def _finish_pair(acc_a, acc_b):
    norm = lambda acc: acc[:HEAD_DIM] * (1.0 / acc[HEAD_DIM:HEAD_DIM + 1])
    return jnp.concatenate([norm(acc_a), norm(acc_b)], axis=0).T


def _na_lat_kernel(bias_ref, desc_ref, q_ref, k_ref, vt_ref, kc_ref, vtc_ref, d2_ref, negm_ref, o_ref, z_ref,
                   *, rows):
    g = pl.program_id(1)
    groups = rows // NA_QROWS

    @pl.when((pl.program_id(0) == 0) & (g == 0))
    def _():
        _na_build_bias(bias_ref, d2_ref, negm_ref, z_ref)

    tq = NA_QROWS * GRID_W
    k_row0 = jnp.clip(g * NA_QROWS - NA_ROWS // 2, 0, rows - NA_KROWS)
    start = pl.multiple_of(k_row0 * GRID_W, KV_CHUNK)
    chunk0 = k_row0 // (KV_CHUNK // GRID_W)
    n_loc = NA_KROWS * GRID_W
    case = jnp.where(g == 0, 0, jnp.where(g == groups - 1, 2, 1))
    upairs = NA_QROWS // 2

    def scores(h):
        sl = slice((h // 2) * LANES, (h // 2 + 1) * LANES)
        qh = _head_q(q_ref[:, sl], h % 2)
        return _dot_nt(k_ref[pl.ds(start, n_loc), sl], qh), _dot_nt(kc_ref[0, :, sl], qh)

    def attend(h, s_loc, s_ctx):
        rows_h = slice(h * VT_ROWS, (h + 1) * VT_ROWS)
        blocks = []
        for kri in range(NA_KROWS):
            tiles = []
            for u in range(upairs):
                idx = desc_ref[(case * NA_KROWS + kri) * upairs + u]
                tiles.append(s_loc[kri * GRID_W:(kri + 1) * GRID_W, u * LANES:(u + 1) * LANES] + z_ref[h, idx])
            blocks.append(jnp.concatenate(tiles, axis=1))
        s_loc = jnp.concatenate(blocks, axis=0)
        m = jnp.maximum(s_loc.max(axis=0, keepdims=True), s_ctx.max(axis=0, keepdims=True))
        p_loc = jnp.exp2((s_loc - m) * LOG2E).astype(BF16)
        p_ctx = jnp.exp2((s_ctx - m) * LOG2E).astype(BF16)
        acc = _dot(vtc_ref[0, rows_h, :], p_ctx)
        for i in range(n_loc // KV_CHUNK):
            acc = acc + _dot(vt_ref[chunk0 + i, rows_h, :], p_loc[i * KV_CHUNK:(i + 1) * KV_CHUNK])
        return acc

    s_next = scores(0)
    accs = []
    for h in range(NA_HEADS):
        s = s_next
        if h + 1 < NA_HEADS:
            s_next = scores(h + 1)
        accs.append(attend(h, *s))
        if h % 2:
            o_ref[:, (h // 2) * LANES:(h // 2 + 1) * LANES] = _finish_pair(accs[h - 1], accs[h])


def _na_lat(q, k, vt, kc, vtc, bias_l, *, batch, seq):
    rows = seq // GRID_W
    n = q.shape[0]
    groups = rows // NA_QROWS
    tq = NA_QROWS * GRID_W
    chunks = seq // KV_CHUNK
    d2, negm = _na_bias_consts()
    desc = jnp.asarray(_na_tile_descriptors(rows))
    smem = pl.BlockSpec(memory_space=pltpu.SMEM)
    full = lambda a: pl.BlockSpec(a.shape, lambda b, g: (0,) * a.ndim)
    return pl.pallas_call(
        functools.partial(_na_lat_kernel, rows=rows),
        grid=(batch, groups),
        in_specs=[
            smem, smem,
            pl.BlockSpec((tq, 512), lambda b, g: (b * groups + g, 0)),
            pl.BlockSpec((seq, 512), lambda b, g: (b, 0)),
            pl.BlockSpec((chunks, VT_ALL, KV_CHUNK), lambda b, g: (b, 0, 0)),
            pl.BlockSpec((1,) + kc.shape[1:], lambda b, g: (b, 0, 0)),
            pl.BlockSpec((1,) + vtc.shape[1:], lambda b, g: (b, 0, 0)),
            full(d2), full(negm),
        ],
        out_specs=pl.BlockSpec((tq, 512), lambda b, g: (b * groups + g, 0)),
        out_shape=jax.ShapeDtypeStruct((n, 512), F32),
        scratch_shapes=[pltpu.VMEM((NA_HEADS, _Z_NONE + 1, GRID_W, LANES), F32)],
        compiler_params=_params(("arbitrary", "arbitrary")),
        name="na_lat",
    )(bias_l.reshape(-1), desc, q, k, vt, kc, vtc, jnp.asarray(d2), jnp.asarray(negm))


_MLA_C = MLA_SCALE * LOG2E


def _mla_lat_kernel(qn_ref, qr_ref, kn_ref, kr_ref, vt_ref, knc_ref, krc_ref, vtc_ref, o_ref,
                    q2_ref, m_ref, acc_ref, sa_ref, sb_ref, *, tk):
    tq = qn_ref.shape[0]
    n_tiles = kn_ref.shape[0] // tk
    cpt = tk // KV_CHUNK
    pairs = MLA_HEADS // 2
    for p in range(pairs):
        q2_ref[p] = jnp.concatenate([_mla_q(qn_ref, qr_ref, 2 * p), _mla_q(qn_ref, qr_ref, 2 * p + 1)], axis=0)
        m_ref[p] = jnp.full((1, 2 * tq), NEG, F32)
        acc_ref[p] = jnp.zeros((2 * VT_ROWS, 2 * tq), F32)

    def scores(p, kk):
        return _dot_nt(kk, q2_ref[p])

    def update(p, s, vts):
        m_prev = m_ref[p]
        m_new = jnp.maximum(m_prev, s.max(axis=0, keepdims=True))
        alpha = jnp.exp2((m_prev - m_new) * _MLA_C)
        pt = jnp.exp2((s - m_new) * _MLA_C).astype(BF16)
        acc = acc_ref[p] * alpha
        for i, vt in enumerate(vts):
            acc = acc + _dot(vt, pt[i * KV_CHUNK:(i + 1) * KV_CHUNK])
        return m_new, acc

    def store(new):
        for p, (m_new, acc) in enumerate(new):
            acc_ref[p] = acc
            m_ref[p] = m_new

    def lanes(p):
        return slice(p * LANES, (p + 1) * LANES)

    def rows(p):
        return slice(2 * p * VT_ROWS, 2 * (p + 1) * VT_ROWS)

    def keys(t, p):
        ks = pl.ds(pl.multiple_of(t * tk, tk), tk)
        return jnp.concatenate([kn_ref[ks, lanes(p)], kr_ref[ks, :]], axis=1)

    for p in range(pairs):
        sa_ref[p] = scores(p, keys(0, p))

    def tile_step(t, cur_ref, nxt_ref):
        t_next = jnp.minimum(t + 1, n_tiles - 1)
        ahead = 2
        for p in range(ahead):
            nxt_ref[p] = scores(p, keys(t_next, p))
        new = []
        for p in range(pairs):
            new.append(update(p, cur_ref[p], [vt_ref[t * cpt + i, rows(p), :] for i in range(cpt)]))
            if p + ahead < pairs:
                nxt_ref[p + ahead] = scores(p + ahead, keys(t_next, p + ahead))
        store(new)

    def body(i, c):
        tile_step(2 * i, sa_ref, sb_ref)
        tile_step(2 * i + 1, sb_ref, sa_ref)
        return c

    lax.fori_loop(0, n_tiles // 2, body, 0)

    ctx_keys = lambda p: jnp.concatenate([knc_ref[0, :, lanes(p)], krc_ref[0]], axis=1)
    new = []
    s_next = scores(0, ctx_keys(0))
    for p in range(pairs):
        s = s_next
        if p + 1 < pairs:
            s_next = scores(p + 1, ctx_keys(p + 1))
        new.append(update(p, s, [vtc_ref[0, rows(p), :]]))
    for p, (_, acc) in enumerate(new):
        o_ref[:, lanes(p)] = _finish_pair(acc[:VT_ROWS, :tq], acc[VT_ROWS:, tq:])


def _mla_lat(qn, qr, kn, kr, vt, knc, krc, vtc, *, batch, seq, tq, tk):
    n = qn.shape[0]
    steps = seq // tq
    chunks = seq // KV_CHUNK
    assert knc.shape[1] == KV_CHUNK
    qtok = lambda w: pl.BlockSpec((tq, w), lambda b, i: (b * steps + i, 0))
    ktok = lambda w: pl.BlockSpec((seq, w), lambda b, i: (b, 0))
    ctok = lambda a: pl.BlockSpec((1,) + a.shape[1:], lambda b, i: (b, 0, 0))
    pairs = MLA_HEADS // 2
    return pl.pallas_call(
        functools.partial(_mla_lat_kernel, tk=tk),
        grid=(batch, steps),
        in_specs=[qtok(512), qtok(256), ktok(512), ktok(LANES),
                  pl.BlockSpec((chunks, VT_ALL, KV_CHUNK), lambda b, i: (b, 0, 0)),
                  ctok(knc), ctok(krc), ctok(vtc)],
        out_specs=qtok(512),
        out_shape=jax.ShapeDtypeStruct((n, 512), F32),
        scratch_shapes=[pltpu.VMEM((pairs, 2 * tq, 2 * LANES), BF16),
                        pltpu.VMEM((pairs, 1, 2 * tq), F32),
                        pltpu.VMEM((pairs, 2 * VT_ROWS, 2 * tq), F32),
                        pltpu.VMEM((pairs, tk, 2 * tq), F32),
                        pltpu.VMEM((pairs, tk, 2 * tq), F32)],
        compiler_params=_params(("arbitrary", "arbitrary")),
        name="mla_lat",
    )(qn, qr, kn, kr, vt, knc, krc, vtc)


def _cache_kv_kernel(ckv_ref, w_ref, wvt_ref, kn_ref, vt_ref):
    ckv = ckv_ref[0, 0].astype(BF16)
    kn_ref[0, 0] = _dot(ckv, w_ref[0]).astype(BF16)
    _store_vt(vt_ref.at[0], _dot_nt(wvt_ref[0], ckv))


def _cache_kv(ckv, wukv_k, wukv_vt):
    _, nb, rows, _ = ckv.shape
    assert rows == KV_CHUNK
    spec = lambda a, b: pl.BlockSpec((1, a, b), lambda l, j: (l, 0, 0))
    return pl.pallas_call(
        _cache_kv_kernel,
        grid=(DEPTH, nb),
        in_specs=[pl.BlockSpec((1, 1, rows, KV_LORA), lambda l, j: (l, j, 0, 0)),
                  spec(KV_LORA, 512), spec(512, KV_LORA)],
        out_specs=[pl.BlockSpec((1, 1, rows, 512), lambda l, j: (l, j, 0, 0)),
                   pl.BlockSpec((1, 1, VT_ALL, KV_CHUNK), lambda l, j: (l, j, 0, 0))],
        out_shape=[jax.ShapeDtypeStruct((DEPTH, nb, rows, 512), BF16),
                   jax.ShapeDtypeStruct((DEPTH, nb, VT_ALL, KV_CHUNK), BF16)],
        compiler_params=_params(("arbitrary", "arbitrary")),
        name="cache_kv",
    )(ckv, wukv_k, wukv_vt)


def _dft_mats(n, scale):
    k = np.arange(n)
    ang = 2.0 * np.pi * ((k[:, None] * k[None, :]) % n) / n
    return np.cos(ang) * scale, np.sin(ang) * scale


def _hi_lo_np(m):
    m = jnp.asarray(m, F32)
    hi = m.astype(BF16)
    lo = (m - hi.astype(F32)).astype(BF16)
    return hi, lo


def _chan_dft(u, wc_ref):
    uh, ul = _split(u)
    wh = wc_ref[0]
    wl = wc_ref[1]
    return _dot(uh, wh) + _dot(ul, wh) + _dot(uh, wl)


def _fn_ctx_kernel(u_ref, wc_ref, fp_ref, o_ref):
    for g in range(FN_GROUPS):
        sl = slice(g * FN_GROUP_W, (g + 1) * FN_GROUP_W)
        a = _chan_dft(u_ref[:, sl], wc_ref)
        ar_h, ar_l = _split(a[:, :FN_GROUP_W])
        ai_h, ai_l = _split(a[:, FN_GROUP_W:])
        rh = jnp.concatenate([ar_h, ai_h], axis=0)
        rl = jnp.concatenate([ar_l, ai_l], axis=0)
        o_ref[:, sl] = _dot(fp_ref[0], rh) + _dot(fp_ref[0], rl) + _dot(fp_ref[1], rh)


def _fn_ctx(u, *, seq):
    n = u.shape[0]
    cc, sc = _dft_mats(FN_GROUP_W, FN_GROUP_W ** -0.5)
    wc = jnp.stack(_hi_lo_np(np.concatenate([cc, -sc], axis=1)))
    cp, sp = _dft_mats(seq, seq ** -0.5)
    fp = jnp.stack(_hi_lo_np(np.concatenate([cp, sp], axis=1)))
    full = lambda a: pl.BlockSpec(a.shape, lambda b: (0,) * a.ndim)
    return pl.pallas_call(
        _fn_ctx_kernel,
        grid=(n // seq,),
        in_specs=[pl.BlockSpec((seq, FN_WIDTH), lambda b: (b, 0)), full(wc), full(fp)],
        out_specs=pl.BlockSpec((seq, FN_WIDTH), lambda b: (b, 0)),
        out_shape=jax.ShapeDtypeStruct((n, FN_WIDTH), F32),
        compiler_params=_params(("arbitrary",)),
        name="fn_ctx",
    )(u, wc, fp)


_FN_G = 8


def _fn_lat_kernel(u_ref, wc_ref, f1_ref, f2_ref, twc_ref, tws_ref, o_ref, a_ref, t_ref):
    n = GRID_W
    w = FN_GROUP_W

    def chan(i, c):
        rows = pl.ds(pl.multiple_of(i * 512, 512), 512)
        a = _chan_dft(u_ref[rows, :], wc_ref)
        a_ref[0, rows, :] = a[:, :FN_GROUP_W]
        a_ref[1, rows, :] = a[:, FN_GROUP_W:]
        return c

    lax.fori_loop(0, u_ref.shape[0] // 512, chan, 0)

    def gather(ref, base):
        cols = [jnp.concatenate([ref[0, pl.ds(base + j, n, stride=n), :],
                                 ref[1, pl.ds(base + j, n, stride=n), :]], axis=0) for j in range(_FN_G)]
        return jnp.concatenate(cols, axis=1)

    def dft(f_ref, d):
        rows = f_ref.shape[0] // 2
        dh, dl = _split(d)
        rh = _dot(f_ref[...], dh)
        return rh[:rows] + rh[rows:] + _dot(f_ref[:rows, :], dl)

    def stage1(i, c):
        base = i * _FN_G
        b = dft(f1_ref, gather(a_ref, base))
        for j in range(_FN_G):
            br = b[:n, j * w:(j + 1) * w]
            bi = b[n:, j * w:(j + 1) * w]
            rows = pl.ds(pl.multiple_of((base + j) * n, n), n)
            tc = twc_ref[rows, :]
            ts = tws_ref[rows, :]
            t_ref[0, rows, :] = br * tc + bi * ts
            t_ref[1, rows, :] = bi * tc - br * ts
        return c

    lax.fori_loop(0, n // _FN_G, stage1, 0)

    def stage2(i, c):
        base = i * _FN_G
        y = dft(f2_ref, gather(t_ref, base))
        for j in range(_FN_G):
            o_ref[pl.ds(base + j, n, stride=n), :] = y[:, j * w:(j + 1) * w]
        return c

    lax.fori_loop(0, n // _FN_G, stage2, 0)


def _fn_lat(u, *, batch, seq):
    n = GRID_W
    assert seq == n * n
    cc, sc = _dft_mats(FN_GROUP_W, FN_GROUP_W ** -0.5)
    wc = jnp.stack(_hi_lo_np(np.concatenate([cc, -sc], axis=1)))
    c1, s1 = _dft_mats(n, 1.0)
    c2, s2 = _dft_mats(n, 1.0 / n)
    f1 = jnp.concatenate(_hi_lo_np(np.block([[c1, s1], [-s1, c1]])), axis=0)
    f2 = jnp.concatenate(_hi_lo_np(np.concatenate([c2, s2], axis=1)), axis=0)
    n2 = np.arange(n)[:, None]
    k1 = np.arange(n)[None, :]
    ang = (2.0 * np.pi * ((n2 * k1) % seq) / seq).reshape(seq, 1)
    twc = jnp.asarray(np.broadcast_to(np.cos(ang), (seq, FN_GROUP_W)), F32)
    tws = jnp.asarray(np.broadcast_to(np.sin(ang), (seq, FN_GROUP_W)), F32)
    full = lambda a: pl.BlockSpec(a.shape, lambda b, g: (0,) * a.ndim)
    blk = pl.BlockSpec((seq, FN_GROUP_W), lambda b, g: (b, g))
    return pl.pallas_call(
        _fn_lat_kernel,
        grid=(batch, FN_GROUPS),
        in_specs=[blk, full(wc), full(f1), full(f2), full(twc), full(tws)],
        out_specs=blk,
        out_shape=jax.ShapeDtypeStruct(u.shape, F32),
        scratch_shapes=[pltpu.VMEM((2, seq, FN_GROUP_W), F32), pltpu.VMEM((2, seq, FN_GROUP_W), F32)],
        compiler_params=_params(("arbitrary", "arbitrary")),
        name="fn_lat",
    )(u, wc, f1, f2, twc, tws)


_E_GNA, _E_GMLA, _E_GFN, _E_MERGE = 0, 512, 1024, 1536


def _merge_kernel(*refs, final):
    (x_ref, mod_ref, ng_ref, wg_ref, ona_ref, omla_ref, ofn_ref, wona_ref, womla_ref, wofn_ref,
     wout_ref) = refs[:11]
    if final:
        fg_ref, o_ref = refs[11:13]
    else:
        o_ref = refs[11]
    x, xm = _modulated(x_ref, mod_ref, ng_ref)
    gate = mod_ref[0, 2:3, :]

    merged = None
    for i, (br_ref, wo_ref) in enumerate(((ona_ref, wona_ref), (omla_ref, womla_ref), (ofn_ref, wofn_ref))):
        g = _dot_nt(xm, wg_ref[0, i * 512:(i + 1) * 512, :])
        t = _dot((br_ref[...] * _silu(g)).astype(BF16), wo_ref[0])
        ml = _dot_nt(xm, wg_ref[0, _E_MERGE + i * D_MODEL:_E_MERGE + (i + 1) * D_MODEL, :])
        t = _sigmoid(ml) * t
        merged = t if merged is None else merged + t
    h = x + gate * _dot(merged.astype(BF16), wout_ref[0])
    if final:
        h = _rms(h, fg_ref[...])
    o_ref[...] = h


def _layer_spec(a, layer):
    return pl.BlockSpec((1,) + a.shape[1:], lambda *_: (layer,) + (0,) * (a.ndim - 1))


def _merge(x, mod, ng, w_g, ona, omla, ofn, wona, womla, wofn, wout, fg, *, tile, layer):
    n = x.shape[0]
    groups = mod.shape[0]
    steps = n // tile
    per_group = steps // groups
    final = fg is not None
    tok = lambda w: pl.BlockSpec((tile, w), lambda i: (i, 0))
    full = lambda a: pl.BlockSpec(a.shape, lambda i: (0,) * a.ndim)
    lay = lambda a: _layer_spec(a, layer)
    in_specs = [tok(D_MODEL), pl.BlockSpec((1, 3, D_MODEL), lambda i: (i // per_group, 0, 0)),
                full(ng), lay(w_g), tok(512), tok(512), tok(512),
                lay(wona), lay(womla), lay(wofn), lay(wout)]
    args = [x, mod, ng, w_g, ona, omla, ofn, wona, womla, wofn, wout]
    if final:
        in_specs.append(full(fg))
        args.append(fg)
    return pl.pallas_call(
        functools.partial(_merge_kernel, final=final),
        grid=(steps,),
        in_specs=in_specs,
        out_specs=tok(D_MODEL),
        out_shape=jax.ShapeDtypeStruct((n, D_MODEL), F32),
        compiler_params=_params(("arbitrary",)),
        name="merge",
    )(*args)


def _rope_rot(w, axis):
    q = MLA_ROPE // 4
    part = lambda i: lax.slice_in_dim(w, i * q, (i + 1) * q, axis=axis)
    return jnp.concatenate([-part(1), part(0), -part(3), part(2)], axis=axis)


def _prep_w_in(w_in):
    wt = jnp.swapaxes(w_in, 1, 2)
    rows = lambda start, n: wt[:, start:start + n]
    kr = rows(_O_KROPE, MLA_ROPE)
    kr_rot = _rope_rot(kr, 1)
    a = jnp.concatenate([rows(_O_QKV, 1536), rows(_O_QLAT, Q_LORA + KV_LORA),
                         jnp.tile(kr, (1, 4, 1)), jnp.tile(kr_rot, (1, 4, 1)), rows(_O_UFN, FN_WIDTH)], axis=1)
    e = jnp.concatenate([rows(_O_GATE_NA, 512), rows(_O_GATE_MLA, 512), rows(_O_GATE_FN, 512),
                         rows(_O_MERGE, 3 * D_MODEL)], axis=1)
    return a.astype(BF16), e.astype(BF16)


def _prep_w_uq(w):
    w3 = w.reshape(Q_LORA, MLA_HEADS, MLA_NOPE + MLA_ROPE)
    nope = w3[:, :, :MLA_NOPE].reshape(Q_LORA, MLA_HEADS * MLA_NOPE)
    rope = w3[:, :, MLA_NOPE:]
    rot = _rope_rot(rope, 2)
    return jnp.concatenate([nope, rope.reshape(Q_LORA, -1), rot.reshape(Q_LORA, -1)], axis=1).astype(BF16)


def _prep_w_ukv(w):
    w3 = w.reshape(KV_LORA, MLA_HEADS, MLA_NOPE + MLA_V)
    return jnp.concatenate([w3[:, :, :MLA_NOPE].reshape(KV_LORA, -1),
                            w3[:, :, MLA_NOPE:].reshape(KV_LORA, -1)], axis=1).astype(BF16)


def _rope_tables(n):
    t = jnp.arange(n, dtype=jnp.int32)
    row = (t // GRID_W).astype(F32)
    col = (t % GRID_W).astype(F32)
    half = MLA_ROPE // 2
    inv_freq = ROPE_THETA ** (-jnp.arange(0, half, 2, dtype=F32) / half)
    ar = row[:, None] * inv_freq[None, :]
    ac = col[:, None] * inv_freq[None, :]
    ang = jnp.concatenate([ar, ar, ac, ac], axis=-1)
    return jnp.tile(jnp.cos(ang), (1, MLA_HEADS)), jnp.tile(jnp.sin(ang), (1, MLA_HEADS))


def kernel(x_prompt, x_sample, cache_na_k, cache_na_v, cache_mla_ckv, cache_mla_krope, c, c_ctx,
           w_ada, b_ada, norm_g, w_in, q_norm_g, kv_norm_g, w_uq, w_ukv, na_bias,
           w_o_na, w_o_mla, w_o_fourier, w_out, final_norm_g):
    batch, seq, _ = x_prompt.shape
    dbatch, dseq, _ = x_sample.shape
    past = cache_na_k.shape[2]

    cond = jnp.zeros((8, D_MODEL), F32).at[0].set(c_ctx).at[1:1 + dbatch].set(c)
    mods = _ada_mods(cond, w_ada, b_ada).reshape(DEPTH, 8, 3, D_MODEL)

    wukv_all = jnp.stack([_prep_w_ukv(w_ukv[l]) for l in range(DEPTH)])
    wukv_vt_all = wukv_all[:, :, 512:].transpose(0, 2, 1)
    knc_all, mvtc_all = _cache_kv(cache_mla_ckv.transpose(1, 0, 2, 3), wukv_all[:, :, :512], wukv_vt_all)
    krc_all = jnp.tile(cache_mla_krope, (1, 1, 1, 4)).astype(BF16)
    nakc_all = cache_na_k.reshape(dbatch, DEPTH, past, NA_WIDTH).astype(BF16)
    navt = cache_na_v.transpose(0, 1, 3, 4, 2).astype(BF16)
    navtc_all = jnp.concatenate([navt, jnp.ones((dbatch, DEPTH, NA_HEADS, ONES_ROWS, past), BF16)],
                                axis=3).reshape(dbatch, DEPTH, VT_ALL, past)
    cos, sin = _rope_tables(dseq)

    h_ctx = x_prompt.reshape(batch * seq, D_MODEL)
    h_lat = x_sample.reshape(dbatch * dseq, D_MODEL)
    ks, vs, ckvs, krs = [], [], [], []
    row = lambda a: a.reshape(1, -1)
    w_a, w_e = _prep_w_in(w_in)
    wona, womla, wofn, wout = (w_o_na.astype(BF16), w_o_mla.astype(BF16),
                               w_o_fourier.astype(BF16), w_out.astype(BF16))
    for l in range(DEPTH):
        wuq = _prep_w_uq(w_uq[l])
        wukv = wukv_all[l]
        ng, qng, kvng = row(norm_g[l]), row(q_norm_g[l]), row(kv_norm_g[l])
        fg = row(final_norm_g) if l == DEPTH - 1 else None
        mod_ctx = mods[l, 0:1]
        mod_lat = mods[l, 1:1 + dbatch]

        (q, k, v, qn, qr, kn, kr, mv, u, kf, vf, ckvf, krf) = _inproj(
            h_ctx, mod_ctx, ng, w_a, qng, kvng, wuq, wukv, None, tile=512, layer=l, seq=seq)
        ks.append(kf)
        vs.append(vf)
        ckvs.append(ckvf)
        krs.append(krf)
        ona, omla = _ctx_attn(q, k, v, qn, qr, kn, kr, mv, seq=seq)
        ofn = _fn_ctx(u, seq=seq)
        h_ctx = _merge(h_ctx, mod_ctx, ng, w_e, ona, omla, ofn, wona, womla, wofn, wout, fg, tile=512, layer=l)

        (q, k, vt, qn, qr, kn, kr, mvt, u) = _inproj(
            h_lat, mod_lat, ng, w_a, qng, kvng, wuq, wukv, (wukv_vt_all[l], cos, sin), tile=512, layer=l)
        ona = _na_lat(q, k, vt, nakc_all[:, l], navtc_all[:, l], na_bias[l], batch=dbatch, seq=dseq)
        omla = _mla_lat(qn, qr, kn, kr, mvt, knc_all[l], krc_all[:, l], mvtc_all[l],
                        batch=dbatch, seq=dseq, tq=256, tk=512)
        ofn = _fn_lat(u, batch=dbatch, seq=dseq)
        h_lat = _merge(h_lat, mod_lat, ng, w_e, ona, omla, ofn, wona, womla, wofn, wout, fg, tile=512, layer=l)

    y_prompt = h_ctx.reshape(batch, seq, D_MODEL)
    y_sample = h_lat.reshape(dbatch, dseq, D_MODEL)
    heads_t = lambda xs: jnp.stack(xs, axis=1).reshape(
        batch, DEPTH, NA_HEADS, NA_HEAD_DIM, seq).transpose(0, 1, 4, 2, 3)
    new_na_k = heads_t(ks)
    new_na_v = heads_t(vs)
    new_mla_ckv = jnp.stack([a.reshape(batch, seq, KV_LORA) for a in ckvs], axis=1)
    new_mla_krope = jnp.stack(krs, axis=1).transpose(0, 1, 3, 2)
    return (y_prompt, y_sample, new_na_k, new_na_v, new_mla_ckv, new_mla_krope)
```

```python
import functools

import jax
import jax.numpy as jnp
import numpy as np
from jax import lax
from jax.experimental import pallas as pl
from jax.experimental.pallas import tpu as pltpu

F32 = jnp.float32
BF16 = jnp.bfloat16

D_MODEL = 1024
DEPTH = 4
GRID_W = 64
NA_HEADS = 8
NA_HEAD_DIM = 64
NA_WIDTH = 512
NA_ROWS = 8
NA_COLS = 16
MLA_HEADS = 8
MLA_NOPE = 64
MLA_ROPE = 32
MLA_V = 64
MLA_WIDTH = 512
Q_LORA = 256
KV_LORA = 128
MLA_SCALE = (MLA_NOPE + MLA_ROPE) ** -0.5
ROPE_THETA = 10000.0
FN_GROUPS = 4
FN_GROUP_W = 128
FN_WIDTH = 512
EPS = 1e-6
NEG = -1e30

_O_QKV = 0
_O_GATE_NA = 1536
_O_QLAT = 2048
_O_CKV = 2304
_O_KROPE = 2432
_O_GATE_MLA = 2464
_O_UFN = 2976
_O_GATE_FN = 3488
_O_MERGE = 4000
D_IN = 7072

LANES = 128
VMEM_LIMIT = 56 * 1024 * 1024


def _params(sem):
    return pltpu.CompilerParams(dimension_semantics=sem, vmem_limit_bytes=VMEM_LIMIT)


def _sigmoid(x):
    return 1.0 / (1.0 + jnp.exp(-x))


def _silu(x):
    return x * _sigmoid(x)


def _rms(x, g):
    return x * lax.rsqrt(jnp.mean(x * x, axis=-1, keepdims=True) + EPS) * g


def _dot(a, b):
    return jnp.dot(a, b, preferred_element_type=F32)


def _dot_nt(a, b):
    return lax.dot_general(a, b, (((1,), (1,)), ((), ())), preferred_element_type=F32)


def _split(x):
    hi = x.astype(BF16)
    lo = (x - hi.astype(F32)).astype(BF16)
    return hi, lo


def _ada_kernel(cond_ref, w_ref, b_ref, o_ref):
    a = _silu(cond_ref[...]).astype(BF16)
    o_ref[0] = _dot(a, w_ref[0].astype(BF16)) + b_ref[0]


def _ada_mods(cond, w_ada, b_ada):
    nb = 3
    return pl.pallas_call(
        _ada_kernel,
        grid=(DEPTH, nb),
        in_specs=[
            pl.BlockSpec((8, D_MODEL), lambda l, j: (0, 0)),
            pl.BlockSpec((1, D_MODEL, D_MODEL), lambda l, j: (l, 0, j)),
            pl.BlockSpec((1, 1, D_MODEL), lambda l, j: (l, 0, j)),
        ],
        out_specs=pl.BlockSpec((1, 8, D_MODEL), lambda l, j: (l, 0, j)),
        out_shape=jax.ShapeDtypeStruct((DEPTH, 8, 3 * D_MODEL), F32),
        compiler_params=_params(("arbitrary", "arbitrary")),
        name="ada_mods",
    )(cond, w_ada, b_ada.reshape(DEPTH, 1, 3 * D_MODEL))


_A_Q, _A_K, _A_V, _A_QLAT, _A_CKV, _A_KR, _A_KRROT, _A_U, _A_END = (
    0, 512, 1024, 1536, 1792, 1920, 2048, 2176, 2688)


def _modulated(x_ref, mod_ref, ng_ref):
    x = x_ref[...]
    shift = mod_ref[0, 0:1, :]
    scale = mod_ref[0, 1:2, :]
    return x, (_rms(x, ng_ref[...]) * (1.0 + scale) + shift).astype(BF16)


KV_CHUNK = 256
HEAD_DIM = 64
ONES_ROWS = 16
VT_ROWS = HEAD_DIM + ONES_ROWS
VT_ALL = 8 * VT_ROWS


def _store_vt(vt_ref, vt):
    for c in range(vt_ref.shape[0]):
        cols = slice(c * KV_CHUNK, (c + 1) * KV_CHUNK)
        for h in range(8):
            vt_ref[c, h * VT_ROWS:h * VT_ROWS + HEAD_DIM, :] = vt[h * HEAD_DIM:(h + 1) * HEAD_DIM, cols].astype(BF16)
            vt_ref[c, h * VT_ROWS + HEAD_DIM:(h + 1) * VT_ROWS, :] = jnp.ones((ONES_ROWS, KV_CHUNK), BF16)


def _store_seq_t(ref, xt):
    seq = ref.shape[2]
    for j in range(ref.shape[0]):
        ref[j] = xt[:, j * seq:(j + 1) * seq]


def _inproj_kernel(*refs, latent):
    (x_ref, mod_ref, ng_ref, w_ref, qng_ref, kvng_ref, wuq_ref, wukv_ref) = refs[:8]
    if latent:
        (wmvt_ref, cos_ref, sin_ref,
         naq_ref, nak_ref, navt_ref, qn_ref, qr_ref, kn_ref, kr_ref, mvt_ref, u_ref) = refs[8:]
    else:
        (naq_ref, nak_ref, nav_ref, qn_ref, qr_ref, kn_ref, kr_ref, mv_ref, u_ref,
         kf_ref, vf_ref, ckvf_ref, krf_ref) = refs[8:]

    _, xm = _modulated(x_ref, mod_ref, ng_ref)

    def proj(a, b):
        return _dot_nt(xm, w_ref[0, a:b, :])

    q = proj(_A_Q, _A_K)
    naq_ref[...] = (q * (NA_HEAD_DIM ** -0.5)).astype(BF16)
    k = proj(_A_K, _A_V)
    nak_ref[...] = k.astype(BF16)
    if latent:
        _store_vt(navt_ref, _dot_nt(w_ref[0, _A_V:_A_QLAT, :], xm))
    else:
        v = proj(_A_V, _A_QLAT)
        nav_ref[...] = v.astype(BF16)
        _store_seq_t(kf_ref, k.T)
        _store_seq_t(vf_ref, v.T)

    qlat = _rms(proj(_A_QLAT, _A_CKV), qng_ref[...]).astype(BF16)
    qq = _dot(qlat, wuq_ref[...])
    qn_ref[...] = qq[:, :512].astype(BF16)
    qr = qq[:, 512:768]
    if latent:
        qr = qr * cos_ref[...] + qq[:, 768:1024] * sin_ref[...]
    qr_ref[...] = qr.astype(BF16)

    ckv = _rms(proj(_A_CKV, _A_KR), kvng_ref[...])
    ckv_b = ckv.astype(BF16)
    kn_ref[...] = _dot(ckv_b, wukv_ref[:, :512]).astype(BF16)
    if latent:
        _store_vt(mvt_ref, _dot_nt(wmvt_ref[...], ckv_b))
    else:
        ckvf_ref[...] = ckv
        mv_ref[...] = _dot(ckv_b, wukv_ref[:, 512:]).astype(BF16)

    kr = proj(_A_KR, _A_KRROT)
    if latent:
        kr = kr * cos_ref[:, :LANES] + proj(_A_KRROT, _A_U) * sin_ref[:, :LANES]
    else:
        _store_seq_t(krf_ref, kr.T[:MLA_ROPE])
    kr_ref[...] = kr.astype(BF16)

    u_ref[...] = proj(_A_U, _A_END)


def _inproj(x, mod, ng, w_a, qng, kvng, wuq, wukv, lat_extra, *, tile, layer, seq=None):
    n = x.shape[0]
    groups = mod.shape[0]
    steps = n // tile
    per_group = steps // groups
    latent = lat_extra is not None
    tok = lambda w: pl.BlockSpec((tile, w), lambda i: (i, 0))
    full = lambda a: pl.BlockSpec(a.shape, lambda i: (0,) * a.ndim)
    in_specs = [tok(D_MODEL),
                pl.BlockSpec((1, 3, D_MODEL), lambda i: (i // per_group, 0, 0)),
                full(ng), _layer_spec(w_a, layer), full(qng), full(kvng), full(wuq), full(wukv)]
    args = [x, mod, ng, w_a, qng, kvng, wuq, wukv]
    out = lambda w, dt: (jax.ShapeDtypeStruct((n, w), dt), tok(w))
    if latent:
        wmvt, cos, sin = lat_extra
        steps_per_seq = cos.shape[0] // tile
        in_specs += [full(wmvt)] + [pl.BlockSpec((tile, 256), lambda i: (i % steps_per_seq, 0))] * 2
        args += [wmvt, cos, sin]
        cpt = tile // KV_CHUNK
        vt = (jax.ShapeDtypeStruct((n // KV_CHUNK, VT_ALL, KV_CHUNK), BF16),
              pl.BlockSpec((cpt, VT_ALL, KV_CHUNK), lambda i: (i, 0, 0)))
        outs = [out(512, BF16), out(512, BF16), vt, out(512, BF16), out(256, BF16), out(512, BF16),
                out(LANES, BF16), vt, out(512, F32)]
    else:
        out_t = lambda w: (jax.ShapeDtypeStruct((n // seq, w, seq), F32),
                           pl.BlockSpec((tile // seq, w, seq), lambda i: (i, 0, 0)))
        outs = [out(512, BF16)] * 3 + [out(512, BF16), out(256, BF16), out(512, BF16), out(LANES, BF16),
                                      out(512, BF16), out(512, F32),
                                      out_t(512), out_t(512), out(KV_LORA, F32), out_t(MLA_ROPE)]
    return pl.pallas_call(
        functools.partial(_inproj_kernel, latent=latent),
        grid=(steps,),
        in_specs=in_specs,
        out_specs=[o[1] for o in outs],
        out_shape=[o[0] for o in outs],
        compiler_params=_params(("arbitrary",)),
        name="inproj_lat" if latent else "inproj_ctx",
    )(*args)


def _lane_iota(n=LANES):
    return lax.broadcasted_iota(jnp.int32, (1, n), 1)


def _head_q(q_pair, half):
    lane = _lane_iota()
    keep = (lane < NA_HEAD_DIM) if half == 0 else (lane >= NA_HEAD_DIM)
    return jnp.where(keep, q_pair, jnp.zeros_like(q_pair))


def _mla_q(qn_ref, qr_ref, h):
    p, half = divmod(h, 2)
    qa = _head_q(qn_ref[:, p * LANES:(p + 1) * LANES], half)
    g, slot = divmod(h, 4)
    qb = qr_ref[:, g * LANES:(g + 1) * LANES]
    lane = _lane_iota()
    keep = (lane >= slot * MLA_ROPE) & (lane < (slot + 1) * MLA_ROPE)
    qb = jnp.where(keep, qb, jnp.zeros_like(qb))
    return jnp.concatenate([qa, qb], axis=1)


def _pair_out(o0, o1):
    return jnp.where(_lane_iota() < NA_HEAD_DIM, o0, o1)


def _softmax_pv(s_list, v_list):
    m = s_list[0].max(axis=-1, keepdims=True)
    for s in s_list[1:]:
        m = jnp.maximum(m, s.max(axis=-1, keepdims=True))
    l = None
    o = None
    for s, v in zip(s_list, v_list):
        e = jnp.exp(s - m)
        ls = e.sum(axis=-1, keepdims=True)
        pv = _dot(e.astype(BF16), v)
        l = ls if l is None else l + ls
        o = pv if o is None else o + pv
    return o / l


def _ctx_attn_kernel(q_ref, k_ref, v_ref, qn_ref, qr_ref, kn_ref, kr_ref, mv_ref, ona_ref, omla_ref):
    for p in range(NA_HEADS // 2):
        sl = slice(p * LANES, (p + 1) * LANES)
        kp = k_ref[:, sl]
        vp = v_ref[:, sl]
        outs = []
        for half in range(2):
            s = _dot_nt(_head_q(q_ref[:, sl], half), kp)
            outs.append(_softmax_pv([s], [vp]))
        ona_ref[:, sl] = _pair_out(*outs)

        kk = jnp.concatenate([kn_ref[:, sl], kr_ref[...]], axis=1)
        mvp = mv_ref[:, sl]
        outs = []
        for half in range(2):
            s = _dot_nt(_mla_q(qn_ref, qr_ref, 2 * p + half), kk) * MLA_SCALE
            outs.append(_softmax_pv([s], [mvp]))
        omla_ref[:, sl] = _pair_out(*outs)


def _ctx_attn(q, k, v, qn, qr, kn, kr, mv, *, seq):
    n = q.shape[0]
    tok = lambda w: pl.BlockSpec((seq, w), lambda b: (b, 0))
    return pl.pallas_call(
        _ctx_attn_kernel,
        grid=(n // seq,),
        in_specs=[tok(512), tok(512), tok(512), tok(512), tok(256), tok(512), tok(LANES), tok(512)],
        out_specs=[tok(512), tok(512)],
        out_shape=[jax.ShapeDtypeStruct((n, 512), F32)] * 2,
        compiler_params=_params(("arbitrary",)),
        name="ctx_attn",
    )(q, k, v, qn, qr, kn, kr, mv)


NA_QROWS = 4
NA_KROWS = 12
_NDR = 2 * NA_ROWS - 1
_NDC = 2 * NA_COLS - 1
_Z_NONE = 48
LOG2E = 1.4426950408889634


def _na_tile_descriptors(rows):
    last_q0 = rows - NA_QROWS
    last_k0 = rows - NA_KROWS
    desc = np.full((3, NA_KROWS, NA_QROWS // 2), _Z_NONE, np.int32)
    for case, (q0, k0) in enumerate(((0, 0), (NA_QROWS, 0), (last_q0, last_k0))):
        for kri in range(NA_KROWS):
            for u in range(NA_QROWS // 2):
                info = []
                for ri in (2 * u, 2 * u + 1):
                    qrow, krow = q0 + ri, k0 + kri
                    start = min(max(qrow - NA_ROWS // 2, 0), rows - NA_ROWS)
                    info.append((start <= krow < start + NA_ROWS, krow - qrow))
                (vl, drl), (vr, _) = info
                d = drl + NA_ROWS - 1
                if vl and vr:
                    desc[case, kri, u] = d
                elif vl:
                    desc[case, kri, u] = 16 + d
                elif vr:
                    desc[case, kri, u] = 32 + d
    return desc.reshape(-1)


def _na_bias_consts():
    col = np.arange(GRID_W)
    col_start = np.clip(col - NA_COLS // 2, 0, GRID_W - NA_COLS)
    kc, qc = col[:, None], col[None, :]
    col_in = (kc >= col_start[None, :]) & (kc < col_start[None, :] + NA_COLS)
    dc = kc - qc + NA_COLS - 1
    d = np.stack([((dc == j) & col_in) for j in range(_NDC)]).astype(np.float32)
    negm = np.where(col_in, 0.0, NEG).astype(np.float32)
    return np.concatenate([d, d], axis=2), np.concatenate([negm, negm], axis=1)


def _na_build_bias(bias_ref, d2_ref, negm_ref, z_ref):
    left = _lane_iota() < GRID_W

    def coef(h, dr, j):
        if dr < -(NA_ROWS - 1) or dr > NA_ROWS - 1:
            return 0.0
        return bias_ref[(h * _NDR + dr + NA_ROWS - 1) * _NDC + j]

    def per_head(h, c):
        for d in range(16):
            dr_l = d - (NA_ROWS - 1)
            acc = negm_ref[...]
            for j in range(_NDC):
                acc = acc + jnp.where(left, coef(h, dr_l, j), coef(h, dr_l - 1, j)) * d2_ref[j]
            z_ref[h, d] = acc
            z_ref[h, 16 + d] = jnp.where(left, acc, NEG)
            z_ref[h, 32 + d] = jnp.where(left, NEG, acc)
        z_ref[h, _Z_NONE] = jnp.full((GRID_W, LANES), NEG, F32)
        return c

    lax.fori_loop(0, NA_HEADS, per_head, 0)


def _finish_pair(acc_a, acc_b):
    norm = lambda acc: acc[:HEAD_DIM] * (1.0 / acc[HEAD_DIM:HEAD_DIM + 1])
    return jnp.concatenate([norm(acc_a), norm(acc_b)], axis=0).T


def _na_lat_kernel(bias_ref, desc_ref, q_ref, k_ref, vt_ref, kc_ref, vtc_ref, d2_ref, negm_ref, o_ref, z_ref,
                   *, rows):
    g = pl.program_id(1)
    groups = rows // NA_QROWS

    @pl.when((pl.program_id(0) == 0) & (g == 0))
    def _():
        _na_build_bias(bias_ref, d2_ref, negm_ref, z_ref)

    tq = NA_QROWS * GRID_W
    k_row0 = jnp.clip(g * NA_QROWS - NA_ROWS // 2, 0, rows - NA_KROWS)
    start = pl.multiple_of(k_row0 * GRID_W, KV_CHUNK)
    chunk0 = k_row0 // (KV_CHUNK // GRID_W)
    n_loc = NA_KROWS * GRID_W
    case = jnp.where(g == 0, 0, jnp.where(g == groups - 1, 2, 1))
    upairs = NA_QROWS // 2

    def scores(h):
        sl = slice((h // 2) * LANES, (h // 2 + 1) * LANES)
        qt = _head_q(q_ref[:, sl], h % 2).astype(F32).T.astype(BF16)
        return _dot(k_ref[pl.ds(start, n_loc), sl], qt), _dot(kc_ref[0, :, sl], qt)

    def attend(h, s_loc, s_ctx):
        rows_h = slice(h * VT_ROWS, (h + 1) * VT_ROWS)
        blocks = []
        for kri in range(NA_KROWS):
            tiles = []
            for u in range(upairs):
                idx = desc_ref[(case * NA_KROWS + kri) * upairs + u]
                tiles.append(s_loc[kri * GRID_W:(kri + 1) * GRID_W, u * LANES:(u + 1) * LANES] + z_ref[h, idx])
            blocks.append(jnp.concatenate(tiles, axis=1))
        s_loc = jnp.concatenate(blocks, axis=0)
        m = jnp.maximum(s_loc.max(axis=0, keepdims=True), s_ctx.max(axis=0, keepdims=True))
        p_loc = jnp.exp2((s_loc - m) * LOG2E).astype(BF16)
        p_ctx = jnp.exp2((s_ctx - m) * LOG2E).astype(BF16)
        acc = _dot(vtc_ref[0, rows_h, :], p_ctx)
        for i in range(n_loc // KV_CHUNK):
            acc = acc + _dot(vt_ref[chunk0 + i, rows_h, :], p_loc[i * KV_CHUNK:(i + 1) * KV_CHUNK])
        return acc

    s_next = scores(0)
    accs = []
    for h in range(NA_HEADS):
        s = s_next
        if h + 1 < NA_HEADS:
            s_next = scores(h + 1)
        accs.append(attend(h, *s))
        if h % 2:
            o_ref[:, (h // 2) * LANES:(h // 2 + 1) * LANES] = _finish_pair(accs[h - 1], accs[h])


def _na_lat(q, k, vt, kc, vtc, bias_l, *, batch, seq):
    rows = seq // GRID_W
    n = q.shape[0]
    groups = rows // NA_QROWS
    tq = NA_QROWS * GRID_W
    chunks = seq // KV_CHUNK
    d2, negm = _na_bias_consts()
    desc = jnp.asarray(_na_tile_descriptors(rows))
    smem = pl.BlockSpec(memory_space=pltpu.SMEM)
    full = lambda a: pl.BlockSpec(a.shape, lambda b, g: (0,) * a.ndim)
    return pl.pallas_call(
        functools.partial(_na_lat_kernel, rows=rows),
        grid=(batch, groups),
        in_specs=[
            smem, smem,
            pl.BlockSpec((tq, 512), lambda b, g: (b * groups + g, 0)),
            pl.BlockSpec((seq, 512), lambda b, g: (b, 0)),
            pl.BlockSpec((chunks, VT_ALL, KV_CHUNK), lambda b, g: (b, 0, 0)),
            pl.BlockSpec((1,) + kc.shape[1:], lambda b, g: (b, 0, 0)),
            pl.BlockSpec((1,) + vtc.shape[1:], lambda b, g: (b, 0, 0)),
            full(d2), full(negm),
        ],
        out_specs=pl.BlockSpec((tq, 512), lambda b, g: (b * groups + g, 0)),
        out_shape=jax.ShapeDtypeStruct((n, 512), F32),
        scratch_shapes=[pltpu.VMEM((NA_HEADS, _Z_NONE + 1, GRID_W, LANES), F32)],
        compiler_params=_params(("arbitrary", "arbitrary")),
        name="na_lat",
    )(bias_l.reshape(-1), desc, q, k, vt, kc, vtc, jnp.asarray(d2), jnp.asarray(negm))


_MLA_C = MLA_SCALE * LOG2E


def _mla_lat_kernel(qn_ref, qr_ref, kn_ref, kr_ref, vt_ref, knc_ref, krc_ref, vtc_ref, o_ref,
                    q2_ref, m_ref, acc_ref, sa_ref, sb_ref, *, tk):
    tq = qn_ref.shape[0]
    n_tiles = kn_ref.shape[0] // tk
    cpt = tk // KV_CHUNK
    pairs = MLA_HEADS // 2
    for p in range(pairs):
        qt = [_mla_q(qn_ref, qr_ref, 2 * p + half).astype(F32).T.astype(BF16) for half in range(2)]
        q2_ref[p] = jnp.concatenate(qt, axis=1)
        m_ref[p] = jnp.full((1, 2 * tq), NEG, F32)
        acc_ref[p] = jnp.zeros((2, VT_ROWS, tq), F32)

    def scores(p, kk):
        return _dot(kk, q2_ref[p])

    def update(p, s, vts):
        m_prev = m_ref[p]
        m_new = jnp.maximum(m_prev, s.max(axis=0, keepdims=True))
        alpha = jnp.exp2((m_prev - m_new) * _MLA_C)
        pt = jnp.exp2((s - m_new) * _MLA_C).astype(BF16)
        accs = []
        for half in range(2):
            cols = slice(half * tq, (half + 1) * tq)
            acc = acc_ref[p, half] * alpha[:, cols]
            for i, vt in enumerate(vts):
                acc = acc + _dot(vt[half * VT_ROWS:(half + 1) * VT_ROWS], pt[i * KV_CHUNK:(i + 1) * KV_CHUNK, cols])
            accs.append(acc)
        return m_new, accs

    def store(new):
        for p, (m_new, accs) in enumerate(new):
            acc_ref[p, 0] = accs[0]
            acc_ref[p, 1] = accs[1]
            m_ref[p] = m_new

    def lanes(p):
        return slice(p * LANES, (p + 1) * LANES)

    def rows(p):
        return slice(2 * p * VT_ROWS, 2 * (p + 1) * VT_ROWS)

    def keys(t, p):
        ks = pl.ds(pl.multiple_of(t * tk, tk), tk)
        return jnp.concatenate([kn_ref[ks, lanes(p)], kr_ref[ks, :]], axis=1)

    for p in range(pairs):
        sa_ref[p] = scores(p, keys(0, p))

    def tile_step(t, cur_ref, nxt_ref):
        t_next = jnp.minimum(t + 1, n_tiles - 1)
        ahead = 2
        for p in range(ahead):
            nxt_ref[p] = scores(p, keys(t_next, p))
        new = []
        for p in range(pairs):
            new.append(update(p, cur_ref[p], [vt_ref[t * cpt + i, rows(p), :] for i in range(cpt)]))
            if p + ahead < pairs:
                nxt_ref[p + ahead] = scores(p + ahead, keys(t_next, p + ahead))
        store(new)

    def body(i, c):
        tile_step(2 * i, sa_ref, sb_ref)
        tile_step(2 * i + 1, sb_ref, sa_ref)
        return c

    lax.fori_loop(0, n_tiles // 2, body, 0)

    ctx_keys = lambda p: jnp.concatenate([knc_ref[0, :, lanes(p)], krc_ref[0]], axis=1)
    new = []
    s_next = scores(0, ctx_keys(0))
    for p in range(pairs):
        s = s_next
        if p + 1 < pairs:
            s_next = scores(p + 1, ctx_keys(p + 1))
        new.append(update(p, s, [vtc_ref[0, rows(p), :]]))
    for p, (_, accs) in enumerate(new):
        o_ref[:, lanes(p)] = _finish_pair(*accs)


def _mla_lat(qn, qr, kn, kr, vt, knc, krc, vtc, *, batch, seq, tq, tk):
    n = qn.shape[0]
    steps = seq // tq
    chunks = seq // KV_CHUNK
    assert knc.shape[1] == KV_CHUNK
    qtok = lambda w: pl.BlockSpec((tq, w), lambda b, i: (b * steps + i, 0))
    ktok = lambda w: pl.BlockSpec((seq, w), lambda b, i: (b, 0))
    ctok = lambda a: pl.BlockSpec((1,) + a.shape[1:], lambda b, i: (b, 0, 0))
    pairs = MLA_HEADS // 2
    return pl.pallas_call(
        functools.partial(_mla_lat_kernel, tk=tk),
        grid=(batch, steps),
        in_specs=[qtok(512), qtok(256), ktok(512), ktok(LANES),
                  pl.BlockSpec((chunks, VT_ALL, KV_CHUNK), lambda b, i: (b, 0, 0)),
                  ctok(knc), ctok(krc), ctok(vtc)],
        out_specs=qtok(512),
        out_shape=jax.ShapeDtypeStruct((n, 512), F32),
        scratch_shapes=[pltpu.VMEM((pairs, 2 * LANES, 2 * tq), BF16),
                        pltpu.VMEM((pairs, 1, 2 * tq), F32),
                        pltpu.VMEM((pairs, 2, VT_ROWS, tq), F32),
                        pltpu.VMEM((pairs, tk, 2 * tq), F32),
                        pltpu.VMEM((pairs, tk, 2 * tq), F32)],
        compiler_params=_params(("arbitrary", "arbitrary")),
        name="mla_lat",
    )(qn, qr, kn, kr, vt, knc, krc, vtc)


def _cache_kv_kernel(ckv_ref, w_ref, wvt_ref, kn_ref, vt_ref):
    ckv = ckv_ref[0, 0].astype(BF16)
    kn_ref[0, 0] = _dot(ckv, w_ref[0]).astype(BF16)
    _store_vt(vt_ref.at[0], _dot_nt(wvt_ref[0], ckv))


def _cache_kv(ckv, wukv_k, wukv_vt):
    _, nb, rows, _ = ckv.shape
    assert rows == KV_CHUNK
    spec = lambda a, b: pl.BlockSpec((1, a, b), lambda l, j: (l, 0, 0))
    return pl.pallas_call(
        _cache_kv_kernel,
        grid=(DEPTH, nb),
        in_specs=[pl.BlockSpec((1, 1, rows, KV_LORA), lambda l, j: (l, j, 0, 0)),
                  spec(KV_LORA, 512), spec(512, KV_LORA)],
        out_specs=[pl.BlockSpec((1, 1, rows, 512), lambda l, j: (l, j, 0, 0)),
                   pl.BlockSpec((1, 1, VT_ALL, KV_CHUNK), lambda l, j: (l, j, 0, 0))],
        out_shape=[jax.ShapeDtypeStruct((DEPTH, nb, rows, 512), BF16),
                   jax.ShapeDtypeStruct((DEPTH, nb, VT_ALL, KV_CHUNK), BF16)],
        compiler_params=_params(("arbitrary", "arbitrary")),
        name="cache_kv",
    )(ckv, wukv_k, wukv_vt)


def _dft_mats(n, scale):
    k = np.arange(n)
    ang = 2.0 * np.pi * ((k[:, None] * k[None, :]) % n) / n
    return np.cos(ang) * scale, np.sin(ang) * scale


def _hi_lo_np(m):
    m = jnp.asarray(m, F32)
    hi = m.astype(BF16)
    lo = (m - hi.astype(F32)).astype(BF16)
    return hi, lo


def _chan_dft(u, wc_ref):
    uh, ul = _split(u)
    wh = wc_ref[0]
    wl = wc_ref[1]
    return _dot(uh, wh) + _dot(ul, wh) + _dot(uh, wl)


def _fn_ctx_kernel(u_ref, wc_ref, fp_ref, o_ref):
    his, los = [], []
    for g in range(FN_GROUPS):
        sl = slice(g * FN_GROUP_W, (g + 1) * FN_GROUP_W)
        a = _chan_dft(u_ref[:, sl], wc_ref)
        ar_h, ar_l = _split(a[:, :FN_GROUP_W])
        ai_h, ai_l = _split(a[:, FN_GROUP_W:])
        his.append(jnp.concatenate([ar_h, ai_h], axis=0))
        los.append(jnp.concatenate([ar_l, ai_l], axis=0))
    rh = jnp.concatenate(his, axis=1)
    rl = jnp.concatenate(los, axis=1)
    o_ref[...] = _dot(fp_ref[0], rh) + _dot(fp_ref[0], rl) + _dot(fp_ref[1], rh)


def _fn_ctx(u, *, seq):
    n = u.shape[0]
    cc, sc = _dft_mats(FN_GROUP_W, FN_GROUP_W ** -0.5)
    wc = jnp.stack(_hi_lo_np(np.concatenate([cc, -sc], axis=1)))
    cp, sp = _dft_mats(seq, seq ** -0.5)
    fp = jnp.stack(_hi_lo_np(np.concatenate([cp, sp], axis=1)))
    full = lambda a: pl.BlockSpec(a.shape, lambda b: (0,) * a.ndim)
    return pl.pallas_call(
        _fn_ctx_kernel,
        grid=(n // seq,),
        in_specs=[pl.BlockSpec((seq, FN_WIDTH), lambda b: (b, 0)), full(wc), full(fp)],
        out_specs=pl.BlockSpec((seq, FN_WIDTH), lambda b: (b, 0)),
        out_shape=jax.ShapeDtypeStruct((n, FN_WIDTH), F32),
        compiler_params=_params(("arbitrary",)),
        name="fn_ctx",
    )(u, wc, fp)


_FN_G = 8


def _fn_lat_kernel(u_ref, wc_ref, f1_ref, f2_ref, twc_ref, tws_ref, o_ref, a_ref, t_ref):
    n = GRID_W
    w = FN_GROUP_W

    def chan(i, c):
        rows = pl.ds(pl.multiple_of(i * 512, 512), 512)
        a = _chan_dft(u_ref[rows, :], wc_ref)
        a_ref[0, rows, :] = a[:, :FN_GROUP_W]
        a_ref[1, rows, :] = a[:, FN_GROUP_W:]
        return c

    lax.fori_loop(0, u_ref.shape[0] // 512, chan, 0)

    def gather(ref, base):
        cols = [jnp.concatenate([ref[0, pl.ds(base + j, n, stride=n), :],
                                 ref[1, pl.ds(base + j, n, stride=n), :]], axis=0) for j in range(_FN_G)]
        return jnp.concatenate(cols, axis=1)

    def dft(f_ref, d):
        rows = f_ref.shape[0] // 2
        dh, dl = _split(d)
        rh = _dot(f_ref[...], dh)
        return rh[:rows] + rh[rows:] + _dot(f_ref[:rows, :], dl)

    def stage1(i, c):
        base = i * _FN_G
        b = dft(f1_ref, gather(a_ref, base))
        for j in range(_FN_G):
            br = b[:n, j * w:(j + 1) * w]
            bi = b[n:, j * w:(j + 1) * w]
            rows = pl.ds(pl.multiple_of((base + j) * n, n), n)
            tc = twc_ref[rows, :]
            ts = tws_ref[rows, :]
            t_ref[0, rows, :] = br * tc + bi * ts
            t_ref[1, rows, :] = bi * tc - br * ts
        return c

    lax.fori_loop(0, n // _FN_G, stage1, 0)

    def stage2(i, c):
        base = i * _FN_G
        y = dft(f2_ref, gather(t_ref, base))
        for j in range(_FN_G):
            o_ref[pl.ds(base + j, n, stride=n), :] = y[:, j * w:(j + 1) * w]
        return c

    lax.fori_loop(0, n // _FN_G, stage2, 0)


def _fn_lat(u, *, batch, seq):
    n = GRID_W
    assert seq == n * n
    cc, sc = _dft_mats(FN_GROUP_W, FN_GROUP_W ** -0.5)
    wc = jnp.stack(_hi_lo_np(np.concatenate([cc, -sc], axis=1)))
    c1, s1 = _dft_mats(n, 1.0)
    c2, s2 = _dft_mats(n, 1.0 / n)
    f1 = jnp.concatenate(_hi_lo_np(np.block([[c1, s1], [-s1, c1]])), axis=0)
    f2 = jnp.concatenate(_hi_lo_np(np.concatenate([c2, s2], axis=1)), axis=0)
    n2 = np.arange(n)[:, None]
    k1 = np.arange(n)[None, :]
    ang = (2.0 * np.pi * ((n2 * k1) % seq) / seq).reshape(seq, 1)
    twc = jnp.asarray(np.broadcast_to(np.cos(ang), (seq, FN_GROUP_W)), F32)
    tws = jnp.asarray(np.broadcast_to(np.sin(ang), (seq, FN_GROUP_W)), F32)
    full = lambda a: pl.BlockSpec(a.shape, lambda b, g: (0,) * a.ndim)
    blk = pl.BlockSpec((seq, FN_GROUP_W), lambda b, g: (b, g))
    return pl.pallas_call(
        _fn_lat_kernel,
        grid=(batch, FN_GROUPS),
        in_specs=[blk, full(wc), full(f1), full(f2), full(twc), full(tws)],
        out_specs=blk,
        out_shape=jax.ShapeDtypeStruct(u.shape, F32),
        scratch_shapes=[pltpu.VMEM((2, seq, FN_GROUP_W), F32), pltpu.VMEM((2, seq, FN_GROUP_W), F32)],
        compiler_params=_params(("arbitrary", "arbitrary")),
        name="fn_lat",
    )(u, wc, f1, f2, twc, tws)


_E_GNA, _E_GMLA, _E_GFN, _E_MERGE = 0, 512, 1024, 1536


def _merge_kernel(*refs, final):
    (x_ref, mod_ref, ng_ref, wg_ref, ona_ref, omla_ref, ofn_ref, wona_ref, womla_ref, wofn_ref,
     wout_ref) = refs[:11]
    if final:
        fg_ref, o_ref = refs[11:13]
    else:
        o_ref = refs[11]
    x, xm = _modulated(x_ref, mod_ref, ng_ref)
    gate = mod_ref[0, 2:3, :]

    merged = None
    for i, (br_ref, wo_ref) in enumerate(((ona_ref, wona_ref), (omla_ref, womla_ref), (ofn_ref, wofn_ref))):
        g = _dot_nt(xm, wg_ref[0, i * 512:(i + 1) * 512, :])
        t = _dot((br_ref[...] * _silu(g)).astype(BF16), wo_ref[0])
        ml = _dot_nt(xm, wg_ref[0, _E_MERGE + i * D_MODEL:_E_MERGE + (i + 1) * D_MODEL, :])
        t = _sigmoid(ml) * t
        merged = t if merged is None else merged + t
    h = x + gate * _dot(merged.astype(BF16), wout_ref[0])
    if final:
        h = _rms(h, fg_ref[...])
    o_ref[...] = h


def _layer_spec(a, layer):
    return pl.BlockSpec((1,) + a.shape[1:], lambda *_: (layer,) + (0,) * (a.ndim - 1))


def _merge(x, mod, ng, w_g, ona, omla, ofn, wona, womla, wofn, wout, fg, *, tile, layer):
    n = x.shape[0]
    groups = mod.shape[0]
    steps = n // tile
    per_group = steps // groups
    final = fg is not None
    tok = lambda w: pl.BlockSpec((tile, w), lambda i: (i, 0))
    full = lambda a: pl.BlockSpec(a.shape, lambda i: (0,) * a.ndim)
    lay = lambda a: _layer_spec(a, layer)
    in_specs = [tok(D_MODEL), pl.BlockSpec((1, 3, D_MODEL), lambda i: (i // per_group, 0, 0)),
                full(ng), lay(w_g), tok(512), tok(512), tok(512),
                lay(wona), lay(womla), lay(wofn), lay(wout)]
    args = [x, mod, ng, w_g, ona, omla, ofn, wona, womla, wofn, wout]
    if final:
        in_specs.append(full(fg))
        args.append(fg)
    return pl.pallas_call(
        functools.partial(_merge_kernel, final=final),
        grid=(steps,),
        in_specs=in_specs,
        out_specs=tok(D_MODEL),
        out_shape=jax.ShapeDtypeStruct((n, D_MODEL), F32),
        compiler_params=_params(("arbitrary",)),
        name="merge",
    )(*args)


def _rope_rot(w, axis):
    q = MLA_ROPE // 4
    part = lambda i: lax.slice_in_dim(w, i * q, (i + 1) * q, axis=axis)
    return jnp.concatenate([-part(1), part(0), -part(3), part(2)], axis=axis)


def _prep_w_in(w_in):
    wt = jnp.swapaxes(w_in, 1, 2)
    rows = lambda start, n: wt[:, start:start + n]
    kr = rows(_O_KROPE, MLA_ROPE)
    kr_rot = _rope_rot(kr, 1)
    a = jnp.concatenate([rows(_O_QKV, 1536), rows(_O_QLAT, Q_LORA + KV_LORA),
                         jnp.tile(kr, (1, 4, 1)), jnp.tile(kr_rot, (1, 4, 1)), rows(_O_UFN, FN_WIDTH)], axis=1)
    e = jnp.concatenate([rows(_O_GATE_NA, 512), rows(_O_GATE_MLA, 512), rows(_O_GATE_FN, 512),
                         rows(_O_MERGE, 3 * D_MODEL)], axis=1)
    return a.astype(BF16), e.astype(BF16)


def _prep_w_uq(w):
    w3 = w.reshape(Q_LORA, MLA_HEADS, MLA_NOPE + MLA_ROPE)
    nope = w3[:, :, :MLA_NOPE].reshape(Q_LORA, MLA_HEADS * MLA_NOPE)
    rope = w3[:, :, MLA_NOPE:]
    rot = _rope_rot(rope, 2)
    return jnp.concatenate([nope, rope.reshape(Q_LORA, -1), rot.reshape(Q_LORA, -1)], axis=1).astype(BF16)


def _prep_w_ukv(w):
    w3 = w.reshape(KV_LORA, MLA_HEADS, MLA_NOPE + MLA_V)
    return jnp.concatenate([w3[:, :, :MLA_NOPE].reshape(KV_LORA, -1),
                            w3[:, :, MLA_NOPE:].reshape(KV_LORA, -1)], axis=1).astype(BF16)


def _rope_tables(n):
    t = jnp.arange(n, dtype=jnp.int32)
    row = (t // GRID_W).astype(F32)
    col = (t % GRID_W).astype(F32)
    half = MLA_ROPE // 2
    inv_freq = ROPE_THETA ** (-jnp.arange(0, half, 2, dtype=F32) / half)
    ar = row[:, None] * inv_freq[None, :]
    ac = col[:, None] * inv_freq[None, :]
    ang = jnp.concatenate([ar, ar, ac, ac], axis=-1)
    return jnp.tile(jnp.cos(ang), (1, MLA_HEADS)), jnp.tile(jnp.sin(ang), (1, MLA_HEADS))


def kernel(x_prompt, x_sample, cache_na_k, cache_na_v, cache_mla_ckv, cache_mla_krope, c, c_ctx,
           w_ada, b_ada, norm_g, w_in, q_norm_g, kv_norm_g, w_uq, w_ukv, na_bias,
           w_o_na, w_o_mla, w_o_fourier, w_out, final_norm_g):
    batch, seq, _ = x_prompt.shape
    dbatch, dseq, _ = x_sample.shape
    past = cache_na_k.shape[2]

    cond = jnp.zeros((8, D_MODEL), F32).at[0].set(c_ctx).at[1:1 + dbatch].set(c)
    mods = _ada_mods(cond, w_ada, b_ada).reshape(DEPTH, 8, 3, D_MODEL)

    wukv_all = jnp.stack([_prep_w_ukv(w_ukv[l]) for l in range(DEPTH)])
    wukv_vt_all = wukv_all[:, :, 512:].transpose(0, 2, 1)
    knc_all, mvtc_all = _cache_kv(cache_mla_ckv.transpose(1, 0, 2, 3), wukv_all[:, :, :512], wukv_vt_all)
    krc_all = jnp.tile(cache_mla_krope, (1, 1, 1, 4)).astype(BF16)
    nakc_all = cache_na_k.reshape(dbatch, DEPTH, past, NA_WIDTH).astype(BF16)
    navt = cache_na_v.transpose(0, 1, 3, 4, 2).astype(BF16)
    navtc_all = jnp.concatenate([navt, jnp.ones((dbatch, DEPTH, NA_HEADS, ONES_ROWS, past), BF16)],
                                axis=3).reshape(dbatch, DEPTH, VT_ALL, past)
    cos, sin = _rope_tables(dseq)

    h_ctx = x_prompt.reshape(batch * seq, D_MODEL)
    h_lat = x_sample.reshape(dbatch * dseq, D_MODEL)
    ks, vs, ckvs, krs = [], [], [], []
    row = lambda a: a.reshape(1, -1)
    w_a, w_e = _prep_w_in(w_in)
    wona, womla, wofn, wout = (w_o_na.astype(BF16), w_o_mla.astype(BF16),
                               w_o_fourier.astype(BF16), w_out.astype(BF16))
    for l in range(DEPTH):
        wuq = _prep_w_uq(w_uq[l])
        wukv = wukv_all[l]
        ng, qng, kvng = row(norm_g[l]), row(q_norm_g[l]), row(kv_norm_g[l])
        fg = row(final_norm_g) if l == DEPTH - 1 else None
        mod_ctx = mods[l, 0:1]
        mod_lat = mods[l, 1:1 + dbatch]

        (q, k, v, qn, qr, kn, kr, mv, u, kf, vf, ckvf, krf) = _inproj(
            h_ctx, mod_ctx, ng, w_a, qng, kvng, wuq, wukv, None, tile=512, layer=l, seq=seq)
        ks.append(kf)
        vs.append(vf)
        ckvs.append(ckvf)
        krs.append(krf)
        ona, omla = _ctx_attn(q, k, v, qn, qr, kn, kr, mv, seq=seq)
        ofn = _fn_ctx(u, seq=seq)
        h_ctx = _merge(h_ctx, mod_ctx, ng, w_e, ona, omla, ofn, wona, womla, wofn, wout, fg, tile=512, layer=l)

        (q, k, vt, qn, qr, kn, kr, mvt, u) = _inproj(
            h_lat, mod_lat, ng, w_a, qng, kvng, wuq, wukv, (wukv_vt_all[l], cos, sin), tile=512, layer=l)
        ona = _na_lat(q, k, vt, nakc_all[:, l], navtc_all[:, l], na_bias[l], batch=dbatch, seq=dseq)
        omla = _mla_lat(qn, qr, kn, kr, mvt, knc_all[l], krc_all[:, l], mvtc_all[l],
                        batch=dbatch, seq=dseq, tq=256, tk=512)
        ofn = _fn_lat(u, batch=dbatch, seq=dseq)
        h_lat = _merge(h_lat, mod_lat, ng, w_e, ona, omla, ofn, wona, womla, wofn, wout, fg, tile=512, layer=l)

    y_prompt = h_ctx.reshape(batch, seq, D_MODEL)
    y_sample = h_lat.reshape(dbatch, dseq, D_MODEL)
    heads_t = lambda xs: jnp.stack(xs, axis=1).reshape(
        batch, DEPTH, NA_HEADS, NA_HEAD_DIM, seq).transpose(0, 1, 4, 2, 3)
    new_na_k = heads_t(ks)
    new_na_v = heads_t(vs)
    new_mla_ckv = jnp.stack([a.reshape(batch, seq, KV_LORA) for a in ckvs], axis=1)
    new_mla_krope = jnp.stack(krs, axis=1).transpose(0, 1, 3, 2)
    return (y_prompt, y_sample, new_na_k, new_na_v, new_mla_ckv, new_mla_krope)
```

```python
import functools

import jax
import jax.numpy as jnp
import numpy as np
from jax import lax
from jax.experimental import pallas as pl
from jax.experimental.pallas import tpu as pltpu

F32 = jnp.float32
BF16 = jnp.bfloat16

D_MODEL = 1024
DEPTH = 4
GRID_W = 64
NA_HEADS = 8
NA_HEAD_DIM = 64
NA_WIDTH = 512
NA_ROWS = 8
NA_COLS = 16
MLA_HEADS = 8
MLA_NOPE = 64
MLA_ROPE = 32
MLA_V = 64
MLA_WIDTH = 512
Q_LORA = 256
KV_LORA = 128
MLA_SCALE = (MLA_NOPE + MLA_ROPE) ** -0.5
ROPE_THETA = 10000.0
FN_GROUPS = 4
FN_GROUP_W = 128
FN_WIDTH = 512
EPS = 1e-6
NEG = -1e30

_O_QKV = 0
_O_GATE_NA = 1536
_O_QLAT = 2048
_O_CKV = 2304
_O_KROPE = 2432
_O_GATE_MLA = 2464
_O_UFN = 2976
_O_GATE_FN = 3488
_O_MERGE = 4000
D_IN = 7072

LANES = 128
VMEM_LIMIT = 56 * 1024 * 1024


def _params(sem):
    return pltpu.CompilerParams(dimension_semantics=sem, vmem_limit_bytes=VMEM_LIMIT)


def _sigmoid(x):
    return 1.0 / (1.0 + jnp.exp(-x))


def _silu(x):
    return x * _sigmoid(x)


def _rms(x, g):
    return x * lax.rsqrt(jnp.mean(x * x, axis=-1, keepdims=True) + EPS) * g


def _dot(a, b):
    return jnp.dot(a, b, preferred_element_type=F32)


def _dot_nt(a, b):
    return lax.dot_general(a, b, (((1,), (1,)), ((), ())), preferred_element_type=F32)


def _split(x):
    hi = x.astype(BF16)
    lo = (x - hi.astype(F32)).astype(BF16)
    return hi, lo


def _ada_kernel(cond_ref, w_ref, b_ref, o_ref):
    a = _silu(cond_ref[...]).astype(BF16)
    o_ref[0] = _dot(a, w_ref[0].astype(BF16)) + b_ref[0]


def _ada_mods(cond, w_ada, b_ada):
    nb = 3
    return pl.pallas_call(
        _ada_kernel,
        grid=(DEPTH, nb),
        in_specs=[
            pl.BlockSpec((8, D_MODEL), lambda l, j: (0, 0)),
            pl.BlockSpec((1, D_MODEL, D_MODEL), lambda l, j: (l, 0, j)),
            pl.BlockSpec((1, 1, D_MODEL), lambda l, j: (l, 0, j)),
        ],
        out_specs=pl.BlockSpec((1, 8, D_MODEL), lambda l, j: (l, 0, j)),
        out_shape=jax.ShapeDtypeStruct((DEPTH, 8, 3 * D_MODEL), F32),
        compiler_params=_params(("arbitrary", "arbitrary")),
        name="ada_mods",
    )(cond, w_ada, b_ada.reshape(DEPTH, 1, 3 * D_MODEL))


_A_Q, _A_K, _A_V, _A_QLAT, _A_CKV, _A_KR, _A_KRROT, _A_U, _A_END = (
    0, 512, 1024, 1536, 1792, 1920, 2048, 2176, 2688)


def _modulated(x_ref, mod_ref, ng_ref):
    x = x_ref[...]
    shift = mod_ref[0, 0:1, :]
    scale = mod_ref[0, 1:2, :]
    return x, (_rms(x, ng_ref[...]) * (1.0 + scale) + shift).astype(BF16)


KV_CHUNK = 256
HEAD_DIM = 64
ONES_ROWS = 16
VT_ROWS = HEAD_DIM + ONES_ROWS
VT_ALL = 8 * VT_ROWS


def _store_vt(vt_ref, vt):
    for c in range(vt_ref.shape[0]):
        cols = slice(c * KV_CHUNK, (c + 1) * KV_CHUNK)
        for h in range(8):
            vt_ref[c, h * VT_ROWS:h * VT_ROWS + HEAD_DIM, :] = vt[h * HEAD_DIM:(h + 1) * HEAD_DIM, cols].astype(BF16)
            vt_ref[c, h * VT_ROWS + HEAD_DIM:(h + 1) * VT_ROWS, :] = jnp.ones((ONES_ROWS, KV_CHUNK), BF16)


def _store_seq_t(ref, xt):
    seq = ref.shape[2]
    for j in range(ref.shape[0]):
        ref[j] = xt[:, j * seq:(j + 1) * seq]


def _inproj_kernel(*refs, latent):
    (x_ref, mod_ref, ng_ref, w_ref, qng_ref, kvng_ref, wuq_ref, wukv_ref) = refs[:8]
    if latent:
        (wmvt_ref, cos_ref, sin_ref,
         naq_ref, nak_ref, navt_ref, qn_ref, qr_ref, kn_ref, kr_ref, mvt_ref, u_ref) = refs[8:]
    else:
        (naq_ref, nak_ref, nav_ref, qn_ref, qr_ref, kn_ref, kr_ref, mv_ref, u_ref,
         kf_ref, vf_ref, ckvf_ref, krf_ref) = refs[8:]

    _, xm = _modulated(x_ref, mod_ref, ng_ref)

    def proj(a, b):
        return _dot_nt(xm, w_ref[0, a:b, :])

    q = proj(_A_Q, _A_K)
    naq_ref[...] = (q * (NA_HEAD_DIM ** -0.5)).astype(BF16)
    k = proj(_A_K, _A_V)
    nak_ref[...] = k.astype(BF16)
    if latent:
        _store_vt(navt_ref, _dot_nt(w_ref[0, _A_V:_A_QLAT, :], xm))
    else:
        v = proj(_A_V, _A_QLAT)
        nav_ref[...] = v.astype(BF16)
        _store_seq_t(kf_ref, k.T)
        _store_seq_t(vf_ref, v.T)

    qlat = _rms(proj(_A_QLAT, _A_CKV), qng_ref[...]).astype(BF16)
    qq = _dot(qlat, wuq_ref[...])
    qn_ref[...] = qq[:, :512].astype(BF16)
    qr = qq[:, 512:768]
    if latent:
        qr = qr * cos_ref[...] + qq[:, 768:1024] * sin_ref[...]
    qr_ref[...] = qr.astype(BF16)

    ckv = _rms(proj(_A_CKV, _A_KR), kvng_ref[...])
    ckv_b = ckv.astype(BF16)
    kn_ref[...] = _dot(ckv_b, wukv_ref[:, :512]).astype(BF16)
    if latent:
        _store_vt(mvt_ref, _dot_nt(wmvt_ref[...], ckv_b))
    else:
        ckvf_ref[...] = ckv
        mv_ref[...] = _dot(ckv_b, wukv_ref[:, 512:]).astype(BF16)

    kr = proj(_A_KR, _A_KRROT)
    if latent:
        kr = kr * cos_ref[:, :LANES] + proj(_A_KRROT, _A_U) * sin_ref[:, :LANES]
    else:
        _store_seq_t(krf_ref, kr.T[:MLA_ROPE])
    kr_ref[...] = kr.astype(BF16)

    u_ref[...] = proj(_A_U, _A_END)


def _inproj(x, mod, ng, w_a, qng, kvng, wuq, wukv, lat_extra, *, tile, layer, seq=None):
    n = x.shape[0]
    groups = mod.shape[0]
    steps = n // tile
    per_group = steps // groups
    latent = lat_extra is not None
    tok = lambda w: pl.BlockSpec((tile, w), lambda i: (i, 0))
    full = lambda a: pl.BlockSpec(a.shape, lambda i: (0,) * a.ndim)
    in_specs = [tok(D_MODEL),
                pl.BlockSpec((1, 3, D_MODEL), lambda i: (i // per_group, 0, 0)),
                full(ng), _layer_spec(w_a, layer), full(qng), full(kvng), full(wuq), full(wukv)]
    args = [x, mod, ng, w_a, qng, kvng, wuq, wukv]
    out = lambda w, dt: (jax.ShapeDtypeStruct((n, w), dt), tok(w))
    if latent:
        wmvt, cos, sin = lat_extra
        steps_per_seq = cos.shape[0] // tile
        in_specs += [full(wmvt)] + [pl.BlockSpec((tile, 256), lambda i: (i % steps_per_seq, 0))] * 2
        args += [wmvt, cos, sin]
        cpt = tile // KV_CHUNK
        vt = (jax.ShapeDtypeStruct((n // KV_CHUNK, VT_ALL, KV_CHUNK), BF16),
              pl.BlockSpec((cpt, VT_ALL, KV_CHUNK), lambda i: (i, 0, 0)))
        outs = [out(512, BF16), out(512, BF16), vt, out(512, BF16), out(256, BF16), out(512, BF16),
                out(LANES, BF16), vt, out(512, F32)]
    else:
        out_t = lambda w: (jax.ShapeDtypeStruct((n // seq, w, seq), F32),
                           pl.BlockSpec((tile // seq, w, seq), lambda i: (i, 0, 0)))
        outs = [out(512, BF16)] * 3 + [out(512, BF16), out(256, BF16), out(512, BF16), out(LANES, BF16),
                                      out(512, BF16), out(512, F32),
                                      out_t(512), out_t(512), out(KV_LORA, F32), out_t(MLA_ROPE)]
    return pl.pallas_call(
        functools.partial(_inproj_kernel, latent=latent),
        grid=(steps,),
        in_specs=in_specs,
        out_specs=[o[1] for o in outs],
        out_shape=[o[0] for o in outs],
        compiler_params=_params(("arbitrary",)),
        name="inproj_lat" if latent else "inproj_ctx",
    )(*args)


def _lane_iota(n=LANES):
    return lax.broadcasted_iota(jnp.int32, (1, n), 1)


def _head_q(q_pair, half):
    lane = _lane_iota()
    keep = (lane < NA_HEAD_DIM) if half == 0 else (lane >= NA_HEAD_DIM)
    return jnp.where(keep, q_pair, jnp.zeros_like(q_pair))


def _mla_q(qn_ref, qr_ref, h):
    p, half = divmod(h, 2)
    qa = _head_q(qn_ref[:, p * LANES:(p + 1) * LANES], half)
    g, slot = divmod(h, 4)
    qb = qr_ref[:, g * LANES:(g + 1) * LANES]
    lane = _lane_iota()
    keep = (lane >= slot * MLA_ROPE) & (lane < (slot + 1) * MLA_ROPE)
    qb = jnp.where(keep, qb, jnp.zeros_like(qb))
    return jnp.concatenate([qa, qb], axis=1)


def _pair_out(o0, o1):
    return jnp.where(_lane_iota() < NA_HEAD_DIM, o0, o1)


def _softmax_pv(s_list, v_list):
    m = s_list[0].max(axis=-1, keepdims=True)
    for s in s_list[1:]:
        m = jnp.maximum(m, s.max(axis=-1, keepdims=True))
    l = None
    o = None
    for s, v in zip(s_list, v_list):
        e = jnp.exp(s - m)
        ls = e.sum(axis=-1, keepdims=True)
        pv = _dot(e.astype(BF16), v)
        l = ls if l is None else l + ls
        o = pv if o is None else o + pv
    return o / l


def _ctx_attn_kernel(q_ref, k_ref, v_ref, qn_ref, qr_ref, kn_ref, kr_ref, mv_ref, ona_ref, omla_ref):
    for p in range(NA_HEADS // 2):
        sl = slice(p * LANES, (p + 1) * LANES)
        kp = k_ref[:, sl]
        vp = v_ref[:, sl]
        outs = []
        for half in range(2):
            s = _dot_nt(_head_q(q_ref[:, sl], half), kp)
            outs.append(_softmax_pv([s], [vp]))
        ona_ref[:, sl] = _pair_out(*outs)

        kk = jnp.concatenate([kn_ref[:, sl], kr_ref[...]], axis=1)
        mvp = mv_ref[:, sl]
        outs = []
        for half in range(2):
            s = _dot_nt(_mla_q(qn_ref, qr_ref, 2 * p + half), kk) * MLA_SCALE
            outs.append(_softmax_pv([s], [mvp]))
        omla_ref[:, sl] = _pair_out(*outs)


def _ctx_attn(q, k, v, qn, qr, kn, kr, mv, *, seq):
    n = q.shape[0]
    tok = lambda w: pl.BlockSpec((seq, w), lambda b: (b, 0))
    return pl.pallas_call(
        _ctx_attn_kernel,
        grid=(n // seq,),
        in_specs=[tok(512), tok(512), tok(512), tok(512), tok(256), tok(512), tok(LANES), tok(512)],
        out_specs=[tok(512), tok(512)],
        out_shape=[jax.ShapeDtypeStruct((n, 512), F32)] * 2,
        compiler_params=_params(("arbitrary",)),
        name="ctx_attn",
    )(q, k, v, qn, qr, kn, kr, mv)


NA_QROWS = 4
NA_KROWS = 12
_NDR = 2 * NA_ROWS - 1
_NDC = 2 * NA_COLS - 1
_Z_NONE = 48
LOG2E = 1.4426950408889634


def _na_tile_descriptors(rows):
    last_q0 = rows - NA_QROWS
    last_k0 = rows - NA_KROWS
    desc = np.full((3, NA_KROWS, NA_QROWS // 2), _Z_NONE, np.int32)
    for case, (q0, k0) in enumerate(((0, 0), (NA_QROWS, 0), (last_q0, last_k0))):
        for kri in range(NA_KROWS):
            for u in range(NA_QROWS // 2):
                info = []
                for ri in (2 * u, 2 * u + 1):
                    qrow, krow = q0 + ri, k0 + kri
                    start = min(max(qrow - NA_ROWS // 2, 0), rows - NA_ROWS)
                    info.append((start <= krow < start + NA_ROWS, krow - qrow))
                (vl, drl), (vr, _) = info
                d = drl + NA_ROWS - 1
                if vl and vr:
                    desc[case, kri, u] = d
                elif vl:
                    desc[case, kri, u] = 16 + d
                elif vr:
                    desc[case, kri, u] = 32 + d
    return desc.reshape(-1)


def _na_bias_consts():
    col = np.arange(GRID_W)
    col_start = np.clip(col - NA_COLS // 2, 0, GRID_W - NA_COLS)
    kc, qc = col[:, None], col[None, :]
    col_in = (kc >= col_start[None, :]) & (kc < col_start[None, :] + NA_COLS)
    dc = kc - qc + NA_COLS - 1
    d = np.stack([((dc == j) & col_in) for j in range(_NDC)]).astype(np.float32)
    negm = np.where(col_in, 0.0, NEG).astype(np.float32)
    return np.concatenate([d, d], axis=2), np.concatenate([negm, negm], axis=1)


def _na_build_bias(bias_ref, d2_ref, negm_ref, z_ref):
    left = _lane_iota() < GRID_W

    def coef(h, dr, j):
        if dr < -(NA_ROWS - 1) or dr > NA_ROWS - 1:
            return 0.0
        return bias_ref[(h * _NDR + dr + NA_ROWS - 1) * _NDC + j]

    def per_head(h, c):
        for d in range(16):
            dr_l = d - (NA_ROWS - 1)
            acc = negm_ref[...]
            for j in range(_NDC):
                acc = acc + jnp.where(left, coef(h, dr_l, j), coef(h, dr_l - 1, j)) * d2_ref[j]
            z_ref[h, d] = acc
            z_ref[h, 16 + d] = jnp.where(left, acc, NEG)
            z_ref[h, 32 + d] = jnp.where(left, NEG, acc)
        z_ref[h, _Z_NONE] = jnp.full((GRID_W, LANES), NEG, F32)
        return c

    lax.fori_loop(0, NA_HEADS, per_head, 0)


def _finish_pair(acc_a, acc_b):
    norm = lambda acc: acc[:HEAD_DIM] * (1.0 / acc[HEAD_DIM:HEAD_DIM + 1])
    return jnp.concatenate([norm(acc_a), norm(acc_b)], axis=0).T


def _na_lat_kernel(bias_ref, desc_ref, q_ref, k_ref, vt_ref, kc_ref, vtc_ref, d2_ref, negm_ref, o_ref, z_ref,
                   *, rows):
    g = pl.program_id(1)
    groups = rows // NA_QROWS

    @pl.when((pl.program_id(0) == 0) & (g == 0))
    def _():
        _na_build_bias(bias_ref, d2_ref, negm_ref, z_ref)

    tq = NA_QROWS * GRID_W
    k_row0 = jnp.clip(g * NA_QROWS - NA_ROWS // 2, 0, rows - NA_KROWS)
    start = pl.multiple_of(k_row0 * GRID_W, KV_CHUNK)
    chunk0 = k_row0 // (KV_CHUNK // GRID_W)
    n_loc = NA_KROWS * GRID_W
    case = jnp.where(g == 0, 0, jnp.where(g == groups - 1, 2, 1))
    upairs = NA_QROWS // 2

    def scores(h):
        sl = slice((h // 2) * LANES, (h // 2 + 1) * LANES)
        qt = _head_q(q_ref[:, sl], h % 2).astype(F32).T.astype(BF16)
        return _dot(k_ref[pl.ds(start, n_loc), sl], qt), _dot(kc_ref[0, :, sl], qt)

    def attend(h, s_loc, s_ctx):
        rows_h = slice(h * VT_ROWS, (h + 1) * VT_ROWS)
        blocks = []
        for kri in range(NA_KROWS):
            tiles = []
            for u in range(upairs):
                idx = desc_ref[(case * NA_KROWS + kri) * upairs + u]
                tiles.append(s_loc[kri * GRID_W:(kri + 1) * GRID_W, u * LANES:(u + 1) * LANES] + z_ref[h, idx])
            blocks.append(jnp.concatenate(tiles, axis=1))
        s_loc = jnp.concatenate(blocks, axis=0)
        m = jnp.maximum(s_loc.max(axis=0, keepdims=True), s_ctx.max(axis=0, keepdims=True))
        p_loc = jnp.exp2((s_loc - m) * LOG2E).astype(BF16)
        p_ctx = jnp.exp2((s_ctx - m) * LOG2E).astype(BF16)
        acc = _dot(vtc_ref[0, rows_h, :], p_ctx)
        for i in range(n_loc // KV_CHUNK):
            acc = acc + _dot(vt_ref[chunk0 + i, rows_h, :], p_loc[i * KV_CHUNK:(i + 1) * KV_CHUNK])
        return acc

    s_next = scores(0)
    accs = []
    for h in range(NA_HEADS):
        s = s_next
        if h + 1 < NA_HEADS:
            s_next = scores(h + 1)
        accs.append(attend(h, *s))
        if h % 2:
            o_ref[:, (h // 2) * LANES:(h // 2 + 1) * LANES] = _finish_pair(accs[h - 1], accs[h])


def _na_lat(q, k, vt, kc, vtc, bias_l, *, batch, seq):
    rows = seq // GRID_W
    n = q.shape[0]
    groups = rows // NA_QROWS
    tq = NA_QROWS * GRID_W
    chunks = seq // KV_CHUNK
    d2, negm = _na_bias_consts()
    desc = jnp.asarray(_na_tile_descriptors(rows))
    smem = pl.BlockSpec(memory_space=pltpu.SMEM)
    full = lambda a: pl.BlockSpec(a.shape, lambda b, g: (0,) * a.ndim)
    return pl.pallas_call(
        functools.partial(_na_lat_kernel, rows=rows),
        grid=(batch, groups),
        in_specs=[
            smem, smem,
            pl.BlockSpec((tq, 512), lambda b, g: (b * groups + g, 0)),
            pl.BlockSpec((seq, 512), lambda b, g: (b, 0)),
            pl.BlockSpec((chunks, VT_ALL, KV_CHUNK), lambda b, g: (b, 0, 0)),
            pl.BlockSpec((1,) + kc.shape[1:], lambda b, g: (b, 0, 0)),
            pl.BlockSpec((1,) + vtc.shape[1:], lambda b, g: (b, 0, 0)),
            full(d2), full(negm),
        ],
        out_specs=pl.BlockSpec((tq, 512), lambda b, g: (b * groups + g, 0)),
        out_shape=jax.ShapeDtypeStruct((n, 512), F32),
        scratch_shapes=[pltpu.VMEM((NA_HEADS, _Z_NONE + 1, GRID_W, LANES), F32)],
        compiler_params=_params(("arbitrary", "arbitrary")),
        name="na_lat",
    )(bias_l.reshape(-1), desc, q, k, vt, kc, vtc, jnp.asarray(d2), jnp.asarray(negm))


_MLA_C = MLA_SCALE * LOG2E


def _mla_lat_kernel(qn_ref, qr_ref, kn_ref, kr_ref, vt_ref, knc_ref, krc_ref, vtc_ref, o_ref,
                    q2_ref, m_ref, acc_ref, sa_ref, sb_ref, *, tk):
    tq = qn_ref.shape[0]
    n_tiles = kn_ref.shape[0] // tk
    cpt = tk // KV_CHUNK
    pairs = MLA_HEADS // 2
    for p in range(pairs):
        qt = [_mla_q(qn_ref, qr_ref, 2 * p + half).astype(F32).T.astype(BF16) for half in range(2)]
        q2_ref[p] = jnp.concatenate(qt, axis=1)
        m_ref[p] = jnp.full((1, 2 * tq), NEG, F32)
        acc_ref[p] = jnp.zeros((2, VT_ROWS, tq), F32)

    def scores(p, kk):
        return _dot(kk, q2_ref[p])

    def update(p, s, vts):
        m_prev = m_ref[p]
        m_new = jnp.maximum(m_prev, s.max(axis=0, keepdims=True))
        alpha = jnp.exp2((m_prev - m_new) * _MLA_C)
        pt = jnp.exp2((s - m_new) * _MLA_C).astype(BF16)
        accs = []
        for half in range(2):
            cols = slice(half * tq, (half + 1) * tq)
            acc = acc_ref[p, half] * alpha[:, cols]
            for i, vt in enumerate(vts):
                acc = acc + _dot(vt[half * VT_ROWS:(half + 1) * VT_ROWS], pt[i * KV_CHUNK:(i + 1) * KV_CHUNK, cols])
            accs.append(acc)
        return m_new, accs

    def store(new):
        for p, (m_new, accs) in enumerate(new):
            acc_ref[p, 0] = accs[0]
            acc_ref[p, 1] = accs[1]
            m_ref[p] = m_new

    def lanes(p):
        return slice(p * LANES, (p + 1) * LANES)

    def rows(p):
        return slice(2 * p * VT_ROWS, 2 * (p + 1) * VT_ROWS)

    def keys(t, p):
        start = t * tk if isinstance(t, int) else pl.multiple_of(t * tk, tk)
        ks = pl.ds(start, tk)
        return jnp.concatenate([kn_ref[ks, lanes(p)], kr_ref[ks, :]], axis=1)

    for p in range(pairs):
        sa_ref[p] = scores(p, keys(0, p))

    n_ctx = knc_ref.shape[1]

    def latent_next(t):
        def put(nxt_ref, p):
            nxt_ref[p] = scores(p, keys(t + 1, p))
        return put

    def ctx_next(nxt_ref, p):
        kk = jnp.concatenate([knc_ref[0, :, lanes(p)], krc_ref[0]], axis=1)
        nxt_ref[p, :n_ctx] = scores(p, kk)

    def tile_step(t, cur_ref, nxt_ref, put_next):
        ahead = 2
        for p in range(ahead):
            put_next(nxt_ref, p)
        new = []
        for p in range(pairs):
            new.append(update(p, cur_ref[p], [vt_ref[t * cpt + i, rows(p), :] for i in range(cpt)]))
            if p + ahead < pairs:
                put_next(nxt_ref, p + ahead)
        store(new)

    def body(i, c):
        tile_step(2 * i, sa_ref, sb_ref, latent_next(2 * i))
        tile_step(2 * i + 1, sb_ref, sa_ref, latent_next(2 * i + 1))
        return c

    lax.fori_loop(0, n_tiles // 2 - 1, body, 0)
    tile_step(n_tiles - 2, sa_ref, sb_ref, latent_next(n_tiles - 2))
    tile_step(n_tiles - 1, sb_ref, sa_ref, ctx_next)
    for p in range(pairs):
        _, accs = update(p, sa_ref[p, :n_ctx], [vtc_ref[0, rows(p), :]])
        o_ref[:, lanes(p)] = _finish_pair(*accs)


def _mla_lat(qn, qr, kn, kr, vt, knc, krc, vtc, *, batch, seq, tq, tk):
    n = qn.shape[0]
    steps = seq // tq
    chunks = seq // KV_CHUNK
    assert knc.shape[1] == KV_CHUNK
    qtok = lambda w: pl.BlockSpec((tq, w), lambda b, i: (b * steps + i, 0))
    ktok = lambda w: pl.BlockSpec((seq, w), lambda b, i: (b, 0))
    ctok = lambda a: pl.BlockSpec((1,) + a.shape[1:], lambda b, i: (b, 0, 0))
    pairs = MLA_HEADS // 2
    return pl.pallas_call(
        functools.partial(_mla_lat_kernel, tk=tk),
        grid=(batch, steps),
        in_specs=[qtok(512), qtok(256), ktok(512), ktok(LANES),
                  pl.BlockSpec((chunks, VT_ALL, KV_CHUNK), lambda b, i: (b, 0, 0)),
                  ctok(knc), ctok(krc), ctok(vtc)],
        out_specs=qtok(512),
        out_shape=jax.ShapeDtypeStruct((n, 512), F32),
        scratch_shapes=[pltpu.VMEM((pairs, 2 * LANES, 2 * tq), BF16),
                        pltpu.VMEM((pairs, 1, 2 * tq), F32),
                        pltpu.VMEM((pairs, 2, VT_ROWS, tq), F32),
                        pltpu.VMEM((pairs, tk, 2 * tq), F32),
                        pltpu.VMEM((pairs, tk, 2 * tq), F32)],
        compiler_params=_params(("arbitrary", "arbitrary")),
        name="mla_lat",
    )(qn, qr, kn, kr, vt, knc, krc, vtc)


def _cache_kv_kernel(ckv_ref, w_ref, wvt_ref, kn_ref, vt_ref):
    ckv = ckv_ref[0, 0].astype(BF16)
    kn_ref[0, 0] = _dot(ckv, w_ref[0]).astype(BF16)
    _store_vt(vt_ref.at[0], _dot_nt(wvt_ref[0], ckv))


def _cache_kv(ckv, wukv_k, wukv_vt):
    _, nb, rows, _ = ckv.shape
    assert rows == KV_CHUNK
    spec = lambda a, b: pl.BlockSpec((1, a, b), lambda l, j: (l, 0, 0))
    return pl.pallas_call(
        _cache_kv_kernel,
        grid=(DEPTH, nb),
        in_specs=[pl.BlockSpec((1, 1, rows, KV_LORA), lambda l, j: (l, j, 0, 0)),
                  spec(KV_LORA, 512), spec(512, KV_LORA)],
        out_specs=[pl.BlockSpec((1, 1, rows, 512), lambda l, j: (l, j, 0, 0)),
                   pl.BlockSpec((1, 1, VT_ALL, KV_CHUNK), lambda l, j: (l, j, 0, 0))],
        out_shape=[jax.ShapeDtypeStruct((DEPTH, nb, rows, 512), BF16),
                   jax.ShapeDtypeStruct((DEPTH, nb, VT_ALL, KV_CHUNK), BF16)],
        compiler_params=_params(("arbitrary", "arbitrary")),
        name="cache_kv",
    )(ckv, wukv_k, wukv_vt)


def _dft_mats(n, scale):
    k = np.arange(n)
    ang = 2.0 * np.pi * ((k[:, None] * k[None, :]) % n) / n
    return np.cos(ang) * scale, np.sin(ang) * scale


def _hi_lo_np(m):
    m = jnp.asarray(m, F32)
    hi = m.astype(BF16)
    lo = (m - hi.astype(F32)).astype(BF16)
    return hi, lo


def _chan_dft(u, wc_ref):
    uh, ul = _split(u)
    wh = wc_ref[0]
    wl = wc_ref[1]
    return _dot(uh, wh) + _dot(ul, wh) + _dot(uh, wl)


def _fn_ctx_kernel(u_ref, wc_ref, fp_ref, o_ref):
    his, los = [], []
    for g in range(FN_GROUPS):
        sl = slice(g * FN_GROUP_W, (g + 1) * FN_GROUP_W)
        a = _chan_dft(u_ref[:, sl], wc_ref)
        ar_h, ar_l = _split(a[:, :FN_GROUP_W])
        ai_h, ai_l = _split(a[:, FN_GROUP_W:])
        his.append(jnp.concatenate([ar_h, ai_h], axis=0))
        los.append(jnp.concatenate([ar_l, ai_l], axis=0))
    rh = jnp.concatenate(his, axis=1)
    rl = jnp.concatenate(los, axis=1)
    o_ref[...] = _dot(fp_ref[0], rh) + _dot(fp_ref[0], rl) + _dot(fp_ref[1], rh)


def _fn_ctx(u, *, seq):
    n = u.shape[0]
    cc, sc = _dft_mats(FN_GROUP_W, FN_GROUP_W ** -0.5)
    wc = jnp.stack(_hi_lo_np(np.concatenate([cc, -sc], axis=1)))
    cp, sp = _dft_mats(seq, seq ** -0.5)
    fp = jnp.stack(_hi_lo_np(np.concatenate([cp, sp], axis=1)))
    full = lambda a: pl.BlockSpec(a.shape, lambda b: (0,) * a.ndim)
    return pl.pallas_call(
        _fn_ctx_kernel,
        grid=(n // seq,),
        in_specs=[pl.BlockSpec((seq, FN_WIDTH), lambda b: (b, 0)), full(wc), full(fp)],
        out_specs=pl.BlockSpec((seq, FN_WIDTH), lambda b: (b, 0)),
        out_shape=jax.ShapeDtypeStruct((n, FN_WIDTH), F32),
        compiler_params=_params(("arbitrary",)),
        name="fn_ctx",
    )(u, wc, fp)


_FN_G = 8


def _fn_lat_kernel(u_ref, wc_ref, f1_ref, f2_ref, twc_ref, tws_ref, o_ref, a_ref, t_ref):
    n = GRID_W
    w = FN_GROUP_W

    def chan(i, c):
        rows = pl.ds(pl.multiple_of(i * 512, 512), 512)
        a = _chan_dft(u_ref[rows, :], wc_ref)
        a_ref[0, rows, :] = a[:, :FN_GROUP_W]
        a_ref[1, rows, :] = a[:, FN_GROUP_W:]
        return c

    lax.fori_loop(0, u_ref.shape[0] // 512, chan, 0)

    def gather(ref, base):
        cols = [jnp.concatenate([ref[0, pl.ds(base + j, n, stride=n), :],
                                 ref[1, pl.ds(base + j, n, stride=n), :]], axis=0) for j in range(_FN_G)]
        return jnp.concatenate(cols, axis=1)

    def dft(f_ref, d):
        rows = f_ref.shape[0] // 2
        dh, dl = _split(d)
        rh = _dot(f_ref[...], dh)
        return rh[:rows] + rh[rows:] + _dot(f_ref[:rows, :], dl)

    def stage1(i, c):
        base = i * _FN_G
        b = dft(f1_ref, gather(a_ref, base))
        for j in range(_FN_G):
            br = b[:n, j * w:(j + 1) * w]
            bi = b[n:, j * w:(j + 1) * w]
            rows = pl.ds(pl.multiple_of((base + j) * n, n), n)
            tc = twc_ref[rows, :]
            ts = tws_ref[rows, :]
            t_ref[0, rows, :] = br * tc + bi * ts
            t_ref[1, rows, :] = bi * tc - br * ts
        return c

    lax.fori_loop(0, n // _FN_G, stage1, 0)

    def stage2(i, c):
        base = i * _FN_G
        y = dft(f2_ref, gather(t_ref, base))
        for j in range(_FN_G):
            o_ref[pl.ds(base + j, n, stride=n), :] = y[:, j * w:(j + 1) * w]
        return c

    lax.fori_loop(0, n // _FN_G, stage2, 0)


def _fn_lat(u, *, batch, seq):
    n = GRID_W
    assert seq == n * n
    cc, sc = _dft_mats(FN_GROUP_W, FN_GROUP_W ** -0.5)
    wc = jnp.stack(_hi_lo_np(np.concatenate([cc, -sc], axis=1)))
    c1, s1 = _dft_mats(n, 1.0)
    c2, s2 = _dft_mats(n, 1.0 / n)
    f1 = jnp.concatenate(_hi_lo_np(np.block([[c1, s1], [-s1, c1]])), axis=0)
    f2 = jnp.concatenate(_hi_lo_np(np.concatenate([c2, s2], axis=1)), axis=0)
    n2 = np.arange(n)[:, None]
    k1 = np.arange(n)[None, :]
    ang = (2.0 * np.pi * ((n2 * k1) % seq) / seq).reshape(seq, 1)
    twc = jnp.asarray(np.broadcast_to(np.cos(ang), (seq, FN_GROUP_W)), F32)
    tws = jnp.asarray(np.broadcast_to(np.sin(ang), (seq, FN_GROUP_W)), F32)
    full = lambda a: pl.BlockSpec(a.shape, lambda b, g: (0,) * a.ndim)
    blk = pl.BlockSpec((seq, FN_GROUP_W), lambda b, g: (b, g))
    return pl.pallas_call(
        _fn_lat_kernel,
        grid=(batch, FN_GROUPS),
        in_specs=[blk, full(wc), full(f1), full(f2), full(twc), full(tws)],
        out_specs=blk,
        out_shape=jax.ShapeDtypeStruct(u.shape, F32),
        scratch_shapes=[pltpu.VMEM((2, seq, FN_GROUP_W), F32), pltpu.VMEM((2, seq, FN_GROUP_W), F32)],
        compiler_params=_params(("arbitrary", "arbitrary")),
        name="fn_lat",
    )(u, wc, f1, f2, twc, tws)


_E_GNA, _E_GMLA, _E_GFN, _E_MERGE = 0, 512, 1024, 1536


def _merge_kernel(*refs, final):
    (x_ref, mod_ref, ng_ref, wg_ref, ona_ref, omla_ref, ofn_ref, wona_ref, womla_ref, wofn_ref,
     wout_ref) = refs[:11]
    if final:
        fg_ref, o_ref = refs[11:13]
    else:
        o_ref = refs[11]
    x, xm = _modulated(x_ref, mod_ref, ng_ref)
    gate = mod_ref[0, 2:3, :]

    merged = None
    for i, (br_ref, wo_ref) in enumerate(((ona_ref, wona_ref), (omla_ref, womla_ref), (ofn_ref, wofn_ref))):
        g = _dot_nt(xm, wg_ref[0, i * 512:(i + 1) * 512, :])
        t = _dot((br_ref[...] * _silu(g)).astype(BF16), wo_ref[0])
        ml = _dot_nt(xm, wg_ref[0, _E_MERGE + i * D_MODEL:_E_MERGE + (i + 1) * D_MODEL, :])
        t = _sigmoid(ml) * t
        merged = t if merged is None else merged + t
    h = x + gate * _dot(merged.astype(BF16), wout_ref[0])
    if final:
        h = _rms(h, fg_ref[...])
    o_ref[...] = h


def _layer_spec(a, layer):
    return pl.BlockSpec((1,) + a.shape[1:], lambda *_: (layer,) + (0,) * (a.ndim - 1))


def _merge(x, mod, ng, w_g, ona, omla, ofn, wona, womla, wofn, wout, fg, *, tile, layer):
    n = x.shape[0]
    groups = mod.shape[0]
    steps = n // tile
    per_group = steps // groups
    final = fg is not None
    tok = lambda w: pl.BlockSpec((tile, w), lambda i: (i, 0))
    full = lambda a: pl.BlockSpec(a.shape, lambda i: (0,) * a.ndim)
    lay = lambda a: _layer_spec(a, layer)
    in_specs = [tok(D_MODEL), pl.BlockSpec((1, 3, D_MODEL), lambda i: (i // per_group, 0, 0)),
                full(ng), lay(w_g), tok(512), tok(512), tok(512),
                lay(wona), lay(womla), lay(wofn), lay(wout)]
    args = [x, mod, ng, w_g, ona, omla, ofn, wona, womla, wofn, wout]
    if final:
        in_specs.append(full(fg))
        args.append(fg)
    return pl.pallas_call(
        functools.partial(_merge_kernel, final=final),
        grid=(steps,),
        in_specs=in_specs,
        out_specs=tok(D_MODEL),
        out_shape=jax.ShapeDtypeStruct((n, D_MODEL), F32),
        compiler_params=_params(("arbitrary",)),
        name="merge",
    )(*args)


def _rope_rot(w, axis):
    q = MLA_ROPE // 4
    part = lambda i: lax.slice_in_dim(w, i * q, (i + 1) * q, axis=axis)
    return jnp.concatenate([-part(1), part(0), -part(3), part(2)], axis=axis)


def _prep_w_in(w_in):
    wt = jnp.swapaxes(w_in, 1, 2)
    rows = lambda start, n: wt[:, start:start + n]
    kr = rows(_O_KROPE, MLA_ROPE)
    kr_rot = _rope_rot(kr, 1)
    a = jnp.concatenate([rows(_O_QKV, 1536), rows(_O_QLAT, Q_LORA + KV_LORA),
                         jnp.tile(kr, (1, 4, 1)), jnp.tile(kr_rot, (1, 4, 1)), rows(_O_UFN, FN_WIDTH)], axis=1)
    e = jnp.concatenate([rows(_O_GATE_NA, 512), rows(_O_GATE_MLA, 512), rows(_O_GATE_FN, 512),
                         rows(_O_MERGE, 3 * D_MODEL)], axis=1)
    return a.astype(BF16), e.astype(BF16)


def _prep_w_uq(w):
    w3 = w.reshape(Q_LORA, MLA_HEADS, MLA_NOPE + MLA_ROPE)
    nope = w3[:, :, :MLA_NOPE].reshape(Q_LORA, MLA_HEADS * MLA_NOPE)
    rope = w3[:, :, MLA_NOPE:]
    rot = _rope_rot(rope, 2)
    return jnp.concatenate([nope, rope.reshape(Q_LORA, -1), rot.reshape(Q_LORA, -1)], axis=1).astype(BF16)


def _prep_w_ukv(w):
    w3 = w.reshape(KV_LORA, MLA_HEADS, MLA_NOPE + MLA_V)
    return jnp.concatenate([w3[:, :, :MLA_NOPE].reshape(KV_LORA, -1),
                            w3[:, :, MLA_NOPE:].reshape(KV_LORA, -1)], axis=1).astype(BF16)


def _rope_tables(n):
    t = jnp.arange(n, dtype=jnp.int32)
    row = (t // GRID_W).astype(F32)
    col = (t % GRID_W).astype(F32)
    half = MLA_ROPE // 2
    inv_freq = ROPE_THETA ** (-jnp.arange(0, half, 2, dtype=F32) / half)
    ar = row[:, None] * inv_freq[None, :]
    ac = col[:, None] * inv_freq[None, :]
    ang = jnp.concatenate([ar, ar, ac, ac], axis=-1)
    return jnp.tile(jnp.cos(ang), (1, MLA_HEADS)), jnp.tile(jnp.sin(ang), (1, MLA_HEADS))


def kernel(x_prompt, x_sample, cache_na_k, cache_na_v, cache_mla_ckv, cache_mla_krope, c, c_ctx,
           w_ada, b_ada, norm_g, w_in, q_norm_g, kv_norm_g, w_uq, w_ukv, na_bias,
           w_o_na, w_o_mla, w_o_fourier, w_out, final_norm_g):
    batch, seq, _ = x_prompt.shape
    dbatch, dseq, _ = x_sample.shape
    past = cache_na_k.shape[2]

    cond = jnp.zeros((8, D_MODEL), F32).at[0].set(c_ctx).at[1:1 + dbatch].set(c)
    mods = _ada_mods(cond, w_ada, b_ada).reshape(DEPTH, 8, 3, D_MODEL)

    wukv_all = jnp.stack([_prep_w_ukv(w_ukv[l]) for l in range(DEPTH)])
    wukv_vt_all = wukv_all[:, :, 512:].transpose(0, 2, 1)
    knc_all, mvtc_all = _cache_kv(cache_mla_ckv.transpose(1, 0, 2, 3), wukv_all[:, :, :512], wukv_vt_all)
    krc_all = jnp.tile(cache_mla_krope, (1, 1, 1, 4)).astype(BF16)
    nakc_all = cache_na_k.reshape(dbatch, DEPTH, past, NA_WIDTH).astype(BF16)
    navt = cache_na_v.transpose(0, 1, 3, 4, 2).astype(BF16)
    navtc_all = jnp.concatenate([navt, jnp.ones((dbatch, DEPTH, NA_HEADS, ONES_ROWS, past), BF16)],
                                axis=3).reshape(dbatch, DEPTH, VT_ALL, past)
    cos, sin = _rope_tables(dseq)

    h_ctx = x_prompt.reshape(batch * seq, D_MODEL)
    h_lat = x_sample.reshape(dbatch * dseq, D_MODEL)
    ks, vs, ckvs, krs = [], [], [], []
    row = lambda a: a.reshape(1, -1)
    w_a, w_e = _prep_w_in(w_in)
    wona, womla, wofn, wout = (w_o_na.astype(BF16), w_o_mla.astype(BF16),
                               w_o_fourier.astype(BF16), w_out.astype(BF16))
    for l in range(DEPTH):
        wuq = _prep_w_uq(w_uq[l])
        wukv = wukv_all[l]
        ng, qng, kvng = row(norm_g[l]), row(q_norm_g[l]), row(kv_norm_g[l])
        fg = row(final_norm_g) if l == DEPTH - 1 else None
        mod_ctx = mods[l, 0:1]
        mod_lat = mods[l, 1:1 + dbatch]

        (q, k, v, qn, qr, kn, kr, mv, u, kf, vf, ckvf, krf) = _inproj(
            h_ctx, mod_ctx, ng, w_a, qng, kvng, wuq, wukv, None, tile=512, layer=l, seq=seq)
        ks.append(kf)
        vs.append(vf)
        ckvs.append(ckvf)
        krs.append(krf)
        ona, omla = _ctx_attn(q, k, v, qn, qr, kn, kr, mv, seq=seq)
        ofn = _fn_ctx(u, seq=seq)
        h_ctx = _merge(h_ctx, mod_ctx, ng, w_e, ona, omla, ofn, wona, womla, wofn, wout, fg, tile=512, layer=l)

        (q, k, vt, qn, qr, kn, kr, mvt, u) = _inproj(
            h_lat, mod_lat, ng, w_a, qng, kvng, wuq, wukv, (wukv_vt_all[l], cos, sin), tile=512, layer=l)
        ona = _na_lat(q, k, vt, nakc_all[:, l], navtc_all[:, l], na_bias[l], batch=dbatch, seq=dseq)
        omla = _mla_lat(qn, qr, kn, kr, mvt, knc_all[l], krc_all[:, l], mvtc_all[l],
                        batch=dbatch, seq=dseq, tq=256, tk=512)
        ofn = _fn_lat(u, batch=dbatch, seq=dseq)
        h_lat = _merge(h_lat, mod_lat, ng, w_e, ona, omla, ofn, wona, womla, wofn, wout, fg, tile=512, layer=l)

    y_prompt = h_ctx.reshape(batch, seq, D_MODEL)
    y_sample = h_lat.reshape(dbatch, dseq, D_MODEL)
    heads_t = lambda xs: jnp.stack(xs, axis=1).reshape(
        batch, DEPTH, NA_HEADS, NA_HEAD_DIM, seq).transpose(0, 1, 4, 2, 3)
    new_na_k = heads_t(ks)
    new_na_v = heads_t(vs)
    new_mla_ckv = jnp.stack([a.reshape(batch, seq, KV_LORA) for a in ckvs], axis=1)
    new_mla_krope = jnp.stack(krs, axis=1).transpose(0, 1, 3, 2)
    return (y_prompt, y_sample, new_na_k, new_na_v, new_mla_ckv, new_mla_krope)
```

```python
import functools

import jax
import jax.numpy as jnp
import numpy as np
from jax import lax
from jax.experimental import pallas as pl
from jax.experimental.pallas import tpu as pltpu

F32 = jnp.float32
BF16 = jnp.bfloat16

D_MODEL = 1024
DEPTH = 4
GRID_W = 64
NA_HEADS = 8
NA_HEAD_DIM = 64
NA_WIDTH = 512
NA_ROWS = 8
NA_COLS = 16
MLA_HEADS = 8
MLA_NOPE = 64
MLA_ROPE = 32
MLA_V = 64
MLA_WIDTH = 512
Q_LORA = 256
KV_LORA = 128
MLA_SCALE = (MLA_NOPE + MLA_ROPE) ** -0.5
ROPE_THETA = 10000.0
FN_GROUPS = 4
FN_GROUP_W = 128
FN_WIDTH = 512
EPS = 1e-6
NEG = -1e30

_O_QKV = 0
_O_GATE_NA = 1536
_O_QLAT = 2048
_O_CKV = 2304
_O_KROPE = 2432
_O_GATE_MLA = 2464
_O_UFN = 2976
_O_GATE_FN = 3488
_O_MERGE = 4000
D_IN = 7072

LANES = 128
VMEM_LIMIT = 56 * 1024 * 1024


def _params(sem):
    return pltpu.CompilerParams(dimension_semantics=sem, vmem_limit_bytes=VMEM_LIMIT)


def _sigmoid(x):
    return 1.0 / (1.0 + jnp.exp(-x))


def _silu(x):
    return x * _sigmoid(x)


def _rms(x, g):
    return x * lax.rsqrt(jnp.mean(x * x, axis=-1, keepdims=True) + EPS) * g


def _dot(a, b):
    return jnp.dot(a, b, preferred_element_type=F32)


def _dot_nt(a, b):
    return lax.dot_general(a, b, (((1,), (1,)), ((), ())), preferred_element_type=F32)


def _split(x):
    hi = x.astype(BF16)
    lo = (x - hi.astype(F32)).astype(BF16)
    return hi, lo


def _ada_kernel(cond_ref, w_ref, b_ref, o_ref):
    a = _silu(cond_ref[...]).astype(BF16)
    o_ref[0] = _dot(a, w_ref[0].astype(BF16)) + b_ref[0]


def _ada_mods(cond, w_ada, b_ada):
    nb = 3
    return pl.pallas_call(
        _ada_kernel,
        grid=(DEPTH, nb),
        in_specs=[
            pl.BlockSpec((8, D_MODEL), lambda l, j: (0, 0)),
            pl.BlockSpec((1, D_MODEL, D_MODEL), lambda l, j: (l, 0, j)),
            pl.BlockSpec((1, 1, D_MODEL), lambda l, j: (l, 0, j)),
        ],
        out_specs=pl.BlockSpec((1, 8, D_MODEL), lambda l, j: (l, 0, j)),
        out_shape=jax.ShapeDtypeStruct((DEPTH, 8, 3 * D_MODEL), F32),
        compiler_params=_params(("arbitrary", "arbitrary")),
        name="ada_mods",
    )(cond, w_ada, b_ada.reshape(DEPTH, 1, 3 * D_MODEL))


_A_Q, _A_K, _A_V, _A_QLAT, _A_CKV, _A_KR, _A_KRROT, _A_U, _A_END = (
    0, 512, 1024, 1536, 1792, 1920, 2048, 2176, 2688)


def _modulated(x_ref, mod_ref, ng_ref):
    x = x_ref[...]
    shift = mod_ref[0, 0:1, :]
    scale = mod_ref[0, 1:2, :]
    return x, (_rms(x, ng_ref[...]) * (1.0 + scale) + shift).astype(BF16)


KV_CHUNK = 256
HEAD_DIM = 64
ONES_ROWS = 16
VT_ROWS = HEAD_DIM + ONES_ROWS
VT_ALL = 8 * VT_ROWS


def _store_vt(vt_ref, vt):
    for c in range(vt_ref.shape[0]):
        cols = slice(c * KV_CHUNK, (c + 1) * KV_CHUNK)
        for h in range(8):
            vt_ref[c, h * VT_ROWS:h * VT_ROWS + HEAD_DIM, :] = vt[h * HEAD_DIM:(h + 1) * HEAD_DIM, cols].astype(BF16)
            vt_ref[c, h * VT_ROWS + HEAD_DIM:(h + 1) * VT_ROWS, :] = jnp.ones((ONES_ROWS, KV_CHUNK), BF16)


def _store_seq_t(ref, xt):
    seq = ref.shape[2]
    for j in range(ref.shape[0]):
        ref[j] = xt[:, j * seq:(j + 1) * seq]


def _inproj_kernel(*refs, latent):
    (x_ref, mod_ref, ng_ref, w_ref, qng_ref, kvng_ref, wuq_ref, wukv_ref) = refs[:8]
    if latent:
        (wmvt_ref, cos_ref, sin_ref,
         naq_ref, nak_ref, navt_ref, qn_ref, qr_ref, kn_ref, kr_ref, mvt_ref, u_ref) = refs[8:]
    else:
        (naq_ref, nak_ref, nav_ref, qn_ref, qr_ref, kn_ref, kr_ref, mv_ref, u_ref,
         kf_ref, vf_ref, ckvf_ref, krf_ref) = refs[8:]

    _, xm = _modulated(x_ref, mod_ref, ng_ref)

    def proj(a, b):
        return _dot_nt(xm, w_ref[0, a:b, :])

    q = proj(_A_Q, _A_K)
    naq_ref[...] = (q * (NA_HEAD_DIM ** -0.5)).astype(BF16)
    k = proj(_A_K, _A_V)
    nak_ref[...] = k.astype(BF16)
    if latent:
        _store_vt(navt_ref, _dot_nt(w_ref[0, _A_V:_A_QLAT, :], xm))
    else:
        v = proj(_A_V, _A_QLAT)
        nav_ref[...] = v.astype(BF16)
        _store_seq_t(kf_ref, k.T)
        _store_seq_t(vf_ref, v.T)

    qlat = _rms(proj(_A_QLAT, _A_CKV), qng_ref[...]).astype(BF16)
    qq = _dot(qlat, wuq_ref[...])
    qn_ref[...] = qq[:, :512].astype(BF16)
    qr = qq[:, 512:768]
    if latent:
        qr = qr * cos_ref[...] + qq[:, 768:1024] * sin_ref[...]
    qr_ref[...] = qr.astype(BF16)

    ckv = _rms(proj(_A_CKV, _A_KR), kvng_ref[...])
    ckv_b = ckv.astype(BF16)
    kn_ref[...] = _dot(ckv_b, wukv_ref[:, :512]).astype(BF16)
    if latent:
        _store_vt(mvt_ref, _dot_nt(wmvt_ref[...], ckv_b))
    else:
        ckvf_ref[...] = ckv
        mv_ref[...] = _dot(ckv_b, wukv_ref[:, 512:]).astype(BF16)

    kr = proj(_A_KR, _A_KRROT)
    if latent:
        kr = kr * cos_ref[:, :LANES] + proj(_A_KRROT, _A_U) * sin_ref[:, :LANES]
    else:
        _store_seq_t(krf_ref, kr.T[:MLA_ROPE])
    kr_ref[...] = kr.astype(BF16)

    u_ref[...] = proj(_A_U, _A_END)


def _inproj(x, mod, ng, w_a, qng, kvng, wuq, wukv, lat_extra, *, tile, layer, seq=None):
    n = x.shape[0]
    groups = mod.shape[0]
    steps = n // tile
    per_group = steps // groups
    latent = lat_extra is not None
    tok = lambda w: pl.BlockSpec((tile, w), lambda i: (i, 0))
    full = lambda a: pl.BlockSpec(a.shape, lambda i: (0,) * a.ndim)
    in_specs = [tok(D_MODEL),
                pl.BlockSpec((1, 3, D_MODEL), lambda i: (i // per_group, 0, 0)),
                full(ng), _layer_spec(w_a, layer), full(qng), full(kvng), full(wuq), full(wukv)]
    args = [x, mod, ng, w_a, qng, kvng, wuq, wukv]
    out = lambda w, dt: (jax.ShapeDtypeStruct((n, w), dt), tok(w))
    if latent:
        wmvt, cos, sin = lat_extra
        steps_per_seq = cos.shape[0] // tile
        in_specs += [full(wmvt)] + [pl.BlockSpec((tile, 256), lambda i: (i % steps_per_seq, 0))] * 2
        args += [wmvt, cos, sin]
        cpt = tile // KV_CHUNK
        vt = (jax.ShapeDtypeStruct((n // KV_CHUNK, VT_ALL, KV_CHUNK), BF16),
              pl.BlockSpec((cpt, VT_ALL, KV_CHUNK), lambda i: (i, 0, 0)))
        outs = [out(512, BF16), out(512, BF16), vt, out(512, BF16), out(256, BF16), out(512, BF16),
                out(LANES, BF16), vt, out(512, F32)]
    else:
        out_t = lambda w: (jax.ShapeDtypeStruct((n // seq, w, seq), F32),
                           pl.BlockSpec((tile // seq, w, seq), lambda i: (i, 0, 0)))
        outs = [out(512, BF16)] * 3 + [out(512, BF16), out(256, BF16), out(512, BF16), out(LANES, BF16),
                                      out(512, BF16), out(512, F32),
                                      out_t(512), out_t(512), out(KV_LORA, F32), out_t(MLA_ROPE)]
    return pl.pallas_call(
        functools.partial(_inproj_kernel, latent=latent),
        grid=(steps,),
        in_specs=in_specs,
        out_specs=[o[1] for o in outs],
        out_shape=[o[0] for o in outs],
        compiler_params=_params(("arbitrary",)),
        name="inproj_lat" if latent else "inproj_ctx",
    )(*args)


def _lane_iota(n=LANES):
    return lax.broadcasted_iota(jnp.int32, (1, n), 1)


def _head_q(q_pair, half):
    lane = _lane_iota()
    keep = (lane < NA_HEAD_DIM) if half == 0 else (lane >= NA_HEAD_DIM)
    return jnp.where(keep, q_pair, jnp.zeros_like(q_pair))


def _mla_q(qn_ref, qr_ref, h):
    p, half = divmod(h, 2)
    qa = _head_q(qn_ref[:, p * LANES:(p + 1) * LANES], half)
    g, slot = divmod(h, 4)
    qb = qr_ref[:, g * LANES:(g + 1) * LANES]
    lane = _lane_iota()
    keep = (lane >= slot * MLA_ROPE) & (lane < (slot + 1) * MLA_ROPE)
    qb = jnp.where(keep, qb, jnp.zeros_like(qb))
    return jnp.concatenate([qa, qb], axis=1)


def _pair_out(o0, o1):
    return jnp.where(_lane_iota() < NA_HEAD_DIM, o0, o1)


def _softmax_pv(s_list, v_list):
    m = s_list[0].max(axis=-1, keepdims=True)
    for s in s_list[1:]:
        m = jnp.maximum(m, s.max(axis=-1, keepdims=True))
    l = None
    o = None
    for s, v in zip(s_list, v_list):
        e = jnp.exp(s - m)
        ls = e.sum(axis=-1, keepdims=True)
        pv = _dot(e.astype(BF16), v)
        l = ls if l is None else l + ls
        o = pv if o is None else o + pv
    return o / l


def _ctx_attn_kernel(q_ref, k_ref, v_ref, qn_ref, qr_ref, kn_ref, kr_ref, mv_ref, ona_ref, omla_ref):
    for p in range(NA_HEADS // 2):
        sl = slice(p * LANES, (p + 1) * LANES)
        kp = k_ref[:, sl]
        vp = v_ref[:, sl]
        outs = []
        for half in range(2):
            s = _dot_nt(_head_q(q_ref[:, sl], half), kp)
            outs.append(_softmax_pv([s], [vp]))
        ona_ref[:, sl] = _pair_out(*outs)

        kk = jnp.concatenate([kn_ref[:, sl], kr_ref[...]], axis=1)
        mvp = mv_ref[:, sl]
        outs = []
        for half in range(2):
            s = _dot_nt(_mla_q(qn_ref, qr_ref, 2 * p + half), kk) * MLA_SCALE
            outs.append(_softmax_pv([s], [mvp]))
        omla_ref[:, sl] = _pair_out(*outs)


def _ctx_attn(q, k, v, qn, qr, kn, kr, mv, *, seq):
    n = q.shape[0]
    tok = lambda w: pl.BlockSpec((seq, w), lambda b: (b, 0))
    return pl.pallas_call(
        _ctx_attn_kernel,
        grid=(n // seq,),
        in_specs=[tok(512), tok(512), tok(512), tok(512), tok(256), tok(512), tok(LANES), tok(512)],
        out_specs=[tok(512), tok(512)],
        out_shape=[jax.ShapeDtypeStruct((n, 512), F32)] * 2,
        compiler_params=_params(("arbitrary",)),
        name="ctx_attn",
    )(q, k, v, qn, qr, kn, kr, mv)


NA_QROWS = 4
NA_KROWS = 12
_NDR = 2 * NA_ROWS - 1
_NDC = 2 * NA_COLS - 1
_Z_NONE = 48
LOG2E = 1.4426950408889634


def _na_tile_descriptors(rows):
    last_q0 = rows - NA_QROWS
    last_k0 = rows - NA_KROWS
    desc = np.full((3, NA_KROWS, NA_QROWS // 2), _Z_NONE, np.int32)
    for case, (q0, k0) in enumerate(((0, 0), (NA_QROWS, 0), (last_q0, last_k0))):
        for kri in range(NA_KROWS):
            for u in range(NA_QROWS // 2):
                info = []
                for ri in (2 * u, 2 * u + 1):
                    qrow, krow = q0 + ri, k0 + kri
                    start = min(max(qrow - NA_ROWS // 2, 0), rows - NA_ROWS)
                    info.append((start <= krow < start + NA_ROWS, krow - qrow))
                (vl, drl), (vr, _) = info
                d = drl + NA_ROWS - 1
                if vl and vr:
                    desc[case, kri, u] = d
                elif vl:
                    desc[case, kri, u] = 16 + d
                elif vr:
                    desc[case, kri, u] = 32 + d
    return desc.reshape(-1)


def _na_bias_consts():
    col = np.arange(GRID_W)
    col_start = np.clip(col - NA_COLS // 2, 0, GRID_W - NA_COLS)
    kc, qc = col[:, None], col[None, :]
    col_in = (kc >= col_start[None, :]) & (kc < col_start[None, :] + NA_COLS)
    negm = np.where(col_in, 0.0, NEG).astype(np.float32)
    return np.concatenate([negm, negm], axis=1)


def _na_build_bias(bias_ref, negm_ref, z_ref):
    lane = _lane_iota()
    left = lane < GRID_W
    in_window = negm_ref[...] == 0.0

    def coef(h, dr, j):
        if dr < -(NA_ROWS - 1) or dr > NA_ROWS - 1:
            return 0.0
        return bias_ref[(h * _NDR + dr + NA_ROWS - 1) * _NDC + j]

    def per_head(h, c):
        for d in range(16):
            dr_l = d - (NA_ROWS - 1)
            base = jnp.zeros((1, LANES), F32)
            for j in range(_NDC):
                c_left = (NA_COLS - 1 - j) % LANES
                c_right = GRID_W + NA_COLS - 1 - j
                base = jnp.where(lane == c_left, coef(h, dr_l, j), base)
                base = jnp.where(lane == c_right, coef(h, dr_l - 1, j), base)
            rolled = pltpu.roll(jnp.broadcast_to(base, (GRID_W, LANES)), 0, axis=1, stride=1, stride_axis=0)
            acc = jnp.where(in_window, rolled, NEG)
            z_ref[h, d] = acc
            z_ref[h, 16 + d] = jnp.where(left, acc, NEG)
            z_ref[h, 32 + d] = jnp.where(left, NEG, acc)
        z_ref[h, _Z_NONE] = jnp.full((GRID_W, LANES), NEG, F32)
        return c

    lax.fori_loop(0, NA_HEADS, per_head, 0)


def _finish_pair(acc_a, acc_b):
    norm = lambda acc: acc[:HEAD_DIM] * (1.0 / acc[HEAD_DIM:HEAD_DIM + 1])
    return jnp.concatenate([norm(acc_a), norm(acc_b)], axis=0).T


def _na_lat_kernel(bias_ref, desc_ref, q_ref, k_ref, vt_ref, kc_ref, vtc_ref, negm_ref, o_ref, z_ref,
                   *, rows):
    g = pl.program_id(1)
    groups = rows // NA_QROWS

    @pl.when((pl.program_id(0) == 0) & (g == 0))
    def _():
        _na_build_bias(bias_ref, negm_ref, z_ref)

    tq = NA_QROWS * GRID_W
    k_row0 = jnp.clip(g * NA_QROWS - NA_ROWS // 2, 0, rows - NA_KROWS)
    start = pl.multiple_of(k_row0 * GRID_W, KV_CHUNK)
    chunk0 = k_row0 // (KV_CHUNK // GRID_W)
    n_loc = NA_KROWS * GRID_W
    case = jnp.where(g == 0, 0, jnp.where(g == groups - 1, 2, 1))
    upairs = NA_QROWS // 2

    def scores(h):
        sl = slice((h // 2) * LANES, (h // 2 + 1) * LANES)
        qt = _head_q(q_ref[:, sl], h % 2).astype(F32).T.astype(BF16)
        return _dot(k_ref[pl.ds(start, n_loc), sl], qt), _dot(kc_ref[0, :, sl], qt)

    def attend(h, s_loc, s_ctx):
        rows_h = slice(h * VT_ROWS, (h + 1) * VT_ROWS)
        blocks = []
        for kri in range(NA_KROWS):
            tiles = []
            for u in range(upairs):
                idx = desc_ref[(case * NA_KROWS + kri) * upairs + u]
                tiles.append(s_loc[kri * GRID_W:(kri + 1) * GRID_W, u * LANES:(u + 1) * LANES] + z_ref[h, idx])
            blocks.append(jnp.concatenate(tiles, axis=1))
        s_loc = jnp.concatenate(blocks, axis=0)
        m = jnp.maximum(s_loc.max(axis=0, keepdims=True), s_ctx.max(axis=0, keepdims=True))
        p_loc = jnp.exp2((s_loc - m) * LOG2E).astype(BF16)
        p_ctx = jnp.exp2((s_ctx - m) * LOG2E).astype(BF16)
        acc = _dot(vtc_ref[0, rows_h, :], p_ctx)
        for i in range(n_loc // KV_CHUNK):
            acc = acc + _dot(vt_ref[chunk0 + i, rows_h, :], p_loc[i * KV_CHUNK:(i + 1) * KV_CHUNK])
        return acc

    s_next = scores(0)
    accs = []
    for h in range(NA_HEADS):
        s = s_next
        if h + 1 < NA_HEADS:
            s_next = scores(h + 1)
        accs.append(attend(h, *s))
        if h % 2:
            o_ref[:, (h // 2) * LANES:(h // 2 + 1) * LANES] = _finish_pair(accs[h - 1], accs[h])


def _na_lat(q, k, vt, kc, vtc, bias_l, *, batch, seq):
    rows = seq // GRID_W
    n = q.shape[0]
    groups = rows // NA_QROWS
    tq = NA_QROWS * GRID_W
    chunks = seq // KV_CHUNK
    negm = _na_bias_consts()
    desc = jnp.asarray(_na_tile_descriptors(rows))
    smem = pl.BlockSpec(memory_space=pltpu.SMEM)
    full = lambda a: pl.BlockSpec(a.shape, lambda b, g: (0,) * a.ndim)
    return pl.pallas_call(
        functools.partial(_na_lat_kernel, rows=rows),
        grid=(batch, groups),
        in_specs=[
            smem, smem,
            pl.BlockSpec((tq, 512), lambda b, g: (b * groups + g, 0)),
            pl.BlockSpec((seq, 512), lambda b, g: (b, 0)),
            pl.BlockSpec((chunks, VT_ALL, KV_CHUNK), lambda b, g: (b, 0, 0)),
            pl.BlockSpec((1,) + kc.shape[1:], lambda b, g: (b, 0, 0)),
            pl.BlockSpec((1,) + vtc.shape[1:], lambda b, g: (b, 0, 0)),
            full(negm),
        ],
        out_specs=pl.BlockSpec((tq, 512), lambda b, g: (b * groups + g, 0)),
        out_shape=jax.ShapeDtypeStruct((n, 512), F32),
        scratch_shapes=[pltpu.VMEM((NA_HEADS, _Z_NONE + 1, GRID_W, LANES), F32)],
        compiler_params=_params(("arbitrary", "arbitrary")),
        name="na_lat",
    )(bias_l.reshape(-1), desc, q, k, vt, kc, vtc, jnp.asarray(negm))


_MLA_C = MLA_SCALE * LOG2E


def _mla_lat_kernel(qn_ref, qr_ref, kn_ref, kr_ref, vt_ref, knc_ref, krc_ref, vtc_ref, o_ref,
                    q2_ref, m_ref, acc_ref, sa_ref, sb_ref, *, tk):
    tq = qn_ref.shape[0]
    n_tiles = kn_ref.shape[0] // tk
    cpt = tk // KV_CHUNK
    pairs = MLA_HEADS // 2
    for p in range(pairs):
        qt = [_mla_q(qn_ref, qr_ref, 2 * p + half).astype(F32).T.astype(BF16) for half in range(2)]
        q2_ref[p] = jnp.concatenate(qt, axis=1)
        m_ref[p] = jnp.full((1, 2 * tq), NEG, F32)
        acc_ref[p] = jnp.zeros((2, VT_ROWS, tq), F32)

    def scores(p, kk):
        return _dot(kk, q2_ref[p])

    def update(p, s, vts):
        m_prev = m_ref[p]
        m_new = jnp.maximum(m_prev, s.max(axis=0, keepdims=True))
        alpha = jnp.exp2((m_prev - m_new) * _MLA_C)
        pt = jnp.exp2((s - m_new) * _MLA_C).astype(BF16)
        accs = []
        for half in range(2):
            cols = slice(half * tq, (half + 1) * tq)
            acc = acc_ref[p, half] * alpha[:, cols]
            for i, vt in enumerate(vts):
                acc = acc + _dot(vt[half * VT_ROWS:(half + 1) * VT_ROWS], pt[i * KV_CHUNK:(i + 1) * KV_CHUNK, cols])
            accs.append(acc)
        return m_new, accs

    def store(new):
        for p, (m_new, accs) in enumerate(new):
            acc_ref[p, 0] = accs[0]
            acc_ref[p, 1] = accs[1]
            m_ref[p] = m_new

    def lanes(p):
        return slice(p * LANES, (p + 1) * LANES)

    def rows(p):
        return slice(2 * p * VT_ROWS, 2 * (p + 1) * VT_ROWS)

    def keys(t, p):
        start = t * tk if isinstance(t, int) else pl.multiple_of(t * tk, tk)
        ks = pl.ds(start, tk)
        return jnp.concatenate([kn_ref[ks, lanes(p)], kr_ref[ks, :]], axis=1)

    for p in range(pairs):
        sa_ref[p] = scores(p, keys(0, p))

    n_ctx = knc_ref.shape[1]

    def latent_next(t):
        def put(nxt_ref, p):
            nxt_ref[p] = scores(p, keys(t + 1, p))
        return put

    def ctx_next(nxt_ref, p):
        kk = jnp.concatenate([knc_ref[0, :, lanes(p)], krc_ref[0]], axis=1)
        nxt_ref[p, :n_ctx] = scores(p, kk)

    def tile_step(t, cur_ref, nxt_ref, put_next):
        ahead = 1
        for p in range(ahead):
            put_next(nxt_ref, p)
        new = []
        for p in range(pairs):
            new.append(update(p, cur_ref[p], [vt_ref[t * cpt + i, rows(p), :] for i in range(cpt)]))
            if p + ahead < pairs:
                put_next(nxt_ref, p + ahead)
        store(new)

    def body(i, c):
        tile_step(2 * i, sa_ref, sb_ref, latent_next(2 * i))
        tile_step(2 * i + 1, sb_ref, sa_ref, latent_next(2 * i + 1))
        return c

    lax.fori_loop(0, n_tiles // 2 - 1, body, 0)
    tile_step(n_tiles - 2, sa_ref, sb_ref, latent_next(n_tiles - 2))
    tile_step(n_tiles - 1, sb_ref, sa_ref, ctx_next)
    for p in range(pairs):
        _, accs = update(p, sa_ref[p, :n_ctx], [vtc_ref[0, rows(p), :]])
        o_ref[:, lanes(p)] = _finish_pair(*accs)


def _mla_lat(qn, qr, kn, kr, vt, knc, krc, vtc, *, batch, seq, tq, tk):
    n = qn.shape[0]
    steps = seq // tq
    chunks = seq // KV_CHUNK
    assert knc.shape[1] == KV_CHUNK
    qtok = lambda w: pl.BlockSpec((tq, w), lambda b, i: (b * steps + i, 0))
    ktok = lambda w: pl.BlockSpec((seq, w), lambda b, i: (b, 0))
    ctok = lambda a: pl.BlockSpec((1,) + a.shape[1:], lambda b, i: (b, 0, 0))
    pairs = MLA_HEADS // 2
    return pl.pallas_call(
        functools.partial(_mla_lat_kernel, tk=tk),
        grid=(batch, steps),
        in_specs=[qtok(512), qtok(256), ktok(512), ktok(LANES),
                  pl.BlockSpec((chunks, VT_ALL, KV_CHUNK), lambda b, i: (b, 0, 0)),
                  ctok(knc), ctok(krc), ctok(vtc)],
        out_specs=qtok(512),
        out_shape=jax.ShapeDtypeStruct((n, 512), F32),
        scratch_shapes=[pltpu.VMEM((pairs, 2 * LANES, 2 * tq), BF16),
                        pltpu.VMEM((pairs, 1, 2 * tq), F32),
                        pltpu.VMEM((pairs, 2, VT_ROWS, tq), F32),
                        pltpu.VMEM((pairs, tk, 2 * tq), F32),
                        pltpu.VMEM((pairs, tk, 2 * tq), F32)],
        compiler_params=_params(("arbitrary", "arbitrary")),
        name="mla_lat",
    )(qn, qr, kn, kr, vt, knc, krc, vtc)


def _cache_kv_kernel(ckv_ref, w_ref, wvt_ref, kn_ref, vt_ref):
    ckv = ckv_ref[0, 0].astype(BF16)
    kn_ref[0, 0] = _dot(ckv, w_ref[0]).astype(BF16)
    _store_vt(vt_ref.at[0], _dot_nt(wvt_ref[0], ckv))


def _cache_kv(ckv, wukv_k, wukv_vt):
    _, nb, rows, _ = ckv.shape
    assert rows == KV_CHUNK
    spec = lambda a, b: pl.BlockSpec((1, a, b), lambda l, j: (l, 0, 0))
    return pl.pallas_call(
        _cache_kv_kernel,
        grid=(DEPTH, nb),
        in_specs=[pl.BlockSpec((1, 1, rows, KV_LORA), lambda l, j: (l, j, 0, 0)),
                  spec(KV_LORA, 512), spec(512, KV_LORA)],
        out_specs=[pl.BlockSpec((1, 1, rows, 512), lambda l, j: (l, j, 0, 0)),
                   pl.BlockSpec((1, 1, VT_ALL, KV_CHUNK), lambda l, j: (l, j, 0, 0))],
        out_shape=[jax.ShapeDtypeStruct((DEPTH, nb, rows, 512), BF16),
                   jax.ShapeDtypeStruct((DEPTH, nb, VT_ALL, KV_CHUNK), BF16)],
        compiler_params=_params(("arbitrary", "arbitrary")),
        name="cache_kv",
    )(ckv, wukv_k, wukv_vt)


def _dft_mats(n, scale):
    k = np.arange(n)
    ang = 2.0 * np.pi * ((k[:, None] * k[None, :]) % n) / n
    return np.cos(ang) * scale, np.sin(ang) * scale


def _hi_lo_np(m):
    m = jnp.asarray(m, F32)
    hi = m.astype(BF16)
    lo = (m - hi.astype(F32)).astype(BF16)
    return hi, lo


def _chan_dft(u, wc_ref):
    uh, ul = _split(u)
    wh = wc_ref[0]
    wl = wc_ref[1]
    return _dot(uh, wh) + _dot(ul, wh) + _dot(uh, wl)


def _fn_ctx_kernel(u_ref, wc_ref, fp_ref, o_ref):
    his, los = [], []
    for g in range(FN_GROUPS):
        sl = slice(g * FN_GROUP_W, (g + 1) * FN_GROUP_W)
        a = _chan_dft(u_ref[:, sl], wc_ref)
        ar_h, ar_l = _split(a[:, :FN_GROUP_W])
        ai_h, ai_l = _split(a[:, FN_GROUP_W:])
        his.append(jnp.concatenate([ar_h, ai_h], axis=0))
        los.append(jnp.concatenate([ar_l, ai_l], axis=0))
    rh = jnp.concatenate(his, axis=1)
    rl = jnp.concatenate(los, axis=1)
    o_ref[...] = _dot(fp_ref[0], rh) + _dot(fp_ref[0], rl) + _dot(fp_ref[1], rh)


def _fn_ctx(u, *, seq):
    n = u.shape[0]
    cc, sc = _dft_mats(FN_GROUP_W, FN_GROUP_W ** -0.5)
    wc = jnp.stack(_hi_lo_np(np.concatenate([cc, -sc], axis=1)))
    cp, sp = _dft_mats(seq, seq ** -0.5)
    fp = jnp.stack(_hi_lo_np(np.concatenate([cp, sp], axis=1)))
    full = lambda a: pl.BlockSpec(a.shape, lambda b: (0,) * a.ndim)
    return pl.pallas_call(
        _fn_ctx_kernel,
        grid=(n // seq,),
        in_specs=[pl.BlockSpec((seq, FN_WIDTH), lambda b: (b, 0)), full(wc), full(fp)],
        out_specs=pl.BlockSpec((seq, FN_WIDTH), lambda b: (b, 0)),
        out_shape=jax.ShapeDtypeStruct((n, FN_WIDTH), F32),
        compiler_params=_params(("arbitrary",)),
        name="fn_ctx",
    )(u, wc, fp)


_FN_G = 8


def _fn_lat_kernel(u_ref, wc_ref, f1_ref, f2_ref, twc_ref, tws_ref, o_ref, a_ref, t_ref):
    n = GRID_W
    w = FN_GROUP_W

    def chan(i, c):
        rows = pl.ds(pl.multiple_of(i * 512, 512), 512)
        a = _chan_dft(u_ref[rows, :], wc_ref)
        a_ref[0, rows, :] = a[:, :FN_GROUP_W]
        a_ref[1, rows, :] = a[:, FN_GROUP_W:]
        return c

    lax.fori_loop(0, u_ref.shape[0] // 512, chan, 0)

    def gather(ref, base):
        cols = [jnp.concatenate([ref[0, pl.ds(base + j, n, stride=n), :],
                                 ref[1, pl.ds(base + j, n, stride=n), :]], axis=0) for j in range(_FN_G)]
        return jnp.concatenate(cols, axis=1)

    def dft(f_ref, d):
        rows = f_ref.shape[0] // 2
        dh, dl = _split(d)
        rh = _dot(f_ref[...], dh)
        return rh[:rows] + rh[rows:] + _dot(f_ref[:rows, :], dl)

    def stage1(i, c):
        base = i * _FN_G
        b = dft(f1_ref, gather(a_ref, base))
        for j in range(_FN_G):
            br = b[:n, j * w:(j + 1) * w]
            bi = b[n:, j * w:(j + 1) * w]
            rows = pl.ds(pl.multiple_of((base + j) * n, n), n)
            tc = twc_ref[rows, :]
            ts = tws_ref[rows, :]
            t_ref[0, rows, :] = br * tc + bi * ts
            t_ref[1, rows, :] = bi * tc - br * ts
        return c

    lax.fori_loop(0, n // _FN_G, stage1, 0)

    def stage2(i, c):
        base = i * _FN_G
        y = dft(f2_ref, gather(t_ref, base))
        for j in range(_FN_G):
            o_ref[pl.ds(base + j, n, stride=n), :] = y[:, j * w:(j + 1) * w]
        return c

    lax.fori_loop(0, n // _FN_G, stage2, 0)


def _fn_lat(u, *, batch, seq):
    n = GRID_W
    assert seq == n * n
    cc, sc = _dft_mats(FN_GROUP_W, FN_GROUP_W ** -0.5)
    wc = jnp.stack(_hi_lo_np(np.concatenate([cc, -sc], axis=1)))
    c1, s1 = _dft_mats(n, 1.0)
    c2, s2 = _dft_mats(n, 1.0 / n)
    f1 = jnp.concatenate(_hi_lo_np(np.block([[c1, s1], [-s1, c1]])), axis=0)
    f2 = jnp.concatenate(_hi_lo_np(np.concatenate([c2, s2], axis=1)), axis=0)
    n2 = np.arange(n)[:, None]
    k1 = np.arange(n)[None, :]
    ang = (2.0 * np.pi * ((n2 * k1) % seq) / seq).reshape(seq, 1)
    twc = jnp.asarray(np.broadcast_to(np.cos(ang), (seq, FN_GROUP_W)), F32)
    tws = jnp.asarray(np.broadcast_to(np.sin(ang), (seq, FN_GROUP_W)), F32)
    full = lambda a: pl.BlockSpec(a.shape, lambda b, g: (0,) * a.ndim)
    blk = pl.BlockSpec((seq, FN_GROUP_W), lambda b, g: (b, g))
    return pl.pallas_call(
        _fn_lat_kernel,
        grid=(batch, FN_GROUPS),
        in_specs=[blk, full(wc), full(f1), full(f2), full(twc), full(tws)],
        out_specs=blk,
        out_shape=jax.ShapeDtypeStruct(u.shape, F32),
        scratch_shapes=[pltpu.VMEM((2, seq, FN_GROUP_W), F32), pltpu.VMEM((2, seq, FN_GROUP_W), F32)],
        compiler_params=_params(("arbitrary", "arbitrary")),
        name="fn_lat",
    )(u, wc, f1, f2, twc, tws)


_E_GNA, _E_GMLA, _E_GFN, _E_MERGE = 0, 512, 1024, 1536


def _merge_kernel(*refs, final):
    (x_ref, mod_ref, ng_ref, wg_ref, ona_ref, omla_ref, ofn_ref, wona_ref, womla_ref, wofn_ref,
     wout_ref) = refs[:11]
    if final:
        fg_ref, o_ref = refs[11:13]
    else:
        o_ref = refs[11]
    x, xm = _modulated(x_ref, mod_ref, ng_ref)
    gate = mod_ref[0, 2:3, :]

    merged = None
    for i, (br_ref, wo_ref) in enumerate(((ona_ref, wona_ref), (omla_ref, womla_ref), (ofn_ref, wofn_ref))):
        g = _dot_nt(xm, wg_ref[0, i * 512:(i + 1) * 512, :])
        t = _dot((br_ref[...] * _silu(g)).astype(BF16), wo_ref[0])
        ml = _dot_nt(xm, wg_ref[0, _E_MERGE + i * D_MODEL:_E_MERGE + (i + 1) * D_MODEL, :])
        t = _sigmoid(ml) * t
        merged = t if merged is None else merged + t
    h = x + gate * _dot(merged.astype(BF16), wout_ref[0])
    if final:
        h = _rms(h, fg_ref[...])
    o_ref[...] = h


def _layer_spec(a, layer):
    return pl.BlockSpec((1,) + a.shape[1:], lambda *_: (layer,) + (0,) * (a.ndim - 1))


def _merge(x, mod, ng, w_g, ona, omla, ofn, wona, womla, wofn, wout, fg, *, tile, layer):
    n = x.shape[0]
    groups = mod.shape[0]
    steps = n // tile
    per_group = steps // groups
    final = fg is not None
    tok = lambda w: pl.BlockSpec((tile, w), lambda i: (i, 0))
    full = lambda a: pl.BlockSpec(a.shape, lambda i: (0,) * a.ndim)
    lay = lambda a: _layer_spec(a, layer)
    in_specs = [tok(D_MODEL), pl.BlockSpec((1, 3, D_MODEL), lambda i: (i // per_group, 0, 0)),
                full(ng), lay(w_g), tok(512), tok(512), tok(512),
                lay(wona), lay(womla), lay(wofn), lay(wout)]
    args = [x, mod, ng, w_g, ona, omla, ofn, wona, womla, wofn, wout]
    if final:
        in_specs.append(full(fg))
        args.append(fg)
    return pl.pallas_call(
        functools.partial(_merge_kernel, final=final),
        grid=(steps,),
        in_specs=in_specs,
        out_specs=tok(D_MODEL),
        out_shape=jax.ShapeDtypeStruct((n, D_MODEL), F32),
        compiler_params=_params(("arbitrary",)),
        name="merge",
    )(*args)


def _rope_rot(w, axis):
    q = MLA_ROPE // 4
    part = lambda i: lax.slice_in_dim(w, i * q, (i + 1) * q, axis=axis)
    return jnp.concatenate([-part(1), part(0), -part(3), part(2)], axis=axis)


def _prep_w_in(w_in):
    wt = jnp.swapaxes(w_in, 1, 2)
    rows = lambda start, n: wt[:, start:start + n]
    kr = rows(_O_KROPE, MLA_ROPE)
    kr_rot = _rope_rot(kr, 1)
    a = jnp.concatenate([rows(_O_QKV, 1536), rows(_O_QLAT, Q_LORA + KV_LORA),
                         jnp.tile(kr, (1, 4, 1)), jnp.tile(kr_rot, (1, 4, 1)), rows(_O_UFN, FN_WIDTH)], axis=1)
    e = jnp.concatenate([rows(_O_GATE_NA, 512), rows(_O_GATE_MLA, 512), rows(_O_GATE_FN, 512),
                         rows(_O_MERGE, 3 * D_MODEL)], axis=1)
    return a.astype(BF16), e.astype(BF16)


def _prep_w_uq(w):
    w3 = w.reshape(Q_LORA, MLA_HEADS, MLA_NOPE + MLA_ROPE)
    nope = w3[:, :, :MLA_NOPE].reshape(Q_LORA, MLA_HEADS * MLA_NOPE)
    rope = w3[:, :, MLA_NOPE:]
    rot = _rope_rot(rope, 2)
    return jnp.concatenate([nope, rope.reshape(Q_LORA, -1), rot.reshape(Q_LORA, -1)], axis=1).astype(BF16)


def _prep_w_ukv(w):
    w3 = w.reshape(KV_LORA, MLA_HEADS, MLA_NOPE + MLA_V)
    return jnp.concatenate([w3[:, :, :MLA_NOPE].reshape(KV_LORA, -1),
                            w3[:, :, MLA_NOPE:].reshape(KV_LORA, -1)], axis=1).astype(BF16)


def _rope_tables(n):
    t = jnp.arange(n, dtype=jnp.int32)
    row = (t // GRID_W).astype(F32)
    col = (t % GRID_W).astype(F32)
    half = MLA_ROPE // 2
    inv_freq = ROPE_THETA ** (-jnp.arange(0, half, 2, dtype=F32) / half)
    ar = row[:, None] * inv_freq[None, :]
    ac = col[:, None] * inv_freq[None, :]
    ang = jnp.concatenate([ar, ar, ac, ac], axis=-1)
    return jnp.tile(jnp.cos(ang), (1, MLA_HEADS)), jnp.tile(jnp.sin(ang), (1, MLA_HEADS))


def kernel(x_prompt, x_sample, cache_na_k, cache_na_v, cache_mla_ckv, cache_mla_krope, c, c_ctx,
           w_ada, b_ada, norm_g, w_in, q_norm_g, kv_norm_g, w_uq, w_ukv, na_bias,
           w_o_na, w_o_mla, w_o_fourier, w_out, final_norm_g):
    batch, seq, _ = x_prompt.shape
    dbatch, dseq, _ = x_sample.shape
    past = cache_na_k.shape[2]

    cond = jnp.zeros((8, D_MODEL), F32).at[0].set(c_ctx).at[1:1 + dbatch].set(c)
    mods = _ada_mods(cond, w_ada, b_ada).reshape(DEPTH, 8, 3, D_MODEL)

    wukv_all = jnp.stack([_prep_w_ukv(w_ukv[l]) for l in range(DEPTH)])
    wukv_vt_all = wukv_all[:, :, 512:].transpose(0, 2, 1)
    knc_all, mvtc_all = _cache_kv(cache_mla_ckv.transpose(1, 0, 2, 3), wukv_all[:, :, :512], wukv_vt_all)
    krc_all = jnp.tile(cache_mla_krope, (1, 1, 1, 4)).astype(BF16)
    nakc_all = cache_na_k.reshape(dbatch, DEPTH, past, NA_WIDTH).astype(BF16)
    navt = cache_na_v.transpose(0, 1, 3, 4, 2).astype(BF16)
    navtc_all = jnp.concatenate([navt, jnp.ones((dbatch, DEPTH, NA_HEADS, ONES_ROWS, past), BF16)],
                                axis=3).reshape(dbatch, DEPTH, VT_ALL, past)
    cos, sin = _rope_tables(dseq)

    h_ctx = x_prompt.reshape(batch * seq, D_MODEL)
    h_lat = x_sample.reshape(dbatch * dseq, D_MODEL)
    ks, vs, ckvs, krs = [], [], [], []
    row = lambda a: a.reshape(1, -1)
    w_a, w_e = _prep_w_in(w_in)
    wona, womla, wofn, wout = (w_o_na.astype(BF16), w_o_mla.astype(BF16),
                               w_o_fourier.astype(BF16), w_out.astype(BF16))
    for l in range(DEPTH):
        wuq = _prep_w_uq(w_uq[l])
        wukv = wukv_all[l]
        ng, qng, kvng = row(norm_g[l]), row(q_norm_g[l]), row(kv_norm_g[l])
        fg = row(final_norm_g) if l == DEPTH - 1 else None
        mod_ctx = mods[l, 0:1]
        mod_lat = mods[l, 1:1 + dbatch]

        (q, k, v, qn, qr, kn, kr, mv, u, kf, vf, ckvf, krf) = _inproj(
            h_ctx, mod_ctx, ng, w_a, qng, kvng, wuq, wukv, None, tile=512, layer=l, seq=seq)
        ks.append(kf)
        vs.append(vf)
        ckvs.append(ckvf)
        krs.append(krf)
        ona, omla = _ctx_attn(q, k, v, qn, qr, kn, kr, mv, seq=seq)
        ofn = _fn_ctx(u, seq=seq)
        h_ctx = _merge(h_ctx, mod_ctx, ng, w_e, ona, omla, ofn, wona, womla, wofn, wout, fg, tile=512, layer=l)

        (q, k, vt, qn, qr, kn, kr, mvt, u) = _inproj(
            h_lat, mod_lat, ng, w_a, qng, kvng, wuq, wukv, (wukv_vt_all[l], cos, sin), tile=512, layer=l)
        ona = _na_lat(q, k, vt, nakc_all[:, l], navtc_all[:, l], na_bias[l], batch=dbatch, seq=dseq)
        omla = _mla_lat(qn, qr, kn, kr, mvt, knc_all[l], krc_all[:, l], mvtc_all[l],
                        batch=dbatch, seq=dseq, tq=256, tk=512)
        ofn = _fn_lat(u, batch=dbatch, seq=dseq)
        h_lat = _merge(h_lat, mod_lat, ng, w_e, ona, omla, ofn, wona, womla, wofn, wout, fg, tile=512, layer=l)

    y_prompt = h_ctx.reshape(batch, seq, D_MODEL)
    y_sample = h_lat.reshape(dbatch, dseq, D_MODEL)
    heads_t = lambda xs: jnp.stack(xs, axis=1).reshape(
        batch, DEPTH, NA_HEADS, NA_HEAD_DIM, seq).transpose(0, 1, 4, 2, 3)
    new_na_k = heads_t(ks)
    new_na_v = heads_t(vs)
    new_mla_ckv = jnp.stack([a.reshape(batch, seq, KV_LORA) for a in ckvs], axis=1)
    new_mla_krope = jnp.stack(krs, axis=1).transpose(0, 1, 3, 2)
    return (y_prompt, y_sample, new_na_k, new_na_v, new_mla_ckv, new_mla_krope)
```

```python
import functools

import jax
import jax.numpy as jnp
import numpy as np
from jax import lax
from jax.experimental import pallas as pl
from jax.experimental.pallas import tpu as pltpu

F32 = jnp.float32
BF16 = jnp.bfloat16

D_MODEL = 1024
DEPTH = 4
GRID_W = 64
NA_HEADS = 8
NA_HEAD_DIM = 64
NA_WIDTH = 512
NA_ROWS = 8
NA_COLS = 16
MLA_HEADS = 8
MLA_NOPE = 64
MLA_ROPE = 32
MLA_V = 64
MLA_WIDTH = 512
Q_LORA = 256
KV_LORA = 128
MLA_SCALE = (MLA_NOPE + MLA_ROPE) ** -0.5
ROPE_THETA = 10000.0
FN_GROUPS = 4
FN_GROUP_W = 128
FN_WIDTH = 512
EPS = 1e-6
NEG = -1e30

_O_QKV = 0
_O_GATE_NA = 1536
_O_QLAT = 2048
_O_CKV = 2304
_O_KROPE = 2432
_O_GATE_MLA = 2464
_O_UFN = 2976
_O_GATE_FN = 3488
_O_MERGE = 4000
D_IN = 7072

LANES = 128
SUBLANES = 8
VMEM_LIMIT = 56 * 1024 * 1024

TOKEN_TILE = 512
MLA_TQ = 256
MLA_TK = 512


def _params(sem):
    return pltpu.CompilerParams(dimension_semantics=sem, vmem_limit_bytes=VMEM_LIMIT)


def _sigmoid(x):
    return 1.0 / (1.0 + jnp.exp(-x))


def _silu(x):
    return x * _sigmoid(x)


def _rms(x, g):
    return x * lax.rsqrt(jnp.mean(x * x, axis=-1, keepdims=True) + EPS) * g


def _dot(a, b):
    return jnp.dot(a, b, preferred_element_type=F32)


def _dot_nt(a, b):
    return lax.dot_general(a, b, (((1,), (1,)), ((), ())), preferred_element_type=F32)


def _split(x):
    hi = x.astype(BF16)
    lo = (x - hi.astype(F32)).astype(BF16)
    return hi, lo


def _ada_kernel(cond_ref, w_ref, b_ref, o_ref):
    a = _silu(cond_ref[...]).astype(BF16)
    o_ref[0] = _dot(a, w_ref[0].astype(BF16)) + b_ref[0]


def _ada_mods(cond, w_ada, b_ada):
    nb = 3
    return pl.pallas_call(
        _ada_kernel,
        grid=(DEPTH, nb),
        in_specs=[
            pl.BlockSpec((SUBLANES, D_MODEL), lambda l, j: (0, 0)),
            pl.BlockSpec((1, D_MODEL, D_MODEL), lambda l, j: (l, 0, j)),
            pl.BlockSpec((1, 1, D_MODEL), lambda l, j: (l, 0, j)),
        ],
        out_specs=pl.BlockSpec((1, SUBLANES, D_MODEL), lambda l, j: (l, 0, j)),
        out_shape=jax.ShapeDtypeStruct((DEPTH, SUBLANES, 3 * D_MODEL), F32),
        compiler_params=_params(("arbitrary", "arbitrary")),
        name="ada_mods",
    )(cond, w_ada, b_ada.reshape(DEPTH, 1, 3 * D_MODEL))


_A_Q, _A_K, _A_V, _A_QLAT, _A_CKV, _A_KR, _A_KRROT, _A_U, _A_END = (
    0, 512, 1024, 1536, 1792, 1920, 2048, 2176, 2688)


def _modulated(x_ref, mod_ref, ng_ref):
    x = x_ref[...]
    shift = mod_ref[0, 0:1, :]
    scale = mod_ref[0, 1:2, :]
    return x, (_rms(x, ng_ref[...]) * (1.0 + scale) + shift).astype(BF16)


KV_CHUNK = 256
HEAD_DIM = 64
ONES_ROWS = 16
VT_ROWS = HEAD_DIM + ONES_ROWS
VT_ALL = 8 * VT_ROWS


def _store_vt(vt_ref, vt):
    for c in range(vt_ref.shape[0]):
        cols = slice(c * KV_CHUNK, (c + 1) * KV_CHUNK)
        for h in range(8):
            vt_ref[c, h * VT_ROWS:h * VT_ROWS + HEAD_DIM, :] = vt[h * HEAD_DIM:(h + 1) * HEAD_DIM, cols].astype(BF16)
            vt_ref[c, h * VT_ROWS + HEAD_DIM:(h + 1) * VT_ROWS, :] = jnp.ones((ONES_ROWS, KV_CHUNK), BF16)


def _store_seq_t(ref, xt):
    seq = ref.shape[2]
    for j in range(ref.shape[0]):
        ref[j] = xt[:, j * seq:(j + 1) * seq]


def _inproj_kernel(*refs, latent):
    (x_ref, mod_ref, ng_ref, w_ref, qng_ref, kvng_ref, wuq_ref, wukv_ref) = refs[:8]
    if latent:
        (wmvt_ref, cos_ref, sin_ref,
         naq_ref, nak_ref, navt_ref, qn_ref, qr_ref, kn_ref, kr_ref, mvt_ref, u_ref) = refs[8:]
    else:
        (naq_ref, nak_ref, nav_ref, qn_ref, qr_ref, kn_ref, kr_ref, mv_ref, u_ref,
         kf_ref, vf_ref, ckvf_ref, krf_ref) = refs[8:]

    _, xm = _modulated(x_ref, mod_ref, ng_ref)

    def proj(a, b):
        return _dot_nt(xm, w_ref[0, a:b, :])

    q = proj(_A_Q, _A_K)
    naq_ref[...] = (q * (NA_HEAD_DIM ** -0.5)).astype(BF16)
    k = proj(_A_K, _A_V)
    nak_ref[...] = k.astype(BF16)
    if latent:
        _store_vt(navt_ref, _dot_nt(w_ref[0, _A_V:_A_QLAT, :], xm))
    else:
        v = proj(_A_V, _A_QLAT)
        nav_ref[...] = v.astype(BF16)
        _store_seq_t(kf_ref, k.T)
        _store_seq_t(vf_ref, v.T)

    qlat = _rms(proj(_A_QLAT, _A_CKV), qng_ref[...]).astype(BF16)
    qq = _dot(qlat, wuq_ref[...])
    qn_ref[...] = qq[:, :512].astype(BF16)
    qr = qq[:, 512:768]
    if latent:
        qr = qr * cos_ref[...] + qq[:, 768:1024] * sin_ref[...]
    qr_ref[...] = qr.astype(BF16)

    ckv = _rms(proj(_A_CKV, _A_KR), kvng_ref[...])
    ckv_b = ckv.astype(BF16)
    kn_ref[...] = _dot(ckv_b, wukv_ref[:, :512]).astype(BF16)
    if latent:
        _store_vt(mvt_ref, _dot_nt(wmvt_ref[...], ckv_b))
    else:
        ckvf_ref[...] = ckv
        mv_ref[...] = _dot(ckv_b, wukv_ref[:, 512:]).astype(BF16)

    kr = proj(_A_KR, _A_KRROT)
    if latent:
        kr = kr * cos_ref[:, :LANES] + proj(_A_KRROT, _A_U) * sin_ref[:, :LANES]
    else:
        _store_seq_t(krf_ref, kr.T[:MLA_ROPE])
    kr_ref[...] = kr.astype(BF16)

    u_ref[...] = proj(_A_U, _A_END)


def _inproj(x, mod, ng, w_a, qng, kvng, wuq, wukv, lat_extra, *, tile, layer, seq=None):
    n = x.shape[0]
    groups = mod.shape[0]
    steps = n // tile
    per_group = steps // groups
    latent = lat_extra is not None
    tok = lambda w: pl.BlockSpec((tile, w), lambda i: (i, 0))
    full = lambda a: pl.BlockSpec(a.shape, lambda i: (0,) * a.ndim)
    in_specs = [tok(D_MODEL),
                pl.BlockSpec((1, 3, D_MODEL), lambda i: (i // per_group, 0, 0)),
                full(ng), _layer_spec(w_a, layer), full(qng), full(kvng), full(wuq), full(wukv)]
    args = [x, mod, ng, w_a, qng, kvng, wuq, wukv]
    out = lambda w, dt: (jax.ShapeDtypeStruct((n, w), dt), tok(w))
    if latent:
        wmvt, cos, sin = lat_extra
        steps_per_seq = cos.shape[0] // tile
        in_specs += [full(wmvt)] + [pl.BlockSpec((tile, 256), lambda i: (i % steps_per_seq, 0))] * 2
        args += [wmvt, cos, sin]
        cpt = tile // KV_CHUNK
        vt = (jax.ShapeDtypeStruct((n // KV_CHUNK, VT_ALL, KV_CHUNK), BF16),
              pl.BlockSpec((cpt, VT_ALL, KV_CHUNK), lambda i: (i, 0, 0)))
        outs = [out(512, BF16), out(512, BF16), vt, out(512, BF16), out(256, BF16), out(512, BF16),
                out(LANES, BF16), vt, out(512, F32)]
    else:
        out_t = lambda w: (jax.ShapeDtypeStruct((n // seq, w, seq), F32),
                           pl.BlockSpec((tile // seq, w, seq), lambda i: (i, 0, 0)))
        outs = [out(512, BF16)] * 3 + [out(512, BF16), out(256, BF16), out(512, BF16), out(LANES, BF16),
                                      out(512, BF16), out(512, F32),
                                      out_t(512), out_t(512), out(KV_LORA, F32), out_t(MLA_ROPE)]
    return pl.pallas_call(
        functools.partial(_inproj_kernel, latent=latent),
        grid=(steps,),
        in_specs=in_specs,
        out_specs=[o[1] for o in outs],
        out_shape=[o[0] for o in outs],
        compiler_params=_params(("arbitrary",)),
        name="inproj_lat" if latent else "inproj_ctx",
    )(*args)


def _lane_iota(n=LANES):
    return lax.broadcasted_iota(jnp.int32, (1, n), 1)


def _head_q(q_pair, half):
    lane = _lane_iota()
    keep = (lane < NA_HEAD_DIM) if half == 0 else (lane >= NA_HEAD_DIM)
    return jnp.where(keep, q_pair, jnp.zeros_like(q_pair))


def _mla_q(qn_ref, qr_ref, h):
    p, half = divmod(h, 2)
    qa = _head_q(qn_ref[:, p * LANES:(p + 1) * LANES], half)
    g, slot = divmod(h, 4)
    qb = qr_ref[:, g * LANES:(g + 1) * LANES]
    lane = _lane_iota()
    keep = (lane >= slot * MLA_ROPE) & (lane < (slot + 1) * MLA_ROPE)
    qb = jnp.where(keep, qb, jnp.zeros_like(qb))
    return jnp.concatenate([qa, qb], axis=1)


def _pair_out(o0, o1):
    return jnp.where(_lane_iota() < NA_HEAD_DIM, o0, o1)


def _softmax_pv(s_list, v_list):
    m = s_list[0].max(axis=-1, keepdims=True)
    for s in s_list[1:]:
        m = jnp.maximum(m, s.max(axis=-1, keepdims=True))
    l = None
    o = None
    for s, v in zip(s_list, v_list):
        e = jnp.exp(s - m)
        ls = e.sum(axis=-1, keepdims=True)
        pv = _dot(e.astype(BF16), v)
        l = ls if l is None else l + ls
        o = pv if o is None else o + pv
    return o / l


def _ctx_attn_kernel(q_ref, k_ref, v_ref, qn_ref, qr_ref, kn_ref, kr_ref, mv_ref, ona_ref, omla_ref):
    for p in range(NA_HEADS // 2):
        sl = slice(p * LANES, (p + 1) * LANES)
        kp = k_ref[:, sl]
        vp = v_ref[:, sl]
        outs = []
        for half in range(2):
            s = _dot_nt(_head_q(q_ref[:, sl], half), kp)
            outs.append(_softmax_pv([s], [vp]))
        ona_ref[:, sl] = _pair_out(*outs)

        kk = jnp.concatenate([kn_ref[:, sl], kr_ref[...]], axis=1)
        mvp = mv_ref[:, sl]
        outs = []
        for half in range(2):
            s = _dot_nt(_mla_q(qn_ref, qr_ref, 2 * p + half), kk) * MLA_SCALE
            outs.append(_softmax_pv([s], [mvp]))
        omla_ref[:, sl] = _pair_out(*outs)


def _ctx_attn(q, k, v, qn, qr, kn, kr, mv, *, seq):
    n = q.shape[0]
    tok = lambda w: pl.BlockSpec((seq, w), lambda b: (b, 0))
    return pl.pallas_call(
        _ctx_attn_kernel,
        grid=(n // seq,),
        in_specs=[tok(512), tok(512), tok(512), tok(512), tok(256), tok(512), tok(LANES), tok(512)],
        out_specs=[tok(512), tok(512)],
        out_shape=[jax.ShapeDtypeStruct((n, 512), F32)] * 2,
        compiler_params=_params(("arbitrary",)),
        name="ctx_attn",
    )(q, k, v, qn, qr, kn, kr, mv)


NA_QROWS = 4
NA_KROWS = 12
_NDR = 2 * NA_ROWS - 1
_NDC = 2 * NA_COLS - 1
_Z_NONE = 48
LOG2E = 1.4426950408889634


def _na_tile_descriptors(rows):
    last_q0 = rows - NA_QROWS
    last_k0 = rows - NA_KROWS
    desc = np.full((3, NA_KROWS, NA_QROWS // 2), _Z_NONE, np.int32)
    for case, (q0, k0) in enumerate(((0, 0), (NA_QROWS, 0), (last_q0, last_k0))):
        for kri in range(NA_KROWS):
            for u in range(NA_QROWS // 2):
                info = []
                for ri in (2 * u, 2 * u + 1):
                    qrow, krow = q0 + ri, k0 + kri
                    start = min(max(qrow - NA_ROWS // 2, 0), rows - NA_ROWS)
                    info.append((start <= krow < start + NA_ROWS, krow - qrow))
                (vl, drl), (vr, _) = info
                d = drl + NA_ROWS - 1
                if vl and vr:
                    desc[case, kri, u] = d
                elif vl:
                    desc[case, kri, u] = 16 + d
                elif vr:
                    desc[case, kri, u] = 32 + d
    return desc.reshape(-1)


def _na_bias_consts():
    col = np.arange(GRID_W)
    col_start = np.clip(col - NA_COLS // 2, 0, GRID_W - NA_COLS)
    kc, qc = col[:, None], col[None, :]
    col_in = (kc >= col_start[None, :]) & (kc < col_start[None, :] + NA_COLS)
    negm = np.where(col_in, 0.0, NEG).astype(np.float32)
    return np.concatenate([negm, negm], axis=1)


def _na_build_bias(bias_ref, negm_ref, z_ref):
    lane = _lane_iota()
    left = lane < GRID_W
    in_window = negm_ref[...] == 0.0

    def coef(h, dr, j):
        if dr < -(NA_ROWS - 1) or dr > NA_ROWS - 1:
            return 0.0
        return bias_ref[(h * _NDR + dr + NA_ROWS - 1) * _NDC + j]

    def per_head(h, c):
        for d in range(16):
            dr_l = d - (NA_ROWS - 1)
            base = jnp.zeros((1, LANES), F32)
            for j in range(_NDC):
                c_left = (NA_COLS - 1 - j) % LANES
                c_right = GRID_W + NA_COLS - 1 - j
                base = jnp.where(lane == c_left, coef(h, dr_l, j), base)
                base = jnp.where(lane == c_right, coef(h, dr_l - 1, j), base)
            rolled = pltpu.roll(jnp.broadcast_to(base, (GRID_W, LANES)), 0, axis=1, stride=1, stride_axis=0)
            acc = jnp.where(in_window, rolled, NEG)
            z_ref[h, d] = acc
            z_ref[h, 16 + d] = jnp.where(left, acc, NEG)
            z_ref[h, 32 + d] = jnp.where(left, NEG, acc)
        z_ref[h, _Z_NONE] = jnp.full((GRID_W, LANES), NEG, F32)
        return c

    lax.fori_loop(0, NA_HEADS, per_head, 0)


def _finish_pair(acc_a, acc_b):
    norm = lambda acc: acc[:HEAD_DIM] * (1.0 / acc[HEAD_DIM:HEAD_DIM + 1])
    return jnp.concatenate([norm(acc_a), norm(acc_b)], axis=0).T


def _na_lat_kernel(bias_ref, desc_ref, q_ref, k_ref, vt_ref, kc_ref, vtc_ref, negm_ref, o_ref, z_ref,
                   *, rows):
    g = pl.program_id(1)
    groups = rows // NA_QROWS

    @pl.when((pl.program_id(0) == 0) & (g == 0))
    def _():
        _na_build_bias(bias_ref, negm_ref, z_ref)

    tq = NA_QROWS * GRID_W
    k_row0 = jnp.clip(g * NA_QROWS - NA_ROWS // 2, 0, rows - NA_KROWS)
    start = pl.multiple_of(k_row0 * GRID_W, KV_CHUNK)
    chunk0 = k_row0 // (KV_CHUNK // GRID_W)
    n_loc = NA_KROWS * GRID_W
    case = jnp.where(g == 0, 0, jnp.where(g == groups - 1, 2, 1))
    upairs = NA_QROWS // 2

    def scores(h):
        sl = slice((h // 2) * LANES, (h // 2 + 1) * LANES)
        qt = _head_q(q_ref[:, sl], h % 2).astype(F32).T.astype(BF16)
        return _dot(k_ref[pl.ds(start, n_loc), sl], qt), _dot(kc_ref[0, :, sl], qt)

    def attend(h, s_loc, s_ctx):
        rows_h = slice(h * VT_ROWS, (h + 1) * VT_ROWS)
        blocks = []
        for kri in range(NA_KROWS):
            tiles = []
            for u in range(upairs):
                idx = desc_ref[(case * NA_KROWS + kri) * upairs + u]
                tiles.append(s_loc[kri * GRID_W:(kri + 1) * GRID_W, u * LANES:(u + 1) * LANES] + z_ref[h, idx])
            blocks.append(jnp.concatenate(tiles, axis=1))
        s_loc = jnp.concatenate(blocks, axis=0)
        m = jnp.maximum(s_loc.max(axis=0, keepdims=True), s_ctx.max(axis=0, keepdims=True))
        p_loc = jnp.exp2((s_loc - m) * LOG2E).astype(BF16)
        p_ctx = jnp.exp2((s_ctx - m) * LOG2E).astype(BF16)
        acc = _dot(vtc_ref[0, rows_h, :], p_ctx)
        for i in range(n_loc // KV_CHUNK):
            acc = acc + _dot(vt_ref[chunk0 + i, rows_h, :], p_loc[i * KV_CHUNK:(i + 1) * KV_CHUNK])
        return acc

    s_next = scores(0)
    accs = []
    for h in range(NA_HEADS):
        s = s_next
        if h + 1 < NA_HEADS:
            s_next = scores(h + 1)
        accs.append(attend(h, *s))
        if h % 2:
            o_ref[:, (h // 2) * LANES:(h // 2 + 1) * LANES] = _finish_pair(accs[h - 1], accs[h])


def _na_lat(q, k, vt, kc, vtc, bias_l, *, batch, seq):
    rows = seq // GRID_W
    n = q.shape[0]
    groups = rows // NA_QROWS
    tq = NA_QROWS * GRID_W
    chunks = seq // KV_CHUNK
    negm = _na_bias_consts()
    desc = jnp.asarray(_na_tile_descriptors(rows))
    smem = pl.BlockSpec(memory_space=pltpu.SMEM)
    full = lambda a: pl.BlockSpec(a.shape, lambda b, g: (0,) * a.ndim)
    return pl.pallas_call(
        functools.partial(_na_lat_kernel, rows=rows),
        grid=(batch, groups),
        in_specs=[
            smem, smem,
            pl.BlockSpec((tq, 512), lambda b, g: (b * groups + g, 0)),
            pl.BlockSpec((seq, 512), lambda b, g: (b, 0)),
            pl.BlockSpec((chunks, VT_ALL, KV_CHUNK), lambda b, g: (b, 0, 0)),
            pl.BlockSpec((1,) + kc.shape[1:], lambda b, g: (b, 0, 0)),
            pl.BlockSpec((1,) + vtc.shape[1:], lambda b, g: (b, 0, 0)),
            full(negm),
        ],
        out_specs=pl.BlockSpec((tq, 512), lambda b, g: (b * groups + g, 0)),
        out_shape=jax.ShapeDtypeStruct((n, 512), F32),
        scratch_shapes=[pltpu.VMEM((NA_HEADS, _Z_NONE + 1, GRID_W, LANES), F32)],
        compiler_params=_params(("arbitrary", "arbitrary")),
        name="na_lat",
    )(bias_l.reshape(-1), desc, q, k, vt, kc, vtc, jnp.asarray(negm))


_MLA_C = MLA_SCALE * LOG2E


def _mla_lat_kernel(qn_ref, qr_ref, kn_ref, kr_ref, vt_ref, knc_ref, krc_ref, vtc_ref, o_ref,
                    q2_ref, m_ref, acc_ref, sa_ref, sb_ref, *, tk):
    tq = qn_ref.shape[0]
    n_tiles = kn_ref.shape[0] // tk
    cpt = tk // KV_CHUNK
    pairs = MLA_HEADS // 2
    for p in range(pairs):
        qt = [_mla_q(qn_ref, qr_ref, 2 * p + half).astype(F32).T.astype(BF16) for half in range(2)]
        q2_ref[p] = jnp.concatenate(qt, axis=1)
        m_ref[p] = jnp.full((1, 2 * tq), NEG, F32)
        acc_ref[p] = jnp.zeros((2, VT_ROWS, tq), F32)

    def scores(p, kk):
        return _dot(kk, q2_ref[p])

    def update(p, s, vts):
        m_prev = m_ref[p]
        m_new = jnp.maximum(m_prev, s.max(axis=0, keepdims=True))
        alpha = jnp.exp2((m_prev - m_new) * _MLA_C)
        pt = jnp.exp2((s - m_new) * _MLA_C).astype(BF16)
        accs = []
        for half in range(2):
            cols = slice(half * tq, (half + 1) * tq)
            acc = acc_ref[p, half] * alpha[:, cols]
            for i, vt in enumerate(vts):
                acc = acc + _dot(vt[half * VT_ROWS:(half + 1) * VT_ROWS], pt[i * KV_CHUNK:(i + 1) * KV_CHUNK, cols])
            accs.append(acc)
        return m_new, accs

    def store(new):
        for p, (m_new, accs) in enumerate(new):
            acc_ref[p, 0] = accs[0]
            acc_ref[p, 1] = accs[1]
            m_ref[p] = m_new

    def lanes(p):
        return slice(p * LANES, (p + 1) * LANES)

    def rows(p):
        return slice(2 * p * VT_ROWS, 2 * (p + 1) * VT_ROWS)

    def keys(t, p):
        start = t * tk if isinstance(t, int) else pl.multiple_of(t * tk, tk)
        ks = pl.ds(start, tk)
        return jnp.concatenate([kn_ref[ks, lanes(p)], kr_ref[ks, :]], axis=1)

    for p in range(pairs):
        sa_ref[p] = scores(p, keys(0, p))

    n_ctx = knc_ref.shape[1]

    def latent_next(t):
        def put(nxt_ref, p):
            nxt_ref[p] = scores(p, keys(t + 1, p))
        return put

    def ctx_next(nxt_ref, p):
        kk = jnp.concatenate([knc_ref[0, :, lanes(p)], krc_ref[0]], axis=1)
        nxt_ref[p, :n_ctx] = scores(p, kk)

    def tile_step(t, cur_ref, nxt_ref, put_next):
        ahead = 1
        for p in range(ahead):
            put_next(nxt_ref, p)
        new = []
        for p in range(pairs):
            new.append(update(p, cur_ref[p], [vt_ref[t * cpt + i, rows(p), :] for i in range(cpt)]))
            if p + ahead < pairs:
                put_next(nxt_ref, p + ahead)
        store(new)

    def body(i, c):
        tile_step(2 * i, sa_ref, sb_ref, latent_next(2 * i))
        tile_step(2 * i + 1, sb_ref, sa_ref, latent_next(2 * i + 1))
        return c

    lax.fori_loop(0, n_tiles // 2 - 1, body, 0)
    tile_step(n_tiles - 2, sa_ref, sb_ref, latent_next(n_tiles - 2))
    tile_step(n_tiles - 1, sb_ref, sa_ref, ctx_next)
    for p in range(pairs):
        _, accs = update(p, sa_ref[p, :n_ctx], [vtc_ref[0, rows(p), :]])
        o_ref[:, lanes(p)] = _finish_pair(*accs)


def _mla_lat(qn, qr, kn, kr, vt, knc, krc, vtc, *, batch, seq, tq, tk):
    n = qn.shape[0]
    steps = seq // tq
    chunks = seq // KV_CHUNK
    assert knc.shape[1] == KV_CHUNK
    qtok = lambda w: pl.BlockSpec((tq, w), lambda b, i: (b * steps + i, 0))
    ktok = lambda w: pl.BlockSpec((seq, w), lambda b, i: (b, 0))
    ctok = lambda a: pl.BlockSpec((1,) + a.shape[1:], lambda b, i: (b, 0, 0))
    pairs = MLA_HEADS // 2
    return pl.pallas_call(
        functools.partial(_mla_lat_kernel, tk=tk),
        grid=(batch, steps),
        in_specs=[qtok(512), qtok(256), ktok(512), ktok(LANES),
                  pl.BlockSpec((chunks, VT_ALL, KV_CHUNK), lambda b, i: (b, 0, 0)),
                  ctok(knc), ctok(krc), ctok(vtc)],
        out_specs=qtok(512),
        out_shape=jax.ShapeDtypeStruct((n, 512), F32),
        scratch_shapes=[pltpu.VMEM((pairs, 2 * LANES, 2 * tq), BF16),
                        pltpu.VMEM((pairs, 1, 2 * tq), F32),
                        pltpu.VMEM((pairs, 2, VT_ROWS, tq), F32),
                        pltpu.VMEM((pairs, tk, 2 * tq), F32),
                        pltpu.VMEM((pairs, tk, 2 * tq), F32)],
        compiler_params=_params(("arbitrary", "arbitrary")),
        name="mla_lat",
    )(qn, qr, kn, kr, vt, knc, krc, vtc)


def _cache_kv_kernel(ckv_ref, w_ref, wvt_ref, kn_ref, vt_ref):
    ckv = ckv_ref[0, 0].astype(BF16)
    kn_ref[0, 0] = _dot(ckv, w_ref[0]).astype(BF16)
    _store_vt(vt_ref.at[0], _dot_nt(wvt_ref[0], ckv))


def _cache_kv(ckv, wukv_k, wukv_vt):
    _, nb, rows, _ = ckv.shape
    assert rows == KV_CHUNK
    spec = lambda a, b: pl.BlockSpec((1, a, b), lambda l, j: (l, 0, 0))
    return pl.pallas_call(
        _cache_kv_kernel,
        grid=(DEPTH, nb),
        in_specs=[pl.BlockSpec((1, 1, rows, KV_LORA), lambda l, j: (l, j, 0, 0)),
                  spec(KV_LORA, 512), spec(512, KV_LORA)],
        out_specs=[pl.BlockSpec((1, 1, rows, 512), lambda l, j: (l, j, 0, 0)),
                   pl.BlockSpec((1, 1, VT_ALL, KV_CHUNK), lambda l, j: (l, j, 0, 0))],
        out_shape=[jax.ShapeDtypeStruct((DEPTH, nb, rows, 512), BF16),
                   jax.ShapeDtypeStruct((DEPTH, nb, VT_ALL, KV_CHUNK), BF16)],
        compiler_params=_params(("arbitrary", "arbitrary")),
        name="cache_kv",
    )(ckv, wukv_k, wukv_vt)


def _dft_mats(n, scale):
    k = np.arange(n)
    ang = 2.0 * np.pi * ((k[:, None] * k[None, :]) % n) / n
    return np.cos(ang) * scale, np.sin(ang) * scale


def _hi_lo_np(m):
    m = jnp.asarray(m, F32)
    hi = m.astype(BF16)
    lo = (m - hi.astype(F32)).astype(BF16)
    return hi, lo


def _chan_dft(u, wc_ref):
    uh, ul = _split(u)
    wh = wc_ref[0]
    wl = wc_ref[1]
    return _dot(uh, wh) + _dot(ul, wh) + _dot(uh, wl)


def _fn_ctx_kernel(u_ref, wc_ref, fp_ref, o_ref):
    his, los = [], []
    for g in range(FN_GROUPS):
        sl = slice(g * FN_GROUP_W, (g + 1) * FN_GROUP_W)
        a = _chan_dft(u_ref[:, sl], wc_ref)
        ar_h, ar_l = _split(a[:, :FN_GROUP_W])
        ai_h, ai_l = _split(a[:, FN_GROUP_W:])
        his.append(jnp.concatenate([ar_h, ai_h], axis=0))
        los.append(jnp.concatenate([ar_l, ai_l], axis=0))
    rh = jnp.concatenate(his, axis=1)
    rl = jnp.concatenate(los, axis=1)
    o_ref[...] = _dot(fp_ref[0], rh) + _dot(fp_ref[0], rl) + _dot(fp_ref[1], rh)


def _fn_ctx(u, *, seq):
    n = u.shape[0]
    cc, sc = _dft_mats(FN_GROUP_W, FN_GROUP_W ** -0.5)
    wc = jnp.stack(_hi_lo_np(np.concatenate([cc, -sc], axis=1)))
    cp, sp = _dft_mats(seq, seq ** -0.5)
    fp = jnp.stack(_hi_lo_np(np.concatenate([cp, sp], axis=1)))
    full = lambda a: pl.BlockSpec(a.shape, lambda b: (0,) * a.ndim)
    return pl.pallas_call(
        _fn_ctx_kernel,
        grid=(n // seq,),
        in_specs=[pl.BlockSpec((seq, FN_WIDTH), lambda b: (b, 0)), full(wc), full(fp)],
        out_specs=pl.BlockSpec((seq, FN_WIDTH), lambda b: (b, 0)),
        out_shape=jax.ShapeDtypeStruct((n, FN_WIDTH), F32),
        compiler_params=_params(("arbitrary",)),
        name="fn_ctx",
    )(u, wc, fp)


_FN_G = 16


def _fn_lat_kernel(u_ref, wc_ref, f1_ref, f2_ref, twc_ref, tws_ref, o_ref, a_ref, t_ref):
    n = GRID_W
    w = FN_GROUP_W

    def chan(i, c):
        rows = pl.ds(pl.multiple_of(i * 512, 512), 512)
        a = _chan_dft(u_ref[rows, :], wc_ref)
        a_ref[0, rows, :] = a[:, :FN_GROUP_W]
        a_ref[1, rows, :] = a[:, FN_GROUP_W:]
        return c

    lax.fori_loop(0, u_ref.shape[0] // 512, chan, 0)

    def gather(ref, base):
        cols = [jnp.concatenate([ref[0, pl.ds(base + j, n, stride=n), :],
                                 ref[1, pl.ds(base + j, n, stride=n), :]], axis=0) for j in range(_FN_G)]
        return jnp.concatenate(cols, axis=1)

    def dft(f_ref, d):
        rows = f_ref.shape[0] // 2
        dh, dl = _split(d)
        rh = _dot(f_ref[...], dh)
        return rh[:rows] + rh[rows:] + _dot(f_ref[:rows, :], dl)

    def stage1(i, c):
        base = i * _FN_G
        b = dft(f1_ref, gather(a_ref, base))
        for j in range(_FN_G):
            br = b[:n, j * w:(j + 1) * w]
            bi = b[n:, j * w:(j + 1) * w]
            rows = pl.ds(pl.multiple_of((base + j) * n, n), n)
            tc = twc_ref[rows, :]
            ts = tws_ref[rows, :]
            t_ref[0, rows, :] = br * tc + bi * ts
            t_ref[1, rows, :] = bi * tc - br * ts
        return c

    lax.fori_loop(0, n // _FN_G, stage1, 0)

    def stage2(i, c):
        base = i * _FN_G
        y = dft(f2_ref, gather(t_ref, base))
        for j in range(_FN_G):
            o_ref[pl.ds(base + j, n, stride=n), :] = y[:, j * w:(j + 1) * w]
        return c

    lax.fori_loop(0, n // _FN_G, stage2, 0)


def _fn_lat(u, *, batch, seq):
    n = GRID_W
    assert seq == n * n
    cc, sc = _dft_mats(FN_GROUP_W, FN_GROUP_W ** -0.5)
    wc = jnp.stack(_hi_lo_np(np.concatenate([cc, -sc], axis=1)))
    c1, s1 = _dft_mats(n, 1.0)
    c2, s2 = _dft_mats(n, 1.0 / n)
    f1 = jnp.concatenate(_hi_lo_np(np.block([[c1, s1], [-s1, c1]])), axis=0)
    f2 = jnp.concatenate(_hi_lo_np(np.concatenate([c2, s2], axis=1)), axis=0)
    n2 = np.arange(n)[:, None]
    k1 = np.arange(n)[None, :]
    ang = (2.0 * np.pi * ((n2 * k1) % seq) / seq).reshape(seq, 1)
    twc = jnp.asarray(np.broadcast_to(np.cos(ang), (seq, FN_GROUP_W)), F32)
    tws = jnp.asarray(np.broadcast_to(np.sin(ang), (seq, FN_GROUP_W)), F32)
    full = lambda a: pl.BlockSpec(a.shape, lambda b, g: (0,) * a.ndim)
    blk = pl.BlockSpec((seq, FN_GROUP_W), lambda b, g: (b, g))
    return pl.pallas_call(
        _fn_lat_kernel,
        grid=(batch, FN_GROUPS),
        in_specs=[blk, full(wc), full(f1), full(f2), full(twc), full(tws)],
        out_specs=blk,
        out_shape=jax.ShapeDtypeStruct(u.shape, F32),
        scratch_shapes=[pltpu.VMEM((2, seq, FN_GROUP_W), F32), pltpu.VMEM((2, seq, FN_GROUP_W), F32)],
        compiler_params=_params(("arbitrary", "arbitrary")),
        name="fn_lat",
    )(u, wc, f1, f2, twc, tws)


_E_GNA, _E_GMLA, _E_GFN, _E_MERGE = 0, 512, 1024, 1536


def _merge_kernel(*refs, final):
    (x_ref, mod_ref, ng_ref, wg_ref, ona_ref, omla_ref, ofn_ref, wona_ref, womla_ref, wofn_ref,
     wout_ref) = refs[:11]
    if final:
        fg_ref, o_ref = refs[11:13]
    else:
        o_ref = refs[11]
    x, xm = _modulated(x_ref, mod_ref, ng_ref)
    gate = mod_ref[0, 2:3, :]

    merged = None
    for i, (br_ref, wo_ref) in enumerate(((ona_ref, wona_ref), (omla_ref, womla_ref), (ofn_ref, wofn_ref))):
        g = _dot_nt(xm, wg_ref[0, i * 512:(i + 1) * 512, :])
        t = _dot((br_ref[...] * _silu(g)).astype(BF16), wo_ref[0])
        ml = _dot_nt(xm, wg_ref[0, _E_MERGE + i * D_MODEL:_E_MERGE + (i + 1) * D_MODEL, :])
        t = _sigmoid(ml) * t
        merged = t if merged is None else merged + t
    h = x + gate * _dot(merged.astype(BF16), wout_ref[0])
    if final:
        h = _rms(h, fg_ref[...])
    o_ref[...] = h


def _layer_spec(a, layer):
    return pl.BlockSpec((1,) + a.shape[1:], lambda *_: (layer,) + (0,) * (a.ndim - 1))


def _merge(x, mod, ng, w_g, ona, omla, ofn, wona, womla, wofn, wout, fg, *, tile, layer):
    n = x.shape[0]
    groups = mod.shape[0]
    steps = n // tile
    per_group = steps // groups
    final = fg is not None
    tok = lambda w: pl.BlockSpec((tile, w), lambda i: (i, 0))
    full = lambda a: pl.BlockSpec(a.shape, lambda i: (0,) * a.ndim)
    lay = lambda a: _layer_spec(a, layer)
    in_specs = [tok(D_MODEL), pl.BlockSpec((1, 3, D_MODEL), lambda i: (i // per_group, 0, 0)),
                full(ng), lay(w_g), tok(512), tok(512), tok(512),
                lay(wona), lay(womla), lay(wofn), lay(wout)]
    args = [x, mod, ng, w_g, ona, omla, ofn, wona, womla, wofn, wout]
    if final:
        in_specs.append(full(fg))
        args.append(fg)
    return pl.pallas_call(
        functools.partial(_merge_kernel, final=final),
        grid=(steps,),
        in_specs=in_specs,
        out_specs=tok(D_MODEL),
        out_shape=jax.ShapeDtypeStruct((n, D_MODEL), F32),
        compiler_params=_params(("arbitrary",)),
        name="merge",
    )(*args)


def _rope_rot(w, axis):
    q = MLA_ROPE // 4
    part = lambda i: lax.slice_in_dim(w, i * q, (i + 1) * q, axis=axis)
    return jnp.concatenate([-part(1), part(0), -part(3), part(2)], axis=axis)


def _prep_w_in(w_in):
    wt = jnp.swapaxes(w_in, 1, 2)
    rows = lambda start, n: wt[:, start:start + n]
    kr = rows(_O_KROPE, MLA_ROPE)
    kr_rot = _rope_rot(kr, 1)
    a = jnp.concatenate([rows(_O_QKV, 1536), rows(_O_QLAT, Q_LORA + KV_LORA),
                         jnp.tile(kr, (1, 4, 1)), jnp.tile(kr_rot, (1, 4, 1)), rows(_O_UFN, FN_WIDTH)], axis=1)
    e = jnp.concatenate([rows(_O_GATE_NA, 512), rows(_O_GATE_MLA, 512), rows(_O_GATE_FN, 512),
                         rows(_O_MERGE, 3 * D_MODEL)], axis=1)
    return a.astype(BF16), e.astype(BF16)


def _prep_w_uq(w):
    w3 = w.reshape(Q_LORA, MLA_HEADS, MLA_NOPE + MLA_ROPE)
    nope = w3[:, :, :MLA_NOPE].reshape(Q_LORA, MLA_HEADS * MLA_NOPE)
    rope = w3[:, :, MLA_NOPE:]
    rot = _rope_rot(rope, 2)
    return jnp.concatenate([nope, rope.reshape(Q_LORA, -1), rot.reshape(Q_LORA, -1)], axis=1).astype(BF16)


def _prep_w_ukv(w):
    w3 = w.reshape(KV_LORA, MLA_HEADS, MLA_NOPE + MLA_V)
    return jnp.concatenate([w3[:, :, :MLA_NOPE].reshape(KV_LORA, -1),
                            w3[:, :, MLA_NOPE:].reshape(KV_LORA, -1)], axis=1).astype(BF16)


def _rope_tables(n):
    t = jnp.arange(n, dtype=jnp.int32)
    row = (t // GRID_W).astype(F32)
    col = (t % GRID_W).astype(F32)
    half = MLA_ROPE // 2
    inv_freq = ROPE_THETA ** (-jnp.arange(0, half, 2, dtype=F32) / half)
    ar = row[:, None] * inv_freq[None, :]
    ac = col[:, None] * inv_freq[None, :]
    ang = jnp.concatenate([ar, ar, ac, ac], axis=-1)
    return jnp.tile(jnp.cos(ang), (1, MLA_HEADS)), jnp.tile(jnp.sin(ang), (1, MLA_HEADS))


def kernel(x_prompt, x_sample, cache_na_k, cache_na_v, cache_mla_ckv, cache_mla_krope, c, c_ctx,
           w_ada, b_ada, norm_g, w_in, q_norm_g, kv_norm_g, w_uq, w_ukv, na_bias,
           w_o_na, w_o_mla, w_o_fourier, w_out, final_norm_g):
    batch, seq, _ = x_prompt.shape
    dbatch, dseq, _ = x_sample.shape
    past = cache_na_k.shape[2]

    cond = jnp.zeros((SUBLANES, D_MODEL), F32).at[0].set(c_ctx).at[1:1 + dbatch].set(c)
    mods = _ada_mods(cond, w_ada, b_ada).reshape(DEPTH, SUBLANES, 3, D_MODEL)

    wukv_all = jnp.stack([_prep_w_ukv(w_ukv[l]) for l in range(DEPTH)])
    wukv_vt_all = wukv_all[:, :, 512:].transpose(0, 2, 1)
    knc_all, mvtc_all = _cache_kv(cache_mla_ckv.transpose(1, 0, 2, 3), wukv_all[:, :, :512], wukv_vt_all)
    krc_all = jnp.tile(cache_mla_krope, (1, 1, 1, 4)).astype(BF16)
    nakc_all = cache_na_k.reshape(dbatch, DEPTH, past, NA_WIDTH).astype(BF16)
    navt = cache_na_v.transpose(0, 1, 3, 4, 2).astype(BF16)
    navtc_all = jnp.concatenate([navt, jnp.ones((dbatch, DEPTH, NA_HEADS, ONES_ROWS, past), BF16)],
                                axis=3).reshape(dbatch, DEPTH, VT_ALL, past)
    cos, sin = _rope_tables(dseq)

    h_ctx = x_prompt.reshape(batch * seq, D_MODEL)
    h_lat = x_sample.reshape(dbatch * dseq, D_MODEL)
    ks, vs, ckvs, krs = [], [], [], []
    row = lambda a: a.reshape(1, -1)
    w_a, w_e = _prep_w_in(w_in)
    wona, womla, wofn, wout = (w_o_na.astype(BF16), w_o_mla.astype(BF16),
                               w_o_fourier.astype(BF16), w_out.astype(BF16))
    for l in range(DEPTH):
        wuq = _prep_w_uq(w_uq[l])
        wukv = wukv_all[l]
        ng, qng, kvng = row(norm_g[l]), row(q_norm_g[l]), row(kv_norm_g[l])
        fg = row(final_norm_g) if l == DEPTH - 1 else None
        mod_ctx = mods[l, 0:1]
        mod_lat = mods[l, 1:1 + dbatch]

        (q, k, v, qn, qr, kn, kr, mv, u, kf, vf, ckvf, krf) = _inproj(
            h_ctx, mod_ctx, ng, w_a, qng, kvng, wuq, wukv, None, tile=TOKEN_TILE, layer=l, seq=seq)
        ks.append(kf)
        vs.append(vf)
        ckvs.append(ckvf)
        krs.append(krf)
        ona, omla = _ctx_attn(q, k, v, qn, qr, kn, kr, mv, seq=seq)
        ofn = _fn_ctx(u, seq=seq)
        h_ctx = _merge(h_ctx, mod_ctx, ng, w_e, ona, omla, ofn, wona, womla, wofn, wout, fg, tile=TOKEN_TILE, layer=l)

        (q, k, vt, qn, qr, kn, kr, mvt, u) = _inproj(
            h_lat, mod_lat, ng, w_a, qng, kvng, wuq, wukv, (wukv_vt_all[l], cos, sin), tile=TOKEN_TILE, layer=l)
        ona = _na_lat(q, k, vt, nakc_all[:, l], navtc_all[:, l], na_bias[l], batch=dbatch, seq=dseq)
        omla = _mla_lat(qn, qr, kn, kr, mvt, knc_all[l], krc_all[:, l], mvtc_all[l],
                        batch=dbatch, seq=dseq, tq=MLA_TQ, tk=MLA_TK)
        ofn = _fn_lat(u, batch=dbatch, seq=dseq)
        h_lat = _merge(h_lat, mod_lat, ng, w_e, ona, omla, ofn, wona, womla, wofn, wout, fg, tile=TOKEN_TILE, layer=l)

    y_prompt = h_ctx.reshape(batch, seq, D_MODEL)
    y_sample = h_lat.reshape(dbatch, dseq, D_MODEL)
    heads_t = lambda xs: jnp.stack(xs, axis=1).reshape(
        batch, DEPTH, NA_HEADS, NA_HEAD_DIM, seq).transpose(0, 1, 4, 2, 3)
    new_na_k = heads_t(ks)
    new_na_v = heads_t(vs)
    new_mla_ckv = jnp.stack([a.reshape(batch, seq, KV_LORA) for a in ckvs], axis=1)
    new_mla_krope = jnp.stack(krs, axis=1).transpose(0, 1, 3, 2)
    return (y_prompt, y_sample, new_na_k, new_na_v, new_mla_ckv, new_mla_krope)
```

```python
import functools

import jax
import jax.numpy as jnp
import numpy as np
from jax import lax
from jax.experimental import pallas as pl
from jax.experimental.pallas import tpu as pltpu

F32 = jnp.float32
BF16 = jnp.bfloat16

D_MODEL = 1024
DEPTH = 4
GRID_W = 64
NA_HEADS = 8
NA_HEAD_DIM = 64
NA_WIDTH = 512
NA_ROWS = 8
NA_COLS = 16
MLA_HEADS = 8
MLA_NOPE = 64
MLA_ROPE = 32
MLA_V = 64
MLA_WIDTH = 512
Q_LORA = 256
KV_LORA = 128
MLA_SCALE = (MLA_NOPE + MLA_ROPE) ** -0.5
ROPE_THETA = 10000.0
FN_GROUPS = 4
FN_GROUP_W = 128
FN_WIDTH = 512
EPS = 1e-6
NEG = -1e30

_O_QKV = 0
_O_GATE_NA = 1536
_O_QLAT = 2048
_O_CKV = 2304
_O_KROPE = 2432
_O_GATE_MLA = 2464
_O_UFN = 2976
_O_GATE_FN = 3488
_O_MERGE = 4000
D_IN = 7072

LANES = 128
SUBLANES = 8
VMEM_LIMIT = 56 * 1024 * 1024

TOKEN_TILE = 512
MLA_TQ = 256
MLA_TK = 512


def _params(sem):
    return pltpu.CompilerParams(dimension_semantics=sem, vmem_limit_bytes=VMEM_LIMIT)


def _sigmoid(x):
    return 1.0 / (1.0 + jnp.exp(-x))


def _silu(x):
    return x * _sigmoid(x)


def _rms(x, g):
    return x * lax.rsqrt(jnp.mean(x * x, axis=-1, keepdims=True) + EPS) * g


def _dot(a, b):
    return jnp.dot(a, b, preferred_element_type=F32)


def _dot_nt(a, b):
    return lax.dot_general(a, b, (((1,), (1,)), ((), ())), preferred_element_type=F32)


def _split(x):
    hi = x.astype(BF16)
    lo = (x - hi.astype(F32)).astype(BF16)
    return hi, lo


def _ada_kernel(cond_ref, w_ref, b_ref, o_ref):
    a = _silu(cond_ref[...]).astype(BF16)
    o_ref[0] = _dot(a, w_ref[0].astype(BF16)) + b_ref[0]


def _ada_mods(cond, w_ada, b_ada):
    nb = 3
    return pl.pallas_call(
        _ada_kernel,
        grid=(DEPTH, nb),
        in_specs=[
            pl.BlockSpec((SUBLANES, D_MODEL), lambda l, j: (0, 0)),
            pl.BlockSpec((1, D_MODEL, D_MODEL), lambda l, j: (l, 0, j)),
            pl.BlockSpec((1, 1, D_MODEL), lambda l, j: (l, 0, j)),
        ],
        out_specs=pl.BlockSpec((1, SUBLANES, D_MODEL), lambda l, j: (l, 0, j)),
        out_shape=jax.ShapeDtypeStruct((DEPTH, SUBLANES, 3 * D_MODEL), F32),
        compiler_params=_params(("arbitrary", "arbitrary")),
        name="ada_mods",
    )(cond, w_ada, b_ada.reshape(DEPTH, 1, 3 * D_MODEL))


_A_Q, _A_K, _A_V, _A_QLAT, _A_CKV, _A_KR, _A_U, _A_END = (
    0, 512, 1024, 1536, 1792, 1920, 2048, 2560)


def _modulated(x_ref, mod_ref, ng_ref):
    x = x_ref[...]
    shift = mod_ref[0, 0:1, :]
    scale = mod_ref[0, 1:2, :]
    return x, (_rms(x, ng_ref[...]) * (1.0 + scale) + shift).astype(BF16)


KV_CHUNK = 256
HEAD_DIM = 64
ONES_ROWS = 16
VT_ROWS = HEAD_DIM + ONES_ROWS
VT_ALL = 8 * VT_ROWS


def _store_vt(vt_ref, vt):
    for c in range(vt_ref.shape[0]):
        cols = slice(c * KV_CHUNK, (c + 1) * KV_CHUNK)
        for h in range(8):
            vt_ref[c, h * VT_ROWS:h * VT_ROWS + HEAD_DIM, :] = vt[h * HEAD_DIM:(h + 1) * HEAD_DIM, cols].astype(BF16)
            vt_ref[c, h * VT_ROWS + HEAD_DIM:(h + 1) * VT_ROWS, :] = jnp.ones((ONES_ROWS, KV_CHUNK), BF16)


def _store_seq_t(ref, xt):
    seq = ref.shape[2]
    for j in range(ref.shape[0]):
        ref[j] = xt[:, j * seq:(j + 1) * seq]


def _inproj_kernel(*refs, latent):
    (x_ref, mod_ref, ng_ref, w_ref, qng_ref, kvng_ref, wuq_ref, wukv_ref) = refs[:8]
    if latent:
        (wmvt_ref, cos_ref, sin_ref,
         naq_ref, nak_ref, navt_ref, qn_ref, qr_ref, kn_ref, kr_ref, mvt_ref, u_ref) = refs[8:]
    else:
        (naq_ref, nak_ref, nav_ref, qn_ref, qr_ref, kn_ref, kr_ref, mv_ref, u_ref,
         kf_ref, vf_ref, ckvf_ref, krf_ref) = refs[8:]

    _, xm = _modulated(x_ref, mod_ref, ng_ref)

    def proj(a, b):
        return _dot_nt(xm, w_ref[0, a:b, :])

    q = proj(_A_Q, _A_K)
    naq_ref[...] = (q * (NA_HEAD_DIM ** -0.5)).astype(BF16)
    k = proj(_A_K, _A_V)
    nak_ref[...] = k.astype(BF16)
    if latent:
        _store_vt(navt_ref, _dot_nt(w_ref[0, _A_V:_A_QLAT, :], xm))
    else:
        v = proj(_A_V, _A_QLAT)
        nav_ref[...] = v.astype(BF16)
        _store_seq_t(kf_ref, k.T)
        _store_seq_t(vf_ref, v.T)

    qlat = _rms(proj(_A_QLAT, _A_CKV), qng_ref[...]).astype(BF16)
    qq = _dot(qlat, wuq_ref[...])
    qn_ref[...] = qq[:, :512].astype(BF16)
    qr = qq[:, 512:768]
    if latent:
        qr = qr * cos_ref[...] + qq[:, 768:1024] * sin_ref[...]
    qr_ref[...] = qr.astype(BF16)

    ckv = _rms(proj(_A_CKV, _A_KR), kvng_ref[...])
    ckv_b = ckv.astype(BF16)
    kn_ref[...] = _dot(ckv_b, wukv_ref[:, :512]).astype(BF16)
    if latent:
        _store_vt(mvt_ref, _dot_nt(wmvt_ref[...], ckv_b))
    else:
        ckvf_ref[...] = ckv
        mv_ref[...] = _dot(ckv_b, wukv_ref[:, 512:]).astype(BF16)

    kr = proj(_A_KR, _A_U)
    plain_group = (_lane_iota() // MLA_ROPE) % 2 == 0
    if latent:
        kr = kr * jnp.where(plain_group, cos_ref[:, :LANES], sin_ref[:, :LANES])
        kr = kr + pltpu.roll(kr, MLA_ROPE, axis=1)
    else:
        _store_seq_t(krf_ref, kr.T[:MLA_ROPE])
        kr = jnp.where(plain_group, kr, pltpu.roll(kr, MLA_ROPE, axis=1))
    kr_ref[...] = kr.astype(BF16)

    u_ref[...] = proj(_A_U, _A_END)


def _inproj(x, mod, ng, w_a, qng, kvng, wuq, wukv, lat_extra, *, tile, layer, seq=None):
    n = x.shape[0]
    groups = mod.shape[0]
    steps = n // tile
    per_group = steps // groups
    latent = lat_extra is not None
    tok = lambda w: pl.BlockSpec((tile, w), lambda i: (i, 0))
    full = lambda a: pl.BlockSpec(a.shape, lambda i: (0,) * a.ndim)
    in_specs = [tok(D_MODEL),
                pl.BlockSpec((1, 3, D_MODEL), lambda i: (i // per_group, 0, 0)),
                full(ng), _layer_spec(w_a, layer), full(qng), full(kvng), full(wuq), full(wukv)]
    args = [x, mod, ng, w_a, qng, kvng, wuq, wukv]
    out = lambda w, dt: (jax.ShapeDtypeStruct((n, w), dt), tok(w))
    if latent:
        wmvt, cos, sin = lat_extra
        steps_per_seq = cos.shape[0] // tile
        in_specs += [full(wmvt)] + [pl.BlockSpec((tile, 256), lambda i: (i % steps_per_seq, 0))] * 2
        args += [wmvt, cos, sin]
        cpt = tile // KV_CHUNK
        vt = (jax.ShapeDtypeStruct((n // KV_CHUNK, VT_ALL, KV_CHUNK), BF16),
              pl.BlockSpec((cpt, VT_ALL, KV_CHUNK), lambda i: (i, 0, 0)))
        outs = [out(512, BF16), out(512, BF16), vt, out(512, BF16), out(256, BF16), out(512, BF16),
                out(LANES, BF16), vt, out(512, F32)]
    else:
        out_t = lambda w: (jax.ShapeDtypeStruct((n // seq, w, seq), F32),
                           pl.BlockSpec((tile // seq, w, seq), lambda i: (i, 0, 0)))
        outs = [out(512, BF16)] * 3 + [out(512, BF16), out(256, BF16), out(512, BF16), out(LANES, BF16),
                                      out(512, BF16), out(512, F32),
                                      out_t(512), out_t(512), out(KV_LORA, F32), out_t(MLA_ROPE)]
    return pl.pallas_call(
        functools.partial(_inproj_kernel, latent=latent),
        grid=(steps,),
        in_specs=in_specs,
        out_specs=[o[1] for o in outs],
        out_shape=[o[0] for o in outs],
        compiler_params=_params(("arbitrary",)),
        name="inproj_lat" if latent else "inproj_ctx",
    )(*args)


def _lane_iota(n=LANES):
    return lax.broadcasted_iota(jnp.int32, (1, n), 1)


def _head_q(q_pair, half):
    lane = _lane_iota()
    keep = (lane < NA_HEAD_DIM) if half == 0 else (lane >= NA_HEAD_DIM)
    return jnp.where(keep, q_pair, jnp.zeros_like(q_pair))


def _mla_q(qn_ref, qr_ref, h):
    p, half = divmod(h, 2)
    qa = _head_q(qn_ref[:, p * LANES:(p + 1) * LANES], half)
    g, slot = divmod(h, 4)
    qb = qr_ref[:, g * LANES:(g + 1) * LANES]
    lane = _lane_iota()
    keep = (lane >= slot * MLA_ROPE) & (lane < (slot + 1) * MLA_ROPE)
    qb = jnp.where(keep, qb, jnp.zeros_like(qb))
    return jnp.concatenate([qa, qb], axis=1)


def _pair_out(o0, o1):
    return jnp.where(_lane_iota() < NA_HEAD_DIM, o0, o1)


def _softmax_pv(s_list, v_list):
    m = s_list[0].max(axis=-1, keepdims=True)
    for s in s_list[1:]:
        m = jnp.maximum(m, s.max(axis=-1, keepdims=True))
    l = None
    o = None
    for s, v in zip(s_list, v_list):
        e = jnp.exp(s - m)
        ls = e.sum(axis=-1, keepdims=True)
        pv = _dot(e.astype(BF16), v)
        l = ls if l is None else l + ls
        o = pv if o is None else o + pv
    return o / l


def _ctx_attn_kernel(q_ref, k_ref, v_ref, qn_ref, qr_ref, kn_ref, kr_ref, mv_ref, ona_ref, omla_ref):
    for p in range(NA_HEADS // 2):
        sl = slice(p * LANES, (p + 1) * LANES)
        kp = k_ref[:, sl]
        vp = v_ref[:, sl]
        outs = []
        for half in range(2):
            s = _dot_nt(_head_q(q_ref[:, sl], half), kp)
            outs.append(_softmax_pv([s], [vp]))
        ona_ref[:, sl] = _pair_out(*outs)

        kk = jnp.concatenate([kn_ref[:, sl], kr_ref[...]], axis=1)
        mvp = mv_ref[:, sl]
        outs = []
        for half in range(2):
            s = _dot_nt(_mla_q(qn_ref, qr_ref, 2 * p + half), kk) * MLA_SCALE
            outs.append(_softmax_pv([s], [mvp]))
        omla_ref[:, sl] = _pair_out(*outs)


def _ctx_attn(q, k, v, qn, qr, kn, kr, mv, *, seq):
    n = q.shape[0]
    tok = lambda w: pl.BlockSpec((seq, w), lambda b: (b, 0))
    return pl.pallas_call(
        _ctx_attn_kernel,
        grid=(n // seq,),
        in_specs=[tok(512), tok(512), tok(512), tok(512), tok(256), tok(512), tok(LANES), tok(512)],
        out_specs=[tok(512), tok(512)],
        out_shape=[jax.ShapeDtypeStruct((n, 512), F32)] * 2,
        compiler_params=_params(("arbitrary",)),
        name="ctx_attn",
    )(q, k, v, qn, qr, kn, kr, mv)


NA_QROWS = 4
NA_KROWS = 12
_NDR = 2 * NA_ROWS - 1
_NDC = 2 * NA_COLS - 1
_Z_NONE = 48
LOG2E = 1.4426950408889634


def _na_tile_descriptors(rows):
    last_q0 = rows - NA_QROWS
    last_k0 = rows - NA_KROWS
    desc = np.full((3, NA_KROWS, NA_QROWS // 2), _Z_NONE, np.int32)
    for case, (q0, k0) in enumerate(((0, 0), (NA_QROWS, 0), (last_q0, last_k0))):
        for kri in range(NA_KROWS):
            for u in range(NA_QROWS // 2):
                info = []
                for ri in (2 * u, 2 * u + 1):
                    qrow, krow = q0 + ri, k0 + kri
                    start = min(max(qrow - NA_ROWS // 2, 0), rows - NA_ROWS)
                    info.append((start <= krow < start + NA_ROWS, krow - qrow))
                (vl, drl), (vr, _) = info
                d = drl + NA_ROWS - 1
                if vl and vr:
                    desc[case, kri, u] = d
                elif vl:
                    desc[case, kri, u] = 16 + d
                elif vr:
                    desc[case, kri, u] = 32 + d
    return desc.reshape(-1)


def _na_bias_consts():
    col = np.arange(GRID_W)
    col_start = np.clip(col - NA_COLS // 2, 0, GRID_W - NA_COLS)
    kc, qc = col[:, None], col[None, :]
    col_in = (kc >= col_start[None, :]) & (kc < col_start[None, :] + NA_COLS)
    negm = np.where(col_in, 0.0, NEG).astype(np.float32)
    return np.concatenate([negm, negm], axis=1)


def _na_build_bias(bias_ref, negm_ref, z_ref):
    lane = _lane_iota()
    left = lane < GRID_W
    in_window = negm_ref[...] == 0.0

    def coef(h, dr, j):
        if dr < -(NA_ROWS - 1) or dr > NA_ROWS - 1:
            return 0.0
        return bias_ref[(h * _NDR + dr + NA_ROWS - 1) * _NDC + j]

    def per_head(h, c):
        for d in range(16):
            dr_l = d - (NA_ROWS - 1)
            base = jnp.zeros((1, LANES), F32)
            for j in range(_NDC):
                c_left = (NA_COLS - 1 - j) % LANES
                c_right = GRID_W + NA_COLS - 1 - j
                base = jnp.where(lane == c_left, coef(h, dr_l, j), base)
                base = jnp.where(lane == c_right, coef(h, dr_l - 1, j), base)
            rolled = pltpu.roll(jnp.broadcast_to(base, (GRID_W, LANES)), 0, axis=1, stride=1, stride_axis=0)
            acc = jnp.where(in_window, rolled, NEG)
            z_ref[h, d] = acc
            z_ref[h, 16 + d] = jnp.where(left, acc, NEG)
            z_ref[h, 32 + d] = jnp.where(left, NEG, acc)
        z_ref[h, _Z_NONE] = jnp.full((GRID_W, LANES), NEG, F32)
        return c

    lax.fori_loop(0, NA_HEADS, per_head, 0)


def _finish_pair(acc_a, acc_b):
    norm = lambda acc: acc[:HEAD_DIM] * (1.0 / acc[HEAD_DIM:HEAD_DIM + 1])
    return jnp.concatenate([norm(acc_a), norm(acc_b)], axis=0).T


def _na_lat_kernel(bias_ref, desc_ref, q_ref, k_ref, vt_ref, kc_ref, vtc_ref, negm_ref, o_ref, z_ref,
                   *, rows):
    g = pl.program_id(1)
    groups = rows // NA_QROWS

    @pl.when((pl.program_id(0) == 0) & (g == 0))
    def _():
        _na_build_bias(bias_ref, negm_ref, z_ref)

    tq = NA_QROWS * GRID_W
    k_row0 = jnp.clip(g * NA_QROWS - NA_ROWS // 2, 0, rows - NA_KROWS)
    start = pl.multiple_of(k_row0 * GRID_W, KV_CHUNK)
    chunk0 = k_row0 // (KV_CHUNK // GRID_W)
    n_loc = NA_KROWS * GRID_W
    case = jnp.where(g == 0, 0, jnp.where(g == groups - 1, 2, 1))
    upairs = NA_QROWS // 2

    def scores(h):
        sl = slice((h // 2) * LANES, (h // 2 + 1) * LANES)
        qt = _head_q(q_ref[:, sl], h % 2).astype(F32).T.astype(BF16)
        return _dot(k_ref[pl.ds(start, n_loc), sl], qt), _dot(kc_ref[0, :, sl], qt)

    def attend(h, s_loc, s_ctx):
        rows_h = slice(h * VT_ROWS, (h + 1) * VT_ROWS)
        blocks = []
        for kri in range(NA_KROWS):
            tiles = []
            for u in range(upairs):
                idx = desc_ref[(case * NA_KROWS + kri) * upairs + u]
                tiles.append(s_loc[kri * GRID_W:(kri + 1) * GRID_W, u * LANES:(u + 1) * LANES] + z_ref[h, idx])
            blocks.append(jnp.concatenate(tiles, axis=1))
        s_loc = jnp.concatenate(blocks, axis=0)
        m = jnp.maximum(s_loc.max(axis=0, keepdims=True), s_ctx.max(axis=0, keepdims=True))
        p_loc = jnp.exp2((s_loc - m) * LOG2E).astype(BF16)
        p_ctx = jnp.exp2((s_ctx - m) * LOG2E).astype(BF16)
        acc = _dot(vtc_ref[0, rows_h, :], p_ctx)
        for i in range(n_loc // KV_CHUNK):
            acc = acc + _dot(vt_ref[chunk0 + i, rows_h, :], p_loc[i * KV_CHUNK:(i + 1) * KV_CHUNK])
        return acc

    s_next = scores(0)
    accs = []
    for h in range(NA_HEADS):
        s = s_next
        if h + 1 < NA_HEADS:
            s_next = scores(h + 1)
        accs.append(attend(h, *s))
        if h % 2:
            o_ref[:, (h // 2) * LANES:(h // 2 + 1) * LANES] = _finish_pair(accs[h - 1], accs[h])


def _na_lat(q, k, vt, kc, vtc, bias_l, *, batch, seq):
    rows = seq // GRID_W
    n = q.shape[0]
    groups = rows // NA_QROWS
    tq = NA_QROWS * GRID_W
    chunks = seq // KV_CHUNK
    negm = _na_bias_consts()
    desc = jnp.asarray(_na_tile_descriptors(rows))
    smem = pl.BlockSpec(memory_space=pltpu.SMEM)
    full = lambda a: pl.BlockSpec(a.shape, lambda b, g: (0,) * a.ndim)
    return pl.pallas_call(
        functools.partial(_na_lat_kernel, rows=rows),
        grid=(batch, groups),
        in_specs=[
            smem, smem,
            pl.BlockSpec((tq, 512), lambda b, g: (b * groups + g, 0)),
            pl.BlockSpec((seq, 512), lambda b, g: (b, 0)),
            pl.BlockSpec((chunks, VT_ALL, KV_CHUNK), lambda b, g: (b, 0, 0)),
            pl.BlockSpec((1,) + kc.shape[1:], lambda b, g: (b, 0, 0)),
            pl.BlockSpec((1,) + vtc.shape[1:], lambda b, g: (b, 0, 0)),
            full(negm),
        ],
        out_specs=pl.BlockSpec((tq, 512), lambda b, g: (b * groups + g, 0)),
        out_shape=jax.ShapeDtypeStruct((n, 512), F32),
        scratch_shapes=[pltpu.VMEM((NA_HEADS, _Z_NONE + 1, GRID_W, LANES), F32)],
        compiler_params=_params(("arbitrary", "arbitrary")),
        name="na_lat",
    )(bias_l.reshape(-1), desc, q, k, vt, kc, vtc, jnp.asarray(negm))


_MLA_C = MLA_SCALE * LOG2E


def _mla_lat_kernel(qn_ref, qr_ref, kn_ref, kr_ref, vt_ref, knc_ref, krc_ref, vtc_ref, o_ref,
                    q2_ref, m_ref, acc_ref, sa_ref, sb_ref, *, tk):
    tq = qn_ref.shape[0]
    n_tiles = kn_ref.shape[0] // tk
    cpt = tk // KV_CHUNK
    pairs = MLA_HEADS // 2
    for p in range(pairs):
        qt = [_mla_q(qn_ref, qr_ref, 2 * p + half).astype(F32).T.astype(BF16) for half in range(2)]
        q2_ref[p] = jnp.concatenate(qt, axis=1)
        m_ref[p] = jnp.full((1, 2 * tq), NEG, F32)
        acc_ref[p] = jnp.zeros((2, VT_ROWS, tq), F32)

    def scores(p, kk):
        return _dot(kk, q2_ref[p])

    def update(p, s, vts):
        m_prev = m_ref[p]
        m_new = jnp.maximum(m_prev, s.max(axis=0, keepdims=True))
        alpha = jnp.exp2((m_prev - m_new) * _MLA_C)
        pt = jnp.exp2((s - m_new) * _MLA_C).astype(BF16)
        accs = []
        for half in range(2):
            cols = slice(half * tq, (half + 1) * tq)
            acc = acc_ref[p, half] * alpha[:, cols]
            for i, vt in enumerate(vts):
                acc = acc + _dot(vt[half * VT_ROWS:(half + 1) * VT_ROWS], pt[i * KV_CHUNK:(i + 1) * KV_CHUNK, cols])
            accs.append(acc)
        return m_new, accs

    def store(new):
        for p, (m_new, accs) in enumerate(new):
            acc_ref[p, 0] = accs[0]
            acc_ref[p, 1] = accs[1]
            m_ref[p] = m_new

    def lanes(p):
        return slice(p * LANES, (p + 1) * LANES)

    def rows(p):
        return slice(2 * p * VT_ROWS, 2 * (p + 1) * VT_ROWS)

    def keys(t, p):
        start = t * tk if isinstance(t, int) else pl.multiple_of(t * tk, tk)
        ks = pl.ds(start, tk)
        return jnp.concatenate([kn_ref[ks, lanes(p)], kr_ref[ks, :]], axis=1)

    for p in range(pairs):
        sa_ref[p] = scores(p, keys(0, p))

    n_ctx = knc_ref.shape[1]

    def latent_next(t):
        def put(nxt_ref, p):
            nxt_ref[p] = scores(p, keys(t + 1, p))
        return put

    def ctx_next(nxt_ref, p):
        kk = jnp.concatenate([knc_ref[0, :, lanes(p)], krc_ref[0]], axis=1)
        nxt_ref[p, :n_ctx] = scores(p, kk)

    def tile_step(t, cur_ref, nxt_ref, put_next):
        ahead = 1
        for p in range(ahead):
            put_next(nxt_ref, p)
        new = []
        for p in range(pairs):
            new.append(update(p, cur_ref[p], [vt_ref[t * cpt + i, rows(p), :] for i in range(cpt)]))
            if p + ahead < pairs:
                put_next(nxt_ref, p + ahead)
        store(new)

    def body(i, c):
        tile_step(2 * i, sa_ref, sb_ref, latent_next(2 * i))
        tile_step(2 * i + 1, sb_ref, sa_ref, latent_next(2 * i + 1))
        return c

    lax.fori_loop(0, n_tiles // 2 - 1, body, 0)
    tile_step(n_tiles - 2, sa_ref, sb_ref, latent_next(n_tiles - 2))
    tile_step(n_tiles - 1, sb_ref, sa_ref, ctx_next)
    for p in range(pairs):
        _, accs = update(p, sa_ref[p, :n_ctx], [vtc_ref[0, rows(p), :]])
        o_ref[:, lanes(p)] = _finish_pair(*accs)


def _mla_lat(qn, qr, kn, kr, vt, knc, krc, vtc, *, batch, seq, tq, tk):
    n = qn.shape[0]
    steps = seq // tq
    chunks = seq // KV_CHUNK
    assert knc.shape[1] == KV_CHUNK
    qtok = lambda w: pl.BlockSpec((tq, w), lambda b, i: (b * steps + i, 0))
    ktok = lambda w: pl.BlockSpec((seq, w), lambda b, i: (b, 0))
    ctok = lambda a: pl.BlockSpec((1,) + a.shape[1:], lambda b, i: (b, 0, 0))
    pairs = MLA_HEADS // 2
    return pl.pallas_call(
        functools.partial(_mla_lat_kernel, tk=tk),
        grid=(batch, steps),
        in_specs=[qtok(512), qtok(256), ktok(512), ktok(LANES),
                  pl.BlockSpec((chunks, VT_ALL, KV_CHUNK), lambda b, i: (b, 0, 0)),
                  ctok(knc), ctok(krc), ctok(vtc)],
        out_specs=qtok(512),
        out_shape=jax.ShapeDtypeStruct((n, 512), F32),
        scratch_shapes=[pltpu.VMEM((pairs, 2 * LANES, 2 * tq), BF16),
                        pltpu.VMEM((pairs, 1, 2 * tq), F32),
                        pltpu.VMEM((pairs, 2, VT_ROWS, tq), F32),
                        pltpu.VMEM((pairs, tk, 2 * tq), F32),
                        pltpu.VMEM((pairs, tk, 2 * tq), F32)],
        compiler_params=_params(("arbitrary", "arbitrary")),
        name="mla_lat",
    )(qn, qr, kn, kr, vt, knc, krc, vtc)


def _cache_kv_kernel(ckv_ref, w_ref, wvt_ref, kn_ref, vt_ref):
    ckv = ckv_ref[0, 0].astype(BF16)
    kn_ref[0, 0] = _dot(ckv, w_ref[0]).astype(BF16)
    _store_vt(vt_ref.at[0], _dot_nt(wvt_ref[0], ckv))


def _cache_kv(ckv, wukv_k, wukv_vt):
    _, nb, rows, _ = ckv.shape
    assert rows == KV_CHUNK
    spec = lambda a, b: pl.BlockSpec((1, a, b), lambda l, j: (l, 0, 0))
    return pl.pallas_call(
        _cache_kv_kernel,
        grid=(DEPTH, nb),
        in_specs=[pl.BlockSpec((1, 1, rows, KV_LORA), lambda l, j: (l, j, 0, 0)),
                  spec(KV_LORA, 512), spec(512, KV_LORA)],
        out_specs=[pl.BlockSpec((1, 1, rows, 512), lambda l, j: (l, j, 0, 0)),
                   pl.BlockSpec((1, 1, VT_ALL, KV_CHUNK), lambda l, j: (l, j, 0, 0))],
        out_shape=[jax.ShapeDtypeStruct((DEPTH, nb, rows, 512), BF16),
                   jax.ShapeDtypeStruct((DEPTH, nb, VT_ALL, KV_CHUNK), BF16)],
        compiler_params=_params(("arbitrary", "arbitrary")),
        name="cache_kv",
    )(ckv, wukv_k, wukv_vt)


def _dft_mats(n, scale):
    k = np.arange(n)
    ang = 2.0 * np.pi * ((k[:, None] * k[None, :]) % n) / n
    return np.cos(ang) * scale, np.sin(ang) * scale


def _hi_lo_np(m):
    m = jnp.asarray(m, F32)
    hi = m.astype(BF16)
    lo = (m - hi.astype(F32)).astype(BF16)
    return hi, lo


def _chan_dft(u, wc_ref):
    uh, ul = _split(u)
    wh = wc_ref[0]
    wl = wc_ref[1]
    return _dot(uh, wh) + _dot(ul, wh) + _dot(uh, wl)


def _fn_ctx_kernel(u_ref, wc_ref, fp_ref, o_ref):
    his, los = [], []
    for g in range(FN_GROUPS):
        sl = slice(g * FN_GROUP_W, (g + 1) * FN_GROUP_W)
        a = _chan_dft(u_ref[:, sl], wc_ref)
        ar_h, ar_l = _split(a[:, :FN_GROUP_W])
        ai_h, ai_l = _split(a[:, FN_GROUP_W:])
        his.append(jnp.concatenate([ar_h, ai_h], axis=0))
        los.append(jnp.concatenate([ar_l, ai_l], axis=0))
    rh = jnp.concatenate(his, axis=1)
    rl = jnp.concatenate(los, axis=1)
    o_ref[...] = _dot(fp_ref[0], rh) + _dot(fp_ref[0], rl) + _dot(fp_ref[1], rh)


def _fn_ctx(u, *, seq):
    n = u.shape[0]
    cc, sc = _dft_mats(FN_GROUP_W, FN_GROUP_W ** -0.5)
    wc = jnp.stack(_hi_lo_np(np.concatenate([cc, -sc], axis=1)))
    cp, sp = _dft_mats(seq, seq ** -0.5)
    fp = jnp.stack(_hi_lo_np(np.concatenate([cp, sp], axis=1)))
    full = lambda a: pl.BlockSpec(a.shape, lambda b: (0,) * a.ndim)
    return pl.pallas_call(
        _fn_ctx_kernel,
        grid=(n // seq,),
        in_specs=[pl.BlockSpec((seq, FN_WIDTH), lambda b: (b, 0)), full(wc), full(fp)],
        out_specs=pl.BlockSpec((seq, FN_WIDTH), lambda b: (b, 0)),
        out_shape=jax.ShapeDtypeStruct((n, FN_WIDTH), F32),
        compiler_params=_params(("arbitrary",)),
        name="fn_ctx",
    )(u, wc, fp)


_FN_G = 16


def _fn_lat_kernel(u_ref, wc_ref, f1_ref, f2_ref, twc_ref, tws_ref, o_ref, a_ref, t_ref):
    n = GRID_W
    w = FN_GROUP_W

    def chan(i, c):
        rows = pl.ds(pl.multiple_of(i * 512, 512), 512)
        a = _chan_dft(u_ref[rows, :], wc_ref)
        a_ref[0, rows, :] = a[:, :FN_GROUP_W]
        a_ref[1, rows, :] = a[:, FN_GROUP_W:]
        return c

    lax.fori_loop(0, u_ref.shape[0] // 512, chan, 0)

    def gather(ref, base):
        cols = [jnp.concatenate([ref[0, pl.ds(base + j, n, stride=n), :],
                                 ref[1, pl.ds(base + j, n, stride=n), :]], axis=0) for j in range(_FN_G)]
        return jnp.concatenate(cols, axis=1)

    def dft(f_ref, d):
        rows = f_ref.shape[0] // 2
        dh, dl = _split(d)
        rh = _dot(f_ref[...], dh)
        return rh[:rows] + rh[rows:] + _dot(f_ref[:rows, :], dl)

    def stage1(i, c):
        base = i * _FN_G
        b = dft(f1_ref, gather(a_ref, base))
        for j in range(_FN_G):
            br = b[:n, j * w:(j + 1) * w]
            bi = b[n:, j * w:(j + 1) * w]
            rows = pl.ds(pl.multiple_of((base + j) * n, n), n)
            tc = twc_ref[rows, :]
            ts = tws_ref[rows, :]
            t_ref[0, rows, :] = br * tc + bi * ts
            t_ref[1, rows, :] = bi * tc - br * ts
        return c

    lax.fori_loop(0, n // _FN_G, stage1, 0)

    def stage2(i, c):
        base = i * _FN_G
        y = dft(f2_ref, gather(t_ref, base))
        for j in range(_FN_G):
            o_ref[pl.ds(base + j, n, stride=n), :] = y[:, j * w:(j + 1) * w]
        return c

    lax.fori_loop(0, n // _FN_G, stage2, 0)


def _fn_lat(u, *, batch, seq):
    n = GRID_W
    assert seq == n * n
    cc, sc = _dft_mats(FN_GROUP_W, FN_GROUP_W ** -0.5)
    wc = jnp.stack(_hi_lo_np(np.concatenate([cc, -sc], axis=1)))
    c1, s1 = _dft_mats(n, 1.0)
    c2, s2 = _dft_mats(n, 1.0 / n)
    f1 = jnp.concatenate(_hi_lo_np(np.block([[c1, s1], [-s1, c1]])), axis=0)
    f2 = jnp.concatenate(_hi_lo_np(np.concatenate([c2, s2], axis=1)), axis=0)
    n2 = np.arange(n)[:, None]
    k1 = np.arange(n)[None, :]
    ang = (2.0 * np.pi * ((n2 * k1) % seq) / seq).reshape(seq, 1)
    twc = jnp.asarray(np.broadcast_to(np.cos(ang), (seq, FN_GROUP_W)), F32)
    tws = jnp.asarray(np.broadcast_to(np.sin(ang), (seq, FN_GROUP_W)), F32)
    full = lambda a: pl.BlockSpec(a.shape, lambda b, g: (0,) * a.ndim)
    blk = pl.BlockSpec((seq, FN_GROUP_W), lambda b, g: (b, g))
    return pl.pallas_call(
        _fn_lat_kernel,
        grid=(batch, FN_GROUPS),
        in_specs=[blk, full(wc), full(f1), full(f2), full(twc), full(tws)],
        out_specs=blk,
        out_shape=jax.ShapeDtypeStruct(u.shape, F32),
        scratch_shapes=[pltpu.VMEM((2, seq, FN_GROUP_W), F32), pltpu.VMEM((2, seq, FN_GROUP_W), F32)],
        compiler_params=_params(("arbitrary", "arbitrary")),
        name="fn_lat",
    )(u, wc, f1, f2, twc, tws)


_E_GNA, _E_GMLA, _E_GFN, _E_MERGE = 0, 512, 1024, 1536


def _merge_kernel(*refs, final):
    (x_ref, mod_ref, ng_ref, wg_ref, ona_ref, omla_ref, ofn_ref, wona_ref, womla_ref, wofn_ref,
     wout_ref) = refs[:11]
    if final:
        fg_ref, o_ref = refs[11:13]
    else:
        o_ref = refs[11]
    x, xm = _modulated(x_ref, mod_ref, ng_ref)
    gate = mod_ref[0, 2:3, :]

    merged = None
    for i, (br_ref, wo_ref) in enumerate(((ona_ref, wona_ref), (omla_ref, womla_ref), (ofn_ref, wofn_ref))):
        g = _dot_nt(xm, wg_ref[0, i * 512:(i + 1) * 512, :])
        t = _dot((br_ref[...] * _silu(g)).astype(BF16), wo_ref[0])
        ml = _dot_nt(xm, wg_ref[0, _E_MERGE + i * D_MODEL:_E_MERGE + (i + 1) * D_MODEL, :])
        t = _sigmoid(ml) * t
        merged = t if merged is None else merged + t
    h = x + gate * _dot(merged.astype(BF16), wout_ref[0])
    if final:
        h = _rms(h, fg_ref[...])
    o_ref[...] = h


def _layer_spec(a, layer):
    return pl.BlockSpec((1,) + a.shape[1:], lambda *_: (layer,) + (0,) * (a.ndim - 1))


def _merge(x, mod, ng, w_g, ona, omla, ofn, wona, womla, wofn, wout, fg, *, tile, layer):
    n = x.shape[0]
    groups = mod.shape[0]
    steps = n // tile
    per_group = steps // groups
    final = fg is not None
    tok = lambda w: pl.BlockSpec((tile, w), lambda i: (i, 0))
    full = lambda a: pl.BlockSpec(a.shape, lambda i: (0,) * a.ndim)
    lay = lambda a: _layer_spec(a, layer)
    in_specs = [tok(D_MODEL), pl.BlockSpec((1, 3, D_MODEL), lambda i: (i // per_group, 0, 0)),
                full(ng), lay(w_g), tok(512), tok(512), tok(512),
                lay(wona), lay(womla), lay(wofn), lay(wout)]
    args = [x, mod, ng, w_g, ona, omla, ofn, wona, womla, wofn, wout]
    if final:
        in_specs.append(full(fg))
        args.append(fg)
    return pl.pallas_call(
        functools.partial(_merge_kernel, final=final),
        grid=(steps,),
        in_specs=in_specs,
        out_specs=tok(D_MODEL),
        out_shape=jax.ShapeDtypeStruct((n, D_MODEL), F32),
        compiler_params=_params(("arbitrary",)),
        name="merge",
    )(*args)


def _rope_rot(w, axis):
    q = MLA_ROPE // 4
    part = lambda i: lax.slice_in_dim(w, i * q, (i + 1) * q, axis=axis)
    return jnp.concatenate([-part(1), part(0), -part(3), part(2)], axis=axis)


def _prep_w_in(w_in):
    wt = jnp.swapaxes(w_in, 1, 2)
    rows = lambda start, n: wt[:, start:start + n]
    kr = rows(_O_KROPE, MLA_ROPE)
    kr_rot = _rope_rot(kr, 1)
    a = jnp.concatenate([rows(_O_QKV, 1536), rows(_O_QLAT, Q_LORA + KV_LORA),
                         kr, kr_rot, kr, kr_rot, rows(_O_UFN, FN_WIDTH)], axis=1)
    e = jnp.concatenate([rows(_O_GATE_NA, 512), rows(_O_GATE_MLA, 512), rows(_O_GATE_FN, 512),
                         rows(_O_MERGE, 3 * D_MODEL)], axis=1)
    return a.astype(BF16), e.astype(BF16)


def _prep_w_uq(w):
    w3 = w.reshape(Q_LORA, MLA_HEADS, MLA_NOPE + MLA_ROPE)
    nope = w3[:, :, :MLA_NOPE].reshape(Q_LORA, MLA_HEADS * MLA_NOPE)
    rope = w3[:, :, MLA_NOPE:]
    rot = _rope_rot(rope, 2)
    return jnp.concatenate([nope, rope.reshape(Q_LORA, -1), rot.reshape(Q_LORA, -1)], axis=1).astype(BF16)


def _prep_w_ukv(w):
    w3 = w.reshape(KV_LORA, MLA_HEADS, MLA_NOPE + MLA_V)
    return jnp.concatenate([w3[:, :, :MLA_NOPE].reshape(KV_LORA, -1),
                            w3[:, :, MLA_NOPE:].reshape(KV_LORA, -1)], axis=1).astype(BF16)


def _rope_tables(n):
    t = jnp.arange(n, dtype=jnp.int32)
    row = (t // GRID_W).astype(F32)
    col = (t % GRID_W).astype(F32)
    half = MLA_ROPE // 2
    inv_freq = ROPE_THETA ** (-jnp.arange(0, half, 2, dtype=F32) / half)
    ar = row[:, None] * inv_freq[None, :]
    ac = col[:, None] * inv_freq[None, :]
    ang = jnp.concatenate([ar, ar, ac, ac], axis=-1)
    return jnp.tile(jnp.cos(ang), (1, MLA_HEADS)), jnp.tile(jnp.sin(ang), (1, MLA_HEADS))


def kernel(x_prompt, x_sample, cache_na_k, cache_na_v, cache_mla_ckv, cache_mla_krope, c, c_ctx,
           w_ada, b_ada, norm_g, w_in, q_norm_g, kv_norm_g, w_uq, w_ukv, na_bias,
           w_o_na, w_o_mla, w_o_fourier, w_out, final_norm_g):
    batch, seq, _ = x_prompt.shape
    dbatch, dseq, _ = x_sample.shape
    past = cache_na_k.shape[2]

    cond = jnp.zeros((SUBLANES, D_MODEL), F32).at[0].set(c_ctx).at[1:1 + dbatch].set(c)
    mods = _ada_mods(cond, w_ada, b_ada).reshape(DEPTH, SUBLANES, 3, D_MODEL)

    wukv_all = jnp.stack([_prep_w_ukv(w_ukv[l]) for l in range(DEPTH)])
    wukv_vt_all = wukv_all[:, :, 512:].transpose(0, 2, 1)
    knc_all, mvtc_all = _cache_kv(cache_mla_ckv.transpose(1, 0, 2, 3), wukv_all[:, :, :512], wukv_vt_all)
    krc_all = jnp.tile(cache_mla_krope, (1, 1, 1, 4)).astype(BF16)
    nakc_all = cache_na_k.reshape(dbatch, DEPTH, past, NA_WIDTH).astype(BF16)
    navt = cache_na_v.transpose(0, 1, 3, 4, 2).astype(BF16)
    navtc_all = jnp.concatenate([navt, jnp.ones((dbatch, DEPTH, NA_HEADS, ONES_ROWS, past), BF16)],
                                axis=3).reshape(dbatch, DEPTH, VT_ALL, past)
    cos, sin = _rope_tables(dseq)

    h_ctx = x_prompt.reshape(batch * seq, D_MODEL)
    h_lat = x_sample.reshape(dbatch * dseq, D_MODEL)
    ks, vs, ckvs, krs = [], [], [], []
    row = lambda a: a.reshape(1, -1)
    w_a, w_e = _prep_w_in(w_in)
    wona, womla, wofn, wout = (w_o_na.astype(BF16), w_o_mla.astype(BF16),
                               w_o_fourier.astype(BF16), w_out.astype(BF16))
    for l in range(DEPTH):
        wuq = _prep_w_uq(w_uq[l])
        wukv = wukv_all[l]
        ng, qng, kvng = row(norm_g[l]), row(q_norm_g[l]), row(kv_norm_g[l])
        fg = row(final_norm_g) if l == DEPTH - 1 else None
        mod_ctx = mods[l, 0:1]
        mod_lat = mods[l, 1:1 + dbatch]

        (q, k, v, qn, qr, kn, kr, mv, u, kf, vf, ckvf, krf) = _inproj(
            h_ctx, mod_ctx, ng, w_a, qng, kvng, wuq, wukv, None, tile=TOKEN_TILE, layer=l, seq=seq)
        ks.append(kf)
        vs.append(vf)
        ckvs.append(ckvf)
        krs.append(krf)
        ona, omla = _ctx_attn(q, k, v, qn, qr, kn, kr, mv, seq=seq)
        ofn = _fn_ctx(u, seq=seq)
        h_ctx = _merge(h_ctx, mod_ctx, ng, w_e, ona, omla, ofn, wona, womla, wofn, wout, fg, tile=TOKEN_TILE, layer=l)

        (q, k, vt, qn, qr, kn, kr, mvt, u) = _inproj(
            h_lat, mod_lat, ng, w_a, qng, kvng, wuq, wukv, (wukv_vt_all[l], cos, sin), tile=TOKEN_TILE, layer=l)
        ona = _na_lat(q, k, vt, nakc_all[:, l], navtc_all[:, l], na_bias[l], batch=dbatch, seq=dseq)
        omla = _mla_lat(qn, qr, kn, kr, mvt, knc_all[l], krc_all[:, l], mvtc_all[l],
                        batch=dbatch, seq=dseq, tq=MLA_TQ, tk=MLA_TK)
        ofn = _fn_lat(u, batch=dbatch, seq=dseq)
        h_lat = _merge(h_lat, mod_lat, ng, w_e, ona, omla, ofn, wona, womla, wofn, wout, fg, tile=TOKEN_TILE, layer=l)

    y_prompt = h_ctx.reshape(batch, seq, D_MODEL)
    y_sample = h_lat.reshape(dbatch, dseq, D_MODEL)
    heads_t = lambda xs: jnp.stack(xs, axis=1).reshape(
        batch, DEPTH, NA_HEADS, NA_HEAD_DIM, seq).transpose(0, 1, 4, 2, 3)
    new_na_k = heads_t(ks)
    new_na_v = heads_t(vs)
    new_mla_ckv = jnp.stack([a.reshape(batch, seq, KV_LORA) for a in ckvs], axis=1)
    new_mla_krope = jnp.stack(krs, axis=1).transpose(0, 1, 3, 2)
    return (y_prompt, y_sample, new_na_k, new_na_v, new_mla_ckv, new_mla_krope)
```

```python
import functools

import jax
import jax.numpy as jnp
import numpy as np
from jax import lax
from jax.experimental import pallas as pl
from jax.experimental.pallas import tpu as pltpu

F32 = jnp.float32
BF16 = jnp.bfloat16

D_MODEL = 1024
DEPTH = 4
GRID_W = 64
NA_HEADS = 8
NA_HEAD_DIM = 64
NA_WIDTH = 512
NA_ROWS = 8
NA_COLS = 16
MLA_HEADS = 8
MLA_NOPE = 64
MLA_ROPE = 32
MLA_V = 64
MLA_WIDTH = 512
Q_LORA = 256
KV_LORA = 128
MLA_SCALE = (MLA_NOPE + MLA_ROPE) ** -0.5
ROPE_THETA = 10000.0
FN_GROUPS = 4
FN_GROUP_W = 128
FN_WIDTH = 512
EPS = 1e-6
NEG = -1e30

_O_QKV = 0
_O_GATE_NA = 1536
_O_QLAT = 2048
_O_CKV = 2304
_O_KROPE = 2432
_O_GATE_MLA = 2464
_O_UFN = 2976
_O_GATE_FN = 3488
_O_MERGE = 4000
D_IN = 7072

LANES = 128
SUBLANES = 8
VMEM_LIMIT = 56 * 1024 * 1024

TOKEN_TILE = 512
INPROJ_TILE = 1024
MLA_TQ = 256
MLA_TK = 512


def _params(sem):
    return pltpu.CompilerParams(dimension_semantics=sem, vmem_limit_bytes=VMEM_LIMIT)


def _sigmoid(x):
    return 1.0 / (1.0 + jnp.exp(-x))


def _silu(x):
    return x * _sigmoid(x)


def _rms(x, g):
    return x * lax.rsqrt(jnp.mean(x * x, axis=-1, keepdims=True) + EPS) * g


def _dot(a, b):
    return jnp.dot(a, b, preferred_element_type=F32)


def _dot_nt(a, b):
    return lax.dot_general(a, b, (((1,), (1,)), ((), ())), preferred_element_type=F32)


def _split(x):
    hi = x.astype(BF16)
    lo = (x - hi.astype(F32)).astype(BF16)
    return hi, lo


def _ada_kernel(cond_ref, w_ref, b_ref, o_ref):
    a = _silu(cond_ref[...]).astype(BF16)
    o_ref[0] = _dot(a, w_ref[0].astype(BF16)) + b_ref[0]


def _ada_mods(cond, w_ada, b_ada):
    nb = 3
    return pl.pallas_call(
        _ada_kernel,
        grid=(DEPTH, nb),
        in_specs=[
            pl.BlockSpec((SUBLANES, D_MODEL), lambda l, j: (0, 0)),
            pl.BlockSpec((1, D_MODEL, D_MODEL), lambda l, j: (l, 0, j)),
            pl.BlockSpec((1, 1, D_MODEL), lambda l, j: (l, 0, j)),
        ],
        out_specs=pl.BlockSpec((1, SUBLANES, D_MODEL), lambda l, j: (l, 0, j)),
        out_shape=jax.ShapeDtypeStruct((DEPTH, SUBLANES, 3 * D_MODEL), F32),
        compiler_params=_params(("arbitrary", "arbitrary")),
        name="ada_mods",
    )(cond, w_ada, b_ada.reshape(DEPTH, 1, 3 * D_MODEL))


_A_Q, _A_K, _A_V, _A_QLAT, _A_CKV, _A_KR, _A_U, _A_END = (
    0, 512, 1024, 1536, 1792, 1920, 2048, 2560)


def _modulated(x_ref, mod_ref, ng_ref):
    x = x_ref[...]
    shift = mod_ref[0, 0:1, :]
    scale = mod_ref[0, 1:2, :]
    return x, (_rms(x, ng_ref[...]) * (1.0 + scale) + shift).astype(BF16)


KV_CHUNK = 256
HEAD_DIM = 64
ONES_ROWS = 16
VT_ROWS = HEAD_DIM + ONES_ROWS
VT_ALL = 8 * VT_ROWS


def _store_vt(vt_ref, vt):
    for c in range(vt_ref.shape[0]):
        cols = slice(c * KV_CHUNK, (c + 1) * KV_CHUNK)
        for h in range(8):
            vt_ref[c, h * VT_ROWS:h * VT_ROWS + HEAD_DIM, :] = vt[h * HEAD_DIM:(h + 1) * HEAD_DIM, cols].astype(BF16)
            vt_ref[c, h * VT_ROWS + HEAD_DIM:(h + 1) * VT_ROWS, :] = jnp.ones((ONES_ROWS, KV_CHUNK), BF16)


def _store_seq_t(ref, xt):
    seq = ref.shape[2]
    for j in range(ref.shape[0]):
        ref[j] = xt[:, j * seq:(j + 1) * seq]


def _inproj_kernel(*refs, latent):
    (x_ref, mod_ref, ng_ref, w_ref, qng_ref, kvng_ref, wuq_ref, wukv_ref) = refs[:8]
    if latent:
        (wmvt_ref, cos_ref, sin_ref,
         naq_ref, nak_ref, navt_ref, qn_ref, qr_ref, kn_ref, kr_ref, mvt_ref, u_ref) = refs[8:]
    else:
        (naq_ref, nak_ref, nav_ref, qn_ref, qr_ref, kn_ref, kr_ref, mv_ref, u_ref,
         kf_ref, vf_ref, ckvf_ref, krf_ref) = refs[8:]

    _, xm = _modulated(x_ref, mod_ref, ng_ref)

    def proj(a, b):
        return _dot_nt(xm, w_ref[0, a:b, :])

    q = proj(_A_Q, _A_K)
    naq_ref[...] = (q * (NA_HEAD_DIM ** -0.5)).astype(BF16)
    k = proj(_A_K, _A_V)
    nak_ref[...] = k.astype(BF16)
    if latent:
        _store_vt(navt_ref, _dot_nt(w_ref[0, _A_V:_A_QLAT, :], xm))
    else:
        v = proj(_A_V, _A_QLAT)
        nav_ref[...] = v.astype(BF16)
        _store_seq_t(kf_ref, k.T)
        _store_seq_t(vf_ref, v.T)

    qlat = _rms(proj(_A_QLAT, _A_CKV), qng_ref[...]).astype(BF16)
    qq = _dot(qlat, wuq_ref[...])
    qn_ref[...] = qq[:, :512].astype(BF16)
    qr = qq[:, 512:768]
    if latent:
        qr = qr * cos_ref[...] + qq[:, 768:1024] * sin_ref[...]
    qr_ref[...] = qr.astype(BF16)

    ckv = _rms(proj(_A_CKV, _A_KR), kvng_ref[...])
    ckv_b = ckv.astype(BF16)
    kn_ref[...] = _dot(ckv_b, wukv_ref[:, :512]).astype(BF16)
    if latent:
        _store_vt(mvt_ref, _dot_nt(wmvt_ref[...], ckv_b))
    else:
        ckvf_ref[...] = ckv
        mv_ref[...] = _dot(ckv_b, wukv_ref[:, 512:]).astype(BF16)

    kr = proj(_A_KR, _A_U)
    plain_group = (_lane_iota() // MLA_ROPE) % 2 == 0
    if latent:
        kr = kr * jnp.where(plain_group, cos_ref[:, :LANES], sin_ref[:, :LANES])
        kr = kr + pltpu.roll(kr, MLA_ROPE, axis=1)
    else:
        _store_seq_t(krf_ref, kr.T[:MLA_ROPE])
        kr = jnp.where(plain_group, kr, pltpu.roll(kr, MLA_ROPE, axis=1))
    kr_ref[...] = kr.astype(BF16)

    u_ref[...] = proj(_A_U, _A_END)


def _inproj(x, mod, ng, w_a, qng, kvng, wuq, wukv, lat_extra, *, tile, layer, seq=None):
    n = x.shape[0]
    groups = mod.shape[0]
    steps = n // tile
    per_group = steps // groups
    latent = lat_extra is not None
    tok = lambda w: pl.BlockSpec((tile, w), lambda i: (i, 0))
    full = lambda a: pl.BlockSpec(a.shape, lambda i: (0,) * a.ndim)
    in_specs = [tok(D_MODEL),
                pl.BlockSpec((1, 3, D_MODEL), lambda i: (i // per_group, 0, 0)),
                full(ng), _layer_spec(w_a, layer), full(qng), full(kvng), full(wuq), full(wukv)]
    args = [x, mod, ng, w_a, qng, kvng, wuq, wukv]
    out = lambda w, dt: (jax.ShapeDtypeStruct((n, w), dt), tok(w))
    if latent:
        wmvt, cos, sin = lat_extra
        steps_per_seq = cos.shape[0] // tile
        in_specs += [full(wmvt)] + [pl.BlockSpec((tile, 256), lambda i: (i % steps_per_seq, 0))] * 2
        args += [wmvt, cos, sin]
        cpt = tile // KV_CHUNK
        vt = (jax.ShapeDtypeStruct((n // KV_CHUNK, VT_ALL, KV_CHUNK), BF16),
              pl.BlockSpec((cpt, VT_ALL, KV_CHUNK), lambda i: (i, 0, 0)))
        outs = [out(512, BF16), out(512, BF16), vt, out(512, BF16), out(256, BF16), out(512, BF16),
                out(LANES, BF16), vt, out(512, F32)]
    else:
        out_t = lambda w: (jax.ShapeDtypeStruct((n // seq, w, seq), F32),
                           pl.BlockSpec((tile // seq, w, seq), lambda i: (i, 0, 0)))
        outs = [out(512, BF16)] * 3 + [out(512, BF16), out(256, BF16), out(512, BF16), out(LANES, BF16),
                                      out(512, BF16), out(512, F32),
                                      out_t(512), out_t(512), out(KV_LORA, F32), out_t(MLA_ROPE)]
    return pl.pallas_call(
        functools.partial(_inproj_kernel, latent=latent),
        grid=(steps,),
        in_specs=in_specs,
        out_specs=[o[1] for o in outs],
        out_shape=[o[0] for o in outs],
        compiler_params=_params(("arbitrary",)),
        name="inproj_lat" if latent else "inproj_ctx",
    )(*args)


def _lane_iota(n=LANES):
    return lax.broadcasted_iota(jnp.int32, (1, n), 1)


def _head_q(q_pair, half):
    lane = _lane_iota()
    keep = (lane < NA_HEAD_DIM) if half == 0 else (lane >= NA_HEAD_DIM)
    return jnp.where(keep, q_pair, jnp.zeros_like(q_pair))


def _mla_q(qn_ref, qr_ref, h):
    p, half = divmod(h, 2)
    qa = _head_q(qn_ref[:, p * LANES:(p + 1) * LANES], half)
    g, slot = divmod(h, 4)
    qb = qr_ref[:, g * LANES:(g + 1) * LANES]
    lane = _lane_iota()
    keep = (lane >= slot * MLA_ROPE) & (lane < (slot + 1) * MLA_ROPE)
    qb = jnp.where(keep, qb, jnp.zeros_like(qb))
    return jnp.concatenate([qa, qb], axis=1)


def _pair_out(o0, o1):
    return jnp.where(_lane_iota() < NA_HEAD_DIM, o0, o1)


def _softmax_pv(s_list, v_list):
    m = s_list[0].max(axis=-1, keepdims=True)
    for s in s_list[1:]:
        m = jnp.maximum(m, s.max(axis=-1, keepdims=True))
    l = None
    o = None
    for s, v in zip(s_list, v_list):
        e = jnp.exp(s - m)
        ls = e.sum(axis=-1, keepdims=True)
        pv = _dot(e.astype(BF16), v)
        l = ls if l is None else l + ls
        o = pv if o is None else o + pv
    return o / l


def _ctx_attn_kernel(q_ref, k_ref, v_ref, qn_ref, qr_ref, kn_ref, kr_ref, mv_ref, ona_ref, omla_ref):
    for p in range(NA_HEADS // 2):
        sl = slice(p * LANES, (p + 1) * LANES)
        kp = k_ref[:, sl]
        vp = v_ref[:, sl]
        outs = []
        for half in range(2):
            s = _dot_nt(_head_q(q_ref[:, sl], half), kp)
            outs.append(_softmax_pv([s], [vp]))
        ona_ref[:, sl] = _pair_out(*outs)

        kk = jnp.concatenate([kn_ref[:, sl], kr_ref[...]], axis=1)
        mvp = mv_ref[:, sl]
        outs = []
        for half in range(2):
            s = _dot_nt(_mla_q(qn_ref, qr_ref, 2 * p + half), kk) * MLA_SCALE
            outs.append(_softmax_pv([s], [mvp]))
        omla_ref[:, sl] = _pair_out(*outs)


def _ctx_attn(q, k, v, qn, qr, kn, kr, mv, *, seq):
    n = q.shape[0]
    tok = lambda w: pl.BlockSpec((seq, w), lambda b: (b, 0))
    return pl.pallas_call(
        _ctx_attn_kernel,
        grid=(n // seq,),
        in_specs=[tok(512), tok(512), tok(512), tok(512), tok(256), tok(512), tok(LANES), tok(512)],
        out_specs=[tok(512), tok(512)],
        out_shape=[jax.ShapeDtypeStruct((n, 512), F32)] * 2,
        compiler_params=_params(("arbitrary",)),
        name="ctx_attn",
    )(q, k, v, qn, qr, kn, kr, mv)


NA_QROWS = 4
NA_KROWS = 12
_NDR = 2 * NA_ROWS - 1
_NDC = 2 * NA_COLS - 1
_Z_NONE = 48
LOG2E = 1.4426950408889634


def _na_tile_descriptors(rows):
    last_q0 = rows - NA_QROWS
    last_k0 = rows - NA_KROWS
    desc = np.full((3, NA_KROWS, NA_QROWS // 2), _Z_NONE, np.int32)
    for case, (q0, k0) in enumerate(((0, 0), (NA_QROWS, 0), (last_q0, last_k0))):
        for kri in range(NA_KROWS):
            for u in range(NA_QROWS // 2):
                info = []
                for ri in (2 * u, 2 * u + 1):
                    qrow, krow = q0 + ri, k0 + kri
                    start = min(max(qrow - NA_ROWS // 2, 0), rows - NA_ROWS)
                    info.append((start <= krow < start + NA_ROWS, krow - qrow))
                (vl, drl), (vr, _) = info
                d = drl + NA_ROWS - 1
                if vl and vr:
                    desc[case, kri, u] = d
                elif vl:
                    desc[case, kri, u] = 16 + d
                elif vr:
                    desc[case, kri, u] = 32 + d
    return desc.reshape(-1)


def _na_bias_consts():
    col = np.arange(GRID_W)
    col_start = np.clip(col - NA_COLS // 2, 0, GRID_W - NA_COLS)
    kc, qc = col[:, None], col[None, :]
    col_in = (kc >= col_start[None, :]) & (kc < col_start[None, :] + NA_COLS)
    negm = np.where(col_in, 0.0, NEG).astype(np.float32)
    return np.concatenate([negm, negm], axis=1)


def _na_build_bias(bias_ref, negm_ref, z_ref):
    lane = _lane_iota()
    left = lane < GRID_W
    in_window = negm_ref[...] == 0.0

    def coef(h, dr, j):
        if dr < -(NA_ROWS - 1) or dr > NA_ROWS - 1:
            return 0.0
        return bias_ref[(h * _NDR + dr + NA_ROWS - 1) * _NDC + j]

    def per_head(h, c):
        for d in range(16):
            dr_l = d - (NA_ROWS - 1)
            base = jnp.zeros((1, LANES), F32)
            for j in range(_NDC):
                c_left = (NA_COLS - 1 - j) % LANES
                c_right = GRID_W + NA_COLS - 1 - j
                base = jnp.where(lane == c_left, coef(h, dr_l, j), base)
                base = jnp.where(lane == c_right, coef(h, dr_l - 1, j), base)
            rolled = pltpu.roll(jnp.broadcast_to(base, (GRID_W, LANES)), 0, axis=1, stride=1, stride_axis=0)
            acc = jnp.where(in_window, rolled, NEG)
            z_ref[h, d] = acc
            z_ref[h, 16 + d] = jnp.where(left, acc, NEG)
            z_ref[h, 32 + d] = jnp.where(left, NEG, acc)
        z_ref[h, _Z_NONE] = jnp.full((GRID_W, LANES), NEG, F32)
        return c

    lax.fori_loop(0, NA_HEADS, per_head, 0)


def _finish_pair(acc_a, acc_b):
    norm = lambda acc: acc[:HEAD_DIM] * (1.0 / acc[HEAD_DIM:HEAD_DIM + 1])
    return jnp.concatenate([norm(acc_a), norm(acc_b)], axis=0).T


def _na_lat_kernel(bias_ref, desc_ref, q_ref, k_ref, vt_ref, kc_ref, vtc_ref, negm_ref, o_ref, z_ref,
                   *, rows):
    g = pl.program_id(1)
    groups = rows // NA_QROWS

    @pl.when((pl.program_id(0) == 0) & (g == 0))
    def _():
        _na_build_bias(bias_ref, negm_ref, z_ref)

    tq = NA_QROWS * GRID_W
    k_row0 = jnp.clip(g * NA_QROWS - NA_ROWS // 2, 0, rows - NA_KROWS)
    start = pl.multiple_of(k_row0 * GRID_W, KV_CHUNK)
    chunk0 = k_row0 // (KV_CHUNK // GRID_W)
    n_loc = NA_KROWS * GRID_W
    case = jnp.where(g == 0, 0, jnp.where(g == groups - 1, 2, 1))
    upairs = NA_QROWS // 2

    def scores(h):
        sl = slice((h // 2) * LANES, (h // 2 + 1) * LANES)
        qt = _head_q(q_ref[:, sl], h % 2).astype(F32).T.astype(BF16)
        return _dot(k_ref[pl.ds(start, n_loc), sl], qt), _dot(kc_ref[0, :, sl], qt)

    def attend(h, s_loc, s_ctx):
        rows_h = slice(h * VT_ROWS, (h + 1) * VT_ROWS)
        blocks = []
        for kri in range(NA_KROWS):
            tiles = []
            for u in range(upairs):
                idx = desc_ref[(case * NA_KROWS + kri) * upairs + u]
                tiles.append(s_loc[kri * GRID_W:(kri + 1) * GRID_W, u * LANES:(u + 1) * LANES] + z_ref[h, idx])
            blocks.append(jnp.concatenate(tiles, axis=1))
        s_loc = jnp.concatenate(blocks, axis=0)
        m = jnp.maximum(s_loc.max(axis=0, keepdims=True), s_ctx.max(axis=0, keepdims=True))
        p_loc = jnp.exp2((s_loc - m) * LOG2E).astype(BF16)
        p_ctx = jnp.exp2((s_ctx - m) * LOG2E).astype(BF16)
        acc = _dot(vtc_ref[0, rows_h, :], p_ctx)
        for i in range(n_loc // KV_CHUNK):
            acc = acc + _dot(vt_ref[chunk0 + i, rows_h, :], p_loc[i * KV_CHUNK:(i + 1) * KV_CHUNK])
        return acc

    s_next = scores(0)
    accs = []
    for h in range(NA_HEADS):
        s = s_next
        if h + 1 < NA_HEADS:
            s_next = scores(h + 1)
        accs.append(attend(h, *s))
        if h % 2:
            o_ref[:, (h // 2) * LANES:(h // 2 + 1) * LANES] = _finish_pair(accs[h - 1], accs[h])


def _na_lat(q, k, vt, kc, vtc, bias_l, *, batch, seq):
    rows = seq // GRID_W
    n = q.shape[0]
    groups = rows // NA_QROWS
    tq = NA_QROWS * GRID_W
    chunks = seq // KV_CHUNK
    negm = _na_bias_consts()
    desc = jnp.asarray(_na_tile_descriptors(rows))
    smem = pl.BlockSpec(memory_space=pltpu.SMEM)
    full = lambda a: pl.BlockSpec(a.shape, lambda b, g: (0,) * a.ndim)
    return pl.pallas_call(
        functools.partial(_na_lat_kernel, rows=rows),
        grid=(batch, groups),
        in_specs=[
            smem, smem,
            pl.BlockSpec((tq, 512), lambda b, g: (b * groups + g, 0)),
            pl.BlockSpec((seq, 512), lambda b, g: (b, 0)),
            pl.BlockSpec((chunks, VT_ALL, KV_CHUNK), lambda b, g: (b, 0, 0)),
            pl.BlockSpec((1,) + kc.shape[1:], lambda b, g: (b, 0, 0)),
            pl.BlockSpec((1,) + vtc.shape[1:], lambda b, g: (b, 0, 0)),
            full(negm),
        ],
        out_specs=pl.BlockSpec((tq, 512), lambda b, g: (b * groups + g, 0)),
        out_shape=jax.ShapeDtypeStruct((n, 512), F32),
        scratch_shapes=[pltpu.VMEM((NA_HEADS, _Z_NONE + 1, GRID_W, LANES), F32)],
        compiler_params=_params(("arbitrary", "arbitrary")),
        name="na_lat",
    )(bias_l.reshape(-1), desc, q, k, vt, kc, vtc, jnp.asarray(negm))


_MLA_C = MLA_SCALE * LOG2E


def _mla_lat_kernel(qn_ref, qr_ref, kn_ref, kr_ref, vt_ref, knc_ref, krc_ref, vtc_ref, o_ref,
                    q2_ref, m_ref, acc_ref, sa_ref, sb_ref, *, tk):
    tq = qn_ref.shape[0]
    n_tiles = kn_ref.shape[0] // tk
    cpt = tk // KV_CHUNK
    pairs = MLA_HEADS // 2
    for p in range(pairs):
        qt = [_mla_q(qn_ref, qr_ref, 2 * p + half).astype(F32).T.astype(BF16) for half in range(2)]
        q2_ref[p] = jnp.concatenate(qt, axis=1)
        m_ref[p] = jnp.full((1, 2 * tq), NEG, F32)
        acc_ref[p] = jnp.zeros((2, VT_ROWS, tq), F32)

    def scores(p, kk):
        return _dot(kk, q2_ref[p])

    def update(p, s, vts):
        m_prev = m_ref[p]
        m_new = jnp.maximum(m_prev, s.max(axis=0, keepdims=True))
        alpha = jnp.exp2((m_prev - m_new) * _MLA_C)
        pt = jnp.exp2((s - m_new) * _MLA_C).astype(BF16)
        accs = []
        for half in range(2):
            cols = slice(half * tq, (half + 1) * tq)
            acc = acc_ref[p, half] * alpha[:, cols]
            for i, vt in enumerate(vts):
                acc = acc + _dot(vt[half * VT_ROWS:(half + 1) * VT_ROWS], pt[i * KV_CHUNK:(i + 1) * KV_CHUNK, cols])
            accs.append(acc)
        return m_new, accs

    def store(new):
        for p, (m_new, accs) in enumerate(new):
            acc_ref[p, 0] = accs[0]
            acc_ref[p, 1] = accs[1]
            m_ref[p] = m_new

    def lanes(p):
        return slice(p * LANES, (p + 1) * LANES)

    def rows(p):
        return slice(2 * p * VT_ROWS, 2 * (p + 1) * VT_ROWS)

    def keys(t, p):
        start = t * tk if isinstance(t, int) else pl.multiple_of(t * tk, tk)
        ks = pl.ds(start, tk)
        return jnp.concatenate([kn_ref[ks, lanes(p)], kr_ref[ks, :]], axis=1)

    for p in range(pairs):
        sa_ref[p] = scores(p, keys(0, p))

    n_ctx = knc_ref.shape[1]

    def latent_next(t):
        def put(nxt_ref, p):
            nxt_ref[p] = scores(p, keys(t + 1, p))
        return put

    def ctx_next(nxt_ref, p):
        kk = jnp.concatenate([knc_ref[0, :, lanes(p)], krc_ref[0]], axis=1)
        nxt_ref[p, :n_ctx] = scores(p, kk)

    def tile_step(t, cur_ref, nxt_ref, put_next):
        ahead = 1
        for p in range(ahead):
            put_next(nxt_ref, p)
        new = []
        for p in range(pairs):
            new.append(update(p, cur_ref[p], [vt_ref[t * cpt + i, rows(p), :] for i in range(cpt)]))
            if p + ahead < pairs:
                put_next(nxt_ref, p + ahead)
        store(new)

    def body(i, c):
        tile_step(2 * i, sa_ref, sb_ref, latent_next(2 * i))
        tile_step(2 * i + 1, sb_ref, sa_ref, latent_next(2 * i + 1))
        return c

    lax.fori_loop(0, n_tiles // 2 - 1, body, 0)
    tile_step(n_tiles - 2, sa_ref, sb_ref, latent_next(n_tiles - 2))
    tile_step(n_tiles - 1, sb_ref, sa_ref, ctx_next)
    for p in range(pairs):
        _, accs = update(p, sa_ref[p, :n_ctx], [vtc_ref[0, rows(p), :]])
        o_ref[:, lanes(p)] = _finish_pair(*accs)


def _mla_lat(qn, qr, kn, kr, vt, knc, krc, vtc, *, batch, seq, tq, tk):
    n = qn.shape[0]
    steps = seq // tq
    chunks = seq // KV_CHUNK
    assert knc.shape[1] == KV_CHUNK
    qtok = lambda w: pl.BlockSpec((tq, w), lambda b, i: (b * steps + i, 0))
    ktok = lambda w: pl.BlockSpec((seq, w), lambda b, i: (b, 0))
    ctok = lambda a: pl.BlockSpec((1,) + a.shape[1:], lambda b, i: (b, 0, 0))
    pairs = MLA_HEADS // 2
    return pl.pallas_call(
        functools.partial(_mla_lat_kernel, tk=tk),
        grid=(batch, steps),
        in_specs=[qtok(512), qtok(256), ktok(512), ktok(LANES),
                  pl.BlockSpec((chunks, VT_ALL, KV_CHUNK), lambda b, i: (b, 0, 0)),
                  ctok(knc), ctok(krc), ctok(vtc)],
        out_specs=qtok(512),
        out_shape=jax.ShapeDtypeStruct((n, 512), F32),
        scratch_shapes=[pltpu.VMEM((pairs, 2 * LANES, 2 * tq), BF16),
                        pltpu.VMEM((pairs, 1, 2 * tq), F32),
                        pltpu.VMEM((pairs, 2, VT_ROWS, tq), F32),
                        pltpu.VMEM((pairs, tk, 2 * tq), F32),
                        pltpu.VMEM((pairs, tk, 2 * tq), F32)],
        compiler_params=_params(("arbitrary", "arbitrary")),
        name="mla_lat",
    )(qn, qr, kn, kr, vt, knc, krc, vtc)


def _cache_kv_kernel(ckv_ref, w_ref, wvt_ref, kn_ref, vt_ref):
    ckv = ckv_ref[0, 0].astype(BF16)
    kn_ref[0, 0] = _dot(ckv, w_ref[0]).astype(BF16)
    _store_vt(vt_ref.at[0], _dot_nt(wvt_ref[0], ckv))


def _cache_kv(ckv, wukv_k, wukv_vt):
    _, nb, rows, _ = ckv.shape
    assert rows == KV_CHUNK
    spec = lambda a, b: pl.BlockSpec((1, a, b), lambda l, j: (l, 0, 0))
    return pl.pallas_call(
        _cache_kv_kernel,
        grid=(DEPTH, nb),
        in_specs=[pl.BlockSpec((1, 1, rows, KV_LORA), lambda l, j: (l, j, 0, 0)),
                  spec(KV_LORA, 512), spec(512, KV_LORA)],
        out_specs=[pl.BlockSpec((1, 1, rows, 512), lambda l, j: (l, j, 0, 0)),
                   pl.BlockSpec((1, 1, VT_ALL, KV_CHUNK), lambda l, j: (l, j, 0, 0))],
        out_shape=[jax.ShapeDtypeStruct((DEPTH, nb, rows, 512), BF16),
                   jax.ShapeDtypeStruct((DEPTH, nb, VT_ALL, KV_CHUNK), BF16)],
        compiler_params=_params(("arbitrary", "arbitrary")),
        name="cache_kv",
    )(ckv, wukv_k, wukv_vt)


def _dft_mats(n, scale):
    k = np.arange(n)
    ang = 2.0 * np.pi * ((k[:, None] * k[None, :]) % n) / n
    return np.cos(ang) * scale, np.sin(ang) * scale


def _hi_lo_np(m):
    m = jnp.asarray(m, F32)
    hi = m.astype(BF16)
    lo = (m - hi.astype(F32)).astype(BF16)
    return hi, lo


def _chan_dft(u, wc_ref):
    uh, ul = _split(u)
    wh = wc_ref[0]
    wl = wc_ref[1]
    return _dot(uh, wh) + _dot(ul, wh) + _dot(uh, wl)


def _fn_ctx_kernel(u_ref, wc_ref, fp_ref, o_ref):
    his, los = [], []
    for g in range(FN_GROUPS):
        sl = slice(g * FN_GROUP_W, (g + 1) * FN_GROUP_W)
        a = _chan_dft(u_ref[:, sl], wc_ref)
        ar_h, ar_l = _split(a[:, :FN_GROUP_W])
        ai_h, ai_l = _split(a[:, FN_GROUP_W:])
        his.append(jnp.concatenate([ar_h, ai_h], axis=0))
        los.append(jnp.concatenate([ar_l, ai_l], axis=0))
    rh = jnp.concatenate(his, axis=1)
    rl = jnp.concatenate(los, axis=1)
    o_ref[...] = _dot(fp_ref[0], rh) + _dot(fp_ref[0], rl) + _dot(fp_ref[1], rh)


def _fn_ctx(u, *, seq):
    n = u.shape[0]
    cc, sc = _dft_mats(FN_GROUP_W, FN_GROUP_W ** -0.5)
    wc = jnp.stack(_hi_lo_np(np.concatenate([cc, -sc], axis=1)))
    cp, sp = _dft_mats(seq, seq ** -0.5)
    fp = jnp.stack(_hi_lo_np(np.concatenate([cp, sp], axis=1)))
    full = lambda a: pl.BlockSpec(a.shape, lambda b: (0,) * a.ndim)
    return pl.pallas_call(
        _fn_ctx_kernel,
        grid=(n // seq,),
        in_specs=[pl.BlockSpec((seq, FN_WIDTH), lambda b: (b, 0)), full(wc), full(fp)],
        out_specs=pl.BlockSpec((seq, FN_WIDTH), lambda b: (b, 0)),
        out_shape=jax.ShapeDtypeStruct((n, FN_WIDTH), F32),
        compiler_params=_params(("arbitrary",)),
        name="fn_ctx",
    )(u, wc, fp)


_FN_G = 32


def _fn_lat_kernel(u_ref, wc_ref, f1_ref, f2_ref, twc_ref, tws_ref, o_ref, a_ref, t_ref):
    n = GRID_W
    w = FN_GROUP_W

    def chan(i, c):
        rows = pl.ds(pl.multiple_of(i * 512, 512), 512)
        a = _chan_dft(u_ref[rows, :], wc_ref)
        a_ref[0, rows, :] = a[:, :FN_GROUP_W]
        a_ref[1, rows, :] = a[:, FN_GROUP_W:]
        return c

    lax.fori_loop(0, u_ref.shape[0] // 512, chan, 0)

    def gather(ref, base):
        cols = [jnp.concatenate([ref[0, pl.ds(base + j, n, stride=n), :],
                                 ref[1, pl.ds(base + j, n, stride=n), :]], axis=0) for j in range(_FN_G)]
        return jnp.concatenate(cols, axis=1)

    def dft(f_ref, d):
        rows = f_ref.shape[0] // 2
        dh, dl = _split(d)
        rh = _dot(f_ref[...], dh)
        return rh[:rows] + rh[rows:] + _dot(f_ref[:rows, :], dl)

    def stage1(i, c):
        base = i * _FN_G
        b = dft(f1_ref, gather(a_ref, base))
        for j in range(_FN_G):
            br = b[:n, j * w:(j + 1) * w]
            bi = b[n:, j * w:(j + 1) * w]
            rows = pl.ds(pl.multiple_of((base + j) * n, n), n)
            tc = twc_ref[rows, :]
            ts = tws_ref[rows, :]
            t_ref[0, rows, :] = br * tc + bi * ts
            t_ref[1, rows, :] = bi * tc - br * ts
        return c

    lax.fori_loop(0, n // _FN_G, stage1, 0)

    def stage2(i, c):
        base = i * _FN_G
        y = dft(f2_ref, gather(t_ref, base))
        for j in range(_FN_G):
            o_ref[pl.ds(base + j, n, stride=n), :] = y[:, j * w:(j + 1) * w]
        return c

    lax.fori_loop(0, n // _FN_G, stage2, 0)


def _fn_lat(u, *, batch, seq):
    n = GRID_W
    assert seq == n * n
    cc, sc = _dft_mats(FN_GROUP_W, FN_GROUP_W ** -0.5)
    wc = jnp.stack(_hi_lo_np(np.concatenate([cc, -sc], axis=1)))
    c1, s1 = _dft_mats(n, 1.0)
    c2, s2 = _dft_mats(n, 1.0 / n)
    f1 = jnp.concatenate(_hi_lo_np(np.block([[c1, s1], [-s1, c1]])), axis=0)
    f2 = jnp.concatenate(_hi_lo_np(np.concatenate([c2, s2], axis=1)), axis=0)
    n2 = np.arange(n)[:, None]
    k1 = np.arange(n)[None, :]
    ang = (2.0 * np.pi * ((n2 * k1) % seq) / seq).reshape(seq, 1)
    twc = jnp.asarray(np.broadcast_to(np.cos(ang), (seq, FN_GROUP_W)), F32)
    tws = jnp.asarray(np.broadcast_to(np.sin(ang), (seq, FN_GROUP_W)), F32)
    full = lambda a: pl.BlockSpec(a.shape, lambda b, g: (0,) * a.ndim)
    blk = pl.BlockSpec((seq, FN_GROUP_W), lambda b, g: (b, g))
    return pl.pallas_call(
        _fn_lat_kernel,
        grid=(batch, FN_GROUPS),
        in_specs=[blk, full(wc), full(f1), full(f2), full(twc), full(tws)],
        out_specs=blk,
        out_shape=jax.ShapeDtypeStruct(u.shape, F32),
        scratch_shapes=[pltpu.VMEM((2, seq, FN_GROUP_W), F32), pltpu.VMEM((2, seq, FN_GROUP_W), F32)],
        compiler_params=_params(("arbitrary", "arbitrary")),
        name="fn_lat",
    )(u, wc, f1, f2, twc, tws)


_E_GNA, _E_GMLA, _E_GFN, _E_MERGE = 0, 512, 1024, 1536


def _merge_kernel(*refs, final):
    (x_ref, mod_ref, ng_ref, wg_ref, ona_ref, omla_ref, ofn_ref, wona_ref, womla_ref, wofn_ref,
     wout_ref) = refs[:11]
    if final:
        fg_ref, o_ref = refs[11:13]
    else:
        o_ref = refs[11]
    x, xm = _modulated(x_ref, mod_ref, ng_ref)
    gate = mod_ref[0, 2:3, :]

    merged = None
    for i, (br_ref, wo_ref) in enumerate(((ona_ref, wona_ref), (omla_ref, womla_ref), (ofn_ref, wofn_ref))):
        g = _dot_nt(xm, wg_ref[0, i * 512:(i + 1) * 512, :])
        t = _dot((br_ref[...] * _silu(g)).astype(BF16), wo_ref[0])
        ml = _dot_nt(xm, wg_ref[0, _E_MERGE + i * D_MODEL:_E_MERGE + (i + 1) * D_MODEL, :])
        t = _sigmoid(ml) * t
        merged = t if merged is None else merged + t
    h = x + gate * _dot(merged.astype(BF16), wout_ref[0])
    if final:
        h = _rms(h, fg_ref[...])
    o_ref[...] = h


def _layer_spec(a, layer):
    return pl.BlockSpec((1,) + a.shape[1:], lambda *_: (layer,) + (0,) * (a.ndim - 1))


def _merge(x, mod, ng, w_g, ona, omla, ofn, wona, womla, wofn, wout, fg, *, tile, layer):
    n = x.shape[0]
    groups = mod.shape[0]
    steps = n // tile
    per_group = steps // groups
    final = fg is not None
    tok = lambda w: pl.BlockSpec((tile, w), lambda i: (i, 0))
    full = lambda a: pl.BlockSpec(a.shape, lambda i: (0,) * a.ndim)
    lay = lambda a: _layer_spec(a, layer)
    in_specs = [tok(D_MODEL), pl.BlockSpec((1, 3, D_MODEL), lambda i: (i // per_group, 0, 0)),
                full(ng), lay(w_g), tok(512), tok(512), tok(512),
                lay(wona), lay(womla), lay(wofn), lay(wout)]
    args = [x, mod, ng, w_g, ona, omla, ofn, wona, womla, wofn, wout]
    if final:
        in_specs.append(full(fg))
        args.append(fg)
    return pl.pallas_call(
        functools.partial(_merge_kernel, final=final),
        grid=(steps,),
        in_specs=in_specs,
        out_specs=tok(D_MODEL),
        out_shape=jax.ShapeDtypeStruct((n, D_MODEL), F32),
        compiler_params=_params(("arbitrary",)),
        name="merge",
    )(*args)


def _rope_rot(w, axis):
    q = MLA_ROPE // 4
    part = lambda i: lax.slice_in_dim(w, i * q, (i + 1) * q, axis=axis)
    return jnp.concatenate([-part(1), part(0), -part(3), part(2)], axis=axis)


def _prep_w_in(w_in):
    wt = jnp.swapaxes(w_in, 1, 2)
    rows = lambda start, n: wt[:, start:start + n]
    kr = rows(_O_KROPE, MLA_ROPE)
    kr_rot = _rope_rot(kr, 1)
    a = jnp.concatenate([rows(_O_QKV, 1536), rows(_O_QLAT, Q_LORA + KV_LORA),
                         kr, kr_rot, kr, kr_rot, rows(_O_UFN, FN_WIDTH)], axis=1)
    e = jnp.concatenate([rows(_O_GATE_NA, 512), rows(_O_GATE_MLA, 512), rows(_O_GATE_FN, 512),
                         rows(_O_MERGE, 3 * D_MODEL)], axis=1)
    return a.astype(BF16), e.astype(BF16)


def _prep_w_uq(w):
    w3 = w.reshape(Q_LORA, MLA_HEADS, MLA_NOPE + MLA_ROPE)
    nope = w3[:, :, :MLA_NOPE].reshape(Q_LORA, MLA_HEADS * MLA_NOPE)
    rope = w3[:, :, MLA_NOPE:]
    rot = _rope_rot(rope, 2)
    return jnp.concatenate([nope, rope.reshape(Q_LORA, -1), rot.reshape(Q_LORA, -1)], axis=1).astype(BF16)


def _prep_w_ukv(w):
    w3 = w.reshape(KV_LORA, MLA_HEADS, MLA_NOPE + MLA_V)
    return jnp.concatenate([w3[:, :, :MLA_NOPE].reshape(KV_LORA, -1),
                            w3[:, :, MLA_NOPE:].reshape(KV_LORA, -1)], axis=1).astype(BF16)


def _rope_tables(n):
    t = jnp.arange(n, dtype=jnp.int32)
    row = (t // GRID_W).astype(F32)
    col = (t % GRID_W).astype(F32)
    half = MLA_ROPE // 2
    inv_freq = ROPE_THETA ** (-jnp.arange(0, half, 2, dtype=F32) / half)
    ar = row[:, None] * inv_freq[None, :]
    ac = col[:, None] * inv_freq[None, :]
    ang = jnp.concatenate([ar, ar, ac, ac], axis=-1)
    return jnp.tile(jnp.cos(ang), (1, MLA_HEADS)), jnp.tile(jnp.sin(ang), (1, MLA_HEADS))


def kernel(x_prompt, x_sample, cache_na_k, cache_na_v, cache_mla_ckv, cache_mla_krope, c, c_ctx,
           w_ada, b_ada, norm_g, w_in, q_norm_g, kv_norm_g, w_uq, w_ukv, na_bias,
           w_o_na, w_o_mla, w_o_fourier, w_out, final_norm_g):
    batch, seq, _ = x_prompt.shape
    dbatch, dseq, _ = x_sample.shape
    past = cache_na_k.shape[2]

    cond = jnp.zeros((SUBLANES, D_MODEL), F32).at[0].set(c_ctx).at[1:1 + dbatch].set(c)
    mods = _ada_mods(cond, w_ada, b_ada).reshape(DEPTH, SUBLANES, 3, D_MODEL)

    wukv_all = jnp.stack([_prep_w_ukv(w_ukv[l]) for l in range(DEPTH)])
    wukv_vt_all = wukv_all[:, :, 512:].transpose(0, 2, 1)
    knc_all, mvtc_all = _cache_kv(cache_mla_ckv.transpose(1, 0, 2, 3), wukv_all[:, :, :512], wukv_vt_all)
    krc_all = jnp.tile(cache_mla_krope, (1, 1, 1, 4)).astype(BF16)
    nakc_all = cache_na_k.reshape(dbatch, DEPTH, past, NA_WIDTH).astype(BF16)
    navt = cache_na_v.transpose(0, 1, 3, 4, 2).astype(BF16)
    navtc_all = jnp.concatenate([navt, jnp.ones((dbatch, DEPTH, NA_HEADS, ONES_ROWS, past), BF16)],
                                axis=3).reshape(dbatch, DEPTH, VT_ALL, past)
    cos, sin = _rope_tables(dseq)

    h_ctx = x_prompt.reshape(batch * seq, D_MODEL)
    h_lat = x_sample.reshape(dbatch * dseq, D_MODEL)
    ks, vs, ckvs, krs = [], [], [], []
    row = lambda a: a.reshape(1, -1)
    w_a, w_e = _prep_w_in(w_in)
    wona, womla, wofn, wout = (w_o_na.astype(BF16), w_o_mla.astype(BF16),
                               w_o_fourier.astype(BF16), w_out.astype(BF16))
    for l in range(DEPTH):
        wuq = _prep_w_uq(w_uq[l])
        wukv = wukv_all[l]
        ng, qng, kvng = row(norm_g[l]), row(q_norm_g[l]), row(kv_norm_g[l])
        fg = row(final_norm_g) if l == DEPTH - 1 else None
        mod_ctx = mods[l, 0:1]
        mod_lat = mods[l, 1:1 + dbatch]

        (q, k, v, qn, qr, kn, kr, mv, u, kf, vf, ckvf, krf) = _inproj(
            h_ctx, mod_ctx, ng, w_a, qng, kvng, wuq, wukv, None, tile=INPROJ_TILE, layer=l, seq=seq)
        ks.append(kf)
        vs.append(vf)
        ckvs.append(ckvf)
        krs.append(krf)
        ona, omla = _ctx_attn(q, k, v, qn, qr, kn, kr, mv, seq=seq)
        ofn = _fn_ctx(u, seq=seq)
        h_ctx = _merge(h_ctx, mod_ctx, ng, w_e, ona, omla, ofn, wona, womla, wofn, wout, fg, tile=TOKEN_TILE, layer=l)

        (q, k, vt, qn, qr, kn, kr, mvt, u) = _inproj(
            h_lat, mod_lat, ng, w_a, qng, kvng, wuq, wukv, (wukv_vt_all[l], cos, sin), tile=INPROJ_TILE, layer=l)
        ona = _na_lat(q, k, vt, nakc_all[:, l], navtc_all[:, l], na_bias[l], batch=dbatch, seq=dseq)
        omla = _mla_lat(qn, qr, kn, kr, mvt, knc_all[l], krc_all[:, l], mvtc_all[l],
                        batch=dbatch, seq=dseq, tq=MLA_TQ, tk=MLA_TK)
        ofn = _fn_lat(u, batch=dbatch, seq=dseq)
        h_lat = _merge(h_lat, mod_lat, ng, w_e, ona, omla, ofn, wona, womla, wofn, wout, fg, tile=TOKEN_TILE, layer=l)

    y_prompt = h_ctx.reshape(batch, seq, D_MODEL)
    y_sample = h_lat.reshape(dbatch, dseq, D_MODEL)
    heads_t = lambda xs: jnp.stack(xs, axis=1).reshape(
        batch, DEPTH, NA_HEADS, NA_HEAD_DIM, seq).transpose(0, 1, 4, 2, 3)
    new_na_k = heads_t(ks)
    new_na_v = heads_t(vs)
    new_mla_ckv = jnp.stack([a.reshape(batch, seq, KV_LORA) for a in ckvs], axis=1)
    new_mla_krope = jnp.stack(krs, axis=1).transpose(0, 1, 3, 2)
    return (y_prompt, y_sample, new_na_k, new_na_v, new_mla_ckv, new_mla_krope)
```

```python
import functools

import jax
import jax.numpy as jnp
import numpy as np
from jax import lax
from jax.experimental import pallas as pl
from jax.experimental.pallas import tpu as pltpu

F32 = jnp.float32
BF16 = jnp.bfloat16

D_MODEL = 1024
DEPTH = 4
GRID_W = 64
NA_HEADS = 8
NA_HEAD_DIM = 64
NA_WIDTH = 512
NA_ROWS = 8
NA_COLS = 16
MLA_HEADS = 8
MLA_NOPE = 64
MLA_ROPE = 32
MLA_V = 64
MLA_WIDTH = 512
Q_LORA = 256
KV_LORA = 128
MLA_SCALE = (MLA_NOPE + MLA_ROPE) ** -0.5
ROPE_THETA = 10000.0
FN_GROUPS = 4
FN_GROUP_W = 128
FN_WIDTH = 512
EPS = 1e-6
NEG = -1e30

_O_QKV = 0
_O_GATE_NA = 1536
_O_QLAT = 2048
_O_CKV = 2304
_O_KROPE = 2432
_O_GATE_MLA = 2464
_O_UFN = 2976
_O_GATE_FN = 3488
_O_MERGE = 4000
D_IN = 7072

LANES = 128
SUBLANES = 8
VMEM_LIMIT = 56 * 1024 * 1024

TOKEN_TILE = 512
INPROJ_TILE = 1024
MLA_TQ = 256
MLA_TK = 512


def _params(sem):
    return pltpu.CompilerParams(dimension_semantics=sem, vmem_limit_bytes=VMEM_LIMIT)


def _sigmoid(x):
    return 1.0 / (1.0 + jnp.exp(-x))


def _silu(x):
    return x * _sigmoid(x)


def _rms(x, g):
    return x * lax.rsqrt(jnp.mean(x * x, axis=-1, keepdims=True) + EPS) * g


def _dot(a, b):
    return jnp.dot(a, b, preferred_element_type=F32)


def _dot_nt(a, b):
    return lax.dot_general(a, b, (((1,), (1,)), ((), ())), preferred_element_type=F32)


def _split(x):
    hi = x.astype(BF16)
    lo = (x - hi.astype(F32)).astype(BF16)
    return hi, lo


def _ada_kernel(cond_ref, w_ref, b_ref, o_ref):
    a = _silu(cond_ref[...]).astype(BF16)
    o_ref[0] = _dot(a, w_ref[0].astype(BF16)) + b_ref[0]


def _ada_mods(cond, w_ada, b_ada):
    nb = 3
    return pl.pallas_call(
        _ada_kernel,
        grid=(DEPTH, nb),
        in_specs=[
            pl.BlockSpec((SUBLANES, D_MODEL), lambda l, j: (0, 0)),
            pl.BlockSpec((1, D_MODEL, D_MODEL), lambda l, j: (l, 0, j)),
            pl.BlockSpec((1, 1, D_MODEL), lambda l, j: (l, 0, j)),
        ],
        out_specs=pl.BlockSpec((1, SUBLANES, D_MODEL), lambda l, j: (l, 0, j)),
        out_shape=jax.ShapeDtypeStruct((DEPTH, SUBLANES, 3 * D_MODEL), F32),
        compiler_params=_params(("arbitrary", "arbitrary")),
        name="ada_mods",
    )(cond, w_ada, b_ada.reshape(DEPTH, 1, 3 * D_MODEL))


_A_Q, _A_K, _A_V, _A_QLAT, _A_CKV, _A_KR, _A_U, _A_END = (
    0, 512, 1024, 1536, 1792, 1920, 2048, 2560)


def _modulated(x_ref, mod_ref, ng_ref):
    x = x_ref[...]
    shift = mod_ref[0, 0:1, :]
    scale = mod_ref[0, 1:2, :]
    return x, (_rms(x, ng_ref[...]) * (1.0 + scale) + shift).astype(BF16)


KV_CHUNK = 256
HEAD_DIM = 64
ONES_ROWS = 16
VT_ROWS = HEAD_DIM + ONES_ROWS
VT_ALL = 8 * VT_ROWS


def _store_vt(vt_ref, vt):
    for c in range(vt_ref.shape[0]):
        cols = slice(c * KV_CHUNK, (c + 1) * KV_CHUNK)
        for h in range(8):
            vt_ref[c, h * VT_ROWS:h * VT_ROWS + HEAD_DIM, :] = vt[h * HEAD_DIM:(h + 1) * HEAD_DIM, cols].astype(BF16)
            vt_ref[c, h * VT_ROWS + HEAD_DIM:(h + 1) * VT_ROWS, :] = jnp.ones((ONES_ROWS, KV_CHUNK), BF16)


def _store_seq_t(ref, xt):
    seq = ref.shape[2]
    for j in range(ref.shape[0]):
        ref[j] = xt[:, j * seq:(j + 1) * seq]


def _inproj_kernel(*refs, latent):
    (x_ref, mod_ref, ng_ref, w_ref, qng_ref, kvng_ref, wuq_ref, wukv_ref) = refs[:8]
    if latent:
        (wmvt_ref, cos_ref, sin_ref,
         naq_ref, nak_ref, navt_ref, qn_ref, qr_ref, kn_ref, kr_ref, mvt_ref, u_ref) = refs[8:]
    else:
        (naq_ref, nak_ref, nav_ref, qn_ref, qr_ref, kn_ref, kr_ref, mv_ref, u_ref,
         kf_ref, vf_ref, ckvf_ref, krf_ref) = refs[8:]

    _, xm = _modulated(x_ref, mod_ref, ng_ref)

    def proj(a, b):
        return _dot_nt(xm, w_ref[0, a:b, :])

    q = proj(_A_Q, _A_K)
    naq_ref[...] = (q * (NA_HEAD_DIM ** -0.5)).astype(BF16)
    k = proj(_A_K, _A_V)
    nak_ref[...] = k.astype(BF16)
    if latent:
        _store_vt(navt_ref, _dot_nt(w_ref[0, _A_V:_A_QLAT, :], xm))
    else:
        v = proj(_A_V, _A_QLAT)
        nav_ref[...] = v.astype(BF16)
        _store_seq_t(kf_ref, k.T)
        _store_seq_t(vf_ref, v.T)

    qlat = _rms(proj(_A_QLAT, _A_CKV), qng_ref[...]).astype(BF16)
    qq = _dot(qlat, wuq_ref[...])
    qn_ref[...] = qq[:, :512].astype(BF16)
    qr = qq[:, 512:768]
    if latent:
        qr = qr * cos_ref[...] + qq[:, 768:1024] * sin_ref[...]
    qr_ref[...] = qr.astype(BF16)

    ckv = _rms(proj(_A_CKV, _A_KR), kvng_ref[...])
    ckv_b = ckv.astype(BF16)
    kn_ref[...] = _dot(ckv_b, wukv_ref[:, :512]).astype(BF16)
    if latent:
        _store_vt(mvt_ref, _dot_nt(wmvt_ref[...], ckv_b))
    else:
        ckvf_ref[...] = ckv
        mv_ref[...] = _dot(ckv_b, wukv_ref[:, 512:]).astype(BF16)

    kr = proj(_A_KR, _A_U)
    plain_group = (_lane_iota() // MLA_ROPE) % 2 == 0
    if latent:
        kr = kr * jnp.where(plain_group, cos_ref[:, :LANES], sin_ref[:, :LANES])
        kr = kr + pltpu.roll(kr, MLA_ROPE, axis=1)
    else:
        _store_seq_t(krf_ref, kr.T[:MLA_ROPE])
        kr = jnp.where(plain_group, kr, pltpu.roll(kr, MLA_ROPE, axis=1))
    kr_ref[...] = kr.astype(BF16)

    u_ref[...] = proj(_A_U, _A_END)


def _inproj(x, mod, ng, w_a, qng, kvng, wuq, wukv, lat_extra, *, tile, layer, seq=None):
    n = x.shape[0]
    groups = mod.shape[0]
    steps = n // tile
    per_group = steps // groups
    latent = lat_extra is not None
    tok = lambda w: pl.BlockSpec((tile, w), lambda i: (i, 0))
    full = lambda a: pl.BlockSpec(a.shape, lambda i: (0,) * a.ndim)
    in_specs = [tok(D_MODEL),
                pl.BlockSpec((1, 3, D_MODEL), lambda i: (i // per_group, 0, 0)),
                full(ng), _layer_spec(w_a, layer), full(qng), full(kvng), full(wuq), full(wukv)]
    args = [x, mod, ng, w_a, qng, kvng, wuq, wukv]
    out = lambda w, dt: (jax.ShapeDtypeStruct((n, w), dt), tok(w))
    if latent:
        wmvt, cos, sin = lat_extra
        steps_per_seq = cos.shape[0] // tile
        in_specs += [full(wmvt)] + [pl.BlockSpec((tile, 256), lambda i: (i % steps_per_seq, 0))] * 2
        args += [wmvt, cos, sin]
        cpt = tile // KV_CHUNK
        vt = (jax.ShapeDtypeStruct((n // KV_CHUNK, VT_ALL, KV_CHUNK), BF16),
              pl.BlockSpec((cpt, VT_ALL, KV_CHUNK), lambda i: (i, 0, 0)))
        outs = [out(512, BF16), out(512, BF16), vt, out(512, BF16), out(256, BF16), out(512, BF16),
                out(LANES, BF16), vt, out(512, F32)]
    else:
        out_t = lambda w: (jax.ShapeDtypeStruct((n // seq, w, seq), F32),
                           pl.BlockSpec((tile // seq, w, seq), lambda i: (i, 0, 0)))
        outs = [out(512, BF16)] * 3 + [out(512, BF16), out(256, BF16), out(512, BF16), out(LANES, BF16),
                                      out(512, BF16), out(512, F32),
                                      out_t(512), out_t(512), out(KV_LORA, F32), out_t(MLA_ROPE)]
    return pl.pallas_call(
        functools.partial(_inproj_kernel, latent=latent),
        grid=(steps,),
        in_specs=in_specs,
        out_specs=[o[1] for o in outs],
        out_shape=[o[0] for o in outs],
        compiler_params=_params(("arbitrary",)),
        name="inproj_lat" if latent else "inproj_ctx",
    )(*args)


def _lane_iota(n=LANES):
    return lax.broadcasted_iota(jnp.int32, (1, n), 1)


def _head_q(q_pair, half):
    lane = _lane_iota()
    keep = (lane < NA_HEAD_DIM) if half == 0 else (lane >= NA_HEAD_DIM)
    return jnp.where(keep, q_pair, jnp.zeros_like(q_pair))


def _mla_q(qn_ref, qr_ref, h):
    p, half = divmod(h, 2)
    qa = _head_q(qn_ref[:, p * LANES:(p + 1) * LANES], half)
    g, slot = divmod(h, 4)
    qb = qr_ref[:, g * LANES:(g + 1) * LANES]
    lane = _lane_iota()
    keep = (lane >= slot * MLA_ROPE) & (lane < (slot + 1) * MLA_ROPE)
    qb = jnp.where(keep, qb, jnp.zeros_like(qb))
    return jnp.concatenate([qa, qb], axis=1)


def _pair_out(o0, o1):
    return jnp.where(_lane_iota() < NA_HEAD_DIM, o0, o1)


def _softmax_pv(s_list, v_list):
    m = s_list[0].max(axis=-1, keepdims=True)
    for s in s_list[1:]:
        m = jnp.maximum(m, s.max(axis=-1, keepdims=True))
    l = None
    o = None
    for s, v in zip(s_list, v_list):
        e = jnp.exp(s - m)
        ls = e.sum(axis=-1, keepdims=True)
        pv = _dot(e.astype(BF16), v)
        l = ls if l is None else l + ls
        o = pv if o is None else o + pv
    return o / l


CTX_REQS = 2


def _ctx_attn_kernel(*refs, seq):
    for r in range(refs[0].shape[0] // seq):
        _ctx_attn_one(*[ref.at[pl.ds(r * seq, seq), :] for ref in refs])


def _ctx_attn_one(q_ref, k_ref, v_ref, qn_ref, qr_ref, kn_ref, kr_ref, mv_ref, ona_ref, omla_ref):
    for p in range(NA_HEADS // 2):
        sl = slice(p * LANES, (p + 1) * LANES)
        kp = k_ref[:, sl]
        vp = v_ref[:, sl]
        outs = []
        for half in range(2):
            s = _dot_nt(_head_q(q_ref[:, sl], half), kp)
            outs.append(_softmax_pv([s], [vp]))
        ona_ref[:, sl] = _pair_out(*outs)

        kk = jnp.concatenate([kn_ref[:, sl], kr_ref[...]], axis=1)
        mvp = mv_ref[:, sl]
        outs = []
        for half in range(2):
            s = _dot_nt(_mla_q(qn_ref, qr_ref, 2 * p + half), kk) * MLA_SCALE
            outs.append(_softmax_pv([s], [mvp]))
        omla_ref[:, sl] = _pair_out(*outs)


def _ctx_attn(q, k, v, qn, qr, kn, kr, mv, *, seq):
    n = q.shape[0]
    rows = CTX_REQS * seq
    tok = lambda w: pl.BlockSpec((rows, w), lambda b: (b, 0))
    return pl.pallas_call(
        functools.partial(_ctx_attn_kernel, seq=seq),
        grid=(n // rows,),
        in_specs=[tok(512), tok(512), tok(512), tok(512), tok(256), tok(512), tok(LANES), tok(512)],
        out_specs=[tok(512), tok(512)],
        out_shape=[jax.ShapeDtypeStruct((n, 512), F32)] * 2,
        compiler_params=_params(("arbitrary",)),
        name="ctx_attn",
    )(q, k, v, qn, qr, kn, kr, mv)


NA_QROWS = 4
NA_KROWS = 12
_NDR = 2 * NA_ROWS - 1
_NDC = 2 * NA_COLS - 1
_Z_NONE = 48
LOG2E = 1.4426950408889634


def _na_tile_descriptors(rows):
    last_q0 = rows - NA_QROWS
    last_k0 = rows - NA_KROWS
    desc = np.full((3, NA_KROWS, NA_QROWS // 2), _Z_NONE, np.int32)
    for case, (q0, k0) in enumerate(((0, 0), (NA_QROWS, 0), (last_q0, last_k0))):
        for kri in range(NA_KROWS):
            for u in range(NA_QROWS // 2):
                info = []
                for ri in (2 * u, 2 * u + 1):
                    qrow, krow = q0 + ri, k0 + kri
                    start = min(max(qrow - NA_ROWS // 2, 0), rows - NA_ROWS)
                    info.append((start <= krow < start + NA_ROWS, krow - qrow))
                (vl, drl), (vr, _) = info
                d = drl + NA_ROWS - 1
                if vl and vr:
                    desc[case, kri, u] = d
                elif vl:
                    desc[case, kri, u] = 16 + d
                elif vr:
                    desc[case, kri, u] = 32 + d
    return desc.reshape(-1)


def _na_bias_consts():
    col = np.arange(GRID_W)
    col_start = np.clip(col - NA_COLS // 2, 0, GRID_W - NA_COLS)
    kc, qc = col[:, None], col[None, :]
    col_in = (kc >= col_start[None, :]) & (kc < col_start[None, :] + NA_COLS)
    negm = np.where(col_in, 0.0, NEG).astype(np.float32)
    return np.concatenate([negm, negm], axis=1)


def _na_build_bias(bias_ref, negm_ref, z_ref):
    lane = _lane_iota()
    left = lane < GRID_W
    in_window = negm_ref[...] == 0.0

    def coef(h, dr, j):
        if dr < -(NA_ROWS - 1) or dr > NA_ROWS - 1:
            return 0.0
        return bias_ref[(h * _NDR + dr + NA_ROWS - 1) * _NDC + j]

    def per_head(h, c):
        for d in range(16):
            dr_l = d - (NA_ROWS - 1)
            base = jnp.zeros((1, LANES), F32)
            for j in range(_NDC):
                c_left = (NA_COLS - 1 - j) % LANES
                c_right = GRID_W + NA_COLS - 1 - j
                base = jnp.where(lane == c_left, coef(h, dr_l, j), base)
                base = jnp.where(lane == c_right, coef(h, dr_l - 1, j), base)
            rolled = pltpu.roll(jnp.broadcast_to(base, (GRID_W, LANES)), 0, axis=1, stride=1, stride_axis=0)
            acc = jnp.where(in_window, rolled, NEG)
            z_ref[h, d] = acc
            z_ref[h, 16 + d] = jnp.where(left, acc, NEG)
            z_ref[h, 32 + d] = jnp.where(left, NEG, acc)
        z_ref[h, _Z_NONE] = jnp.full((GRID_W, LANES), NEG, F32)
        return c

    lax.fori_loop(0, NA_HEADS, per_head, 0)


def _finish_pair(acc_a, acc_b):
    norm = lambda acc: acc[:HEAD_DIM] * (1.0 / acc[HEAD_DIM:HEAD_DIM + 1])
    return jnp.concatenate([norm(acc_a), norm(acc_b)], axis=0).T


def _na_lat_kernel(bias_ref, desc_ref, q_ref, k_ref, vt_ref, kc_ref, vtc_ref, negm_ref, o_ref, z_ref,
                   *, rows):
    g = pl.program_id(1)
    groups = rows // NA_QROWS

    @pl.when((pl.program_id(0) == 0) & (g == 0))
    def _():
        _na_build_bias(bias_ref, negm_ref, z_ref)

    tq = NA_QROWS * GRID_W
    k_row0 = jnp.clip(g * NA_QROWS - NA_ROWS // 2, 0, rows - NA_KROWS)
    start = pl.multiple_of(k_row0 * GRID_W, KV_CHUNK)
    chunk0 = k_row0 // (KV_CHUNK // GRID_W)
    n_loc = NA_KROWS * GRID_W
    case = jnp.where(g == 0, 0, jnp.where(g == groups - 1, 2, 1))
    upairs = NA_QROWS // 2

    def scores(h):
        sl = slice((h // 2) * LANES, (h // 2 + 1) * LANES)
        qt = _head_q(q_ref[:, sl], h % 2).astype(F32).T.astype(BF16)
        return _dot(k_ref[pl.ds(start, n_loc), sl], qt), _dot(kc_ref[0, :, sl], qt)

    def attend(h, s_loc, s_ctx):
        rows_h = slice(h * VT_ROWS, (h + 1) * VT_ROWS)
        blocks = []
        for kri in range(NA_KROWS):
            tiles = []
            for u in range(upairs):
                idx = desc_ref[(case * NA_KROWS + kri) * upairs + u]
                tiles.append(s_loc[kri * GRID_W:(kri + 1) * GRID_W, u * LANES:(u + 1) * LANES] + z_ref[h, idx])
            blocks.append(jnp.concatenate(tiles, axis=1))
        s_loc = jnp.concatenate(blocks, axis=0)
        m = jnp.maximum(s_loc.max(axis=0, keepdims=True), s_ctx.max(axis=0, keepdims=True))
        p_loc = jnp.exp2((s_loc - m) * LOG2E).astype(BF16)
        p_ctx = jnp.exp2((s_ctx - m) * LOG2E).astype(BF16)
        acc = _dot(vtc_ref[0, rows_h, :], p_ctx)
        for i in range(n_loc // KV_CHUNK):
            acc = acc + _dot(vt_ref[chunk0 + i, rows_h, :], p_loc[i * KV_CHUNK:(i + 1) * KV_CHUNK])
        return acc

    s_next = scores(0)
    accs = []
    for h in range(NA_HEADS):
        s = s_next
        if h + 1 < NA_HEADS:
            s_next = scores(h + 1)
        accs.append(attend(h, *s))
        if h % 2:
            o_ref[:, (h // 2) * LANES:(h // 2 + 1) * LANES] = _finish_pair(accs[h - 1], accs[h])


def _na_lat(q, k, vt, kc, vtc, bias_l, *, batch, seq):
    rows = seq // GRID_W
    n = q.shape[0]
    groups = rows // NA_QROWS
    tq = NA_QROWS * GRID_W
    chunks = seq // KV_CHUNK
    negm = _na_bias_consts()
    desc = jnp.asarray(_na_tile_descriptors(rows))
    smem = pl.BlockSpec(memory_space=pltpu.SMEM)
    full = lambda a: pl.BlockSpec(a.shape, lambda b, g: (0,) * a.ndim)
    return pl.pallas_call(
        functools.partial(_na_lat_kernel, rows=rows),
        grid=(batch, groups),
        in_specs=[
            smem, smem,
            pl.BlockSpec((tq, 512), lambda b, g: (b * groups + g, 0)),
            pl.BlockSpec((seq, 512), lambda b, g: (b, 0)),
            pl.BlockSpec((chunks, VT_ALL, KV_CHUNK), lambda b, g: (b, 0, 0)),
            pl.BlockSpec((1,) + kc.shape[1:], lambda b, g: (b, 0, 0)),
            pl.BlockSpec((1,) + vtc.shape[1:], lambda b, g: (b, 0, 0)),
            full(negm),
        ],
        out_specs=pl.BlockSpec((tq, 512), lambda b, g: (b * groups + g, 0)),
        out_shape=jax.ShapeDtypeStruct((n, 512), F32),
        scratch_shapes=[pltpu.VMEM((NA_HEADS, _Z_NONE + 1, GRID_W, LANES), F32)],
        compiler_params=_params(("arbitrary", "arbitrary")),
        name="na_lat",
    )(bias_l.reshape(-1), desc, q, k, vt, kc, vtc, jnp.asarray(negm))


_MLA_C = MLA_SCALE * LOG2E


def _mla_lat_kernel(qn_ref, qr_ref, kn_ref, kr_ref, vt_ref, knc_ref, krc_ref, vtc_ref, o_ref,
                    q2_ref, m_ref, acc_ref, sa_ref, sb_ref, *, tk):
    tq = qn_ref.shape[0]
    n_tiles = kn_ref.shape[0] // tk
    cpt = tk // KV_CHUNK
    pairs = MLA_HEADS // 2
    for p in range(pairs):
        qt = [_mla_q(qn_ref, qr_ref, 2 * p + half).astype(F32).T.astype(BF16) for half in range(2)]
        q2_ref[p] = jnp.concatenate(qt, axis=1)
        m_ref[p] = jnp.full((1, 2 * tq), NEG, F32)
        acc_ref[p] = jnp.zeros((2, VT_ROWS, tq), F32)

    def scores(p, kk):
        return _dot(kk, q2_ref[p])

    def update(p, s, vts):
        m_prev = m_ref[p]
        m_new = jnp.maximum(m_prev, s.max(axis=0, keepdims=True))
        alpha = jnp.exp2((m_prev - m_new) * _MLA_C)
        pt = jnp.exp2((s - m_new) * _MLA_C).astype(BF16)
        accs = []
        for half in range(2):
            cols = slice(half * tq, (half + 1) * tq)
            acc = acc_ref[p, half] * alpha[:, cols]
            for i, vt in enumerate(vts):
                acc = acc + _dot(vt[half * VT_ROWS:(half + 1) * VT_ROWS], pt[i * KV_CHUNK:(i + 1) * KV_CHUNK, cols])
            accs.append(acc)
        return m_new, accs

    def store(new):
        for p, (m_new, accs) in enumerate(new):
            acc_ref[p, 0] = accs[0]
            acc_ref[p, 1] = accs[1]
            m_ref[p] = m_new

    def lanes(p):
        return slice(p * LANES, (p + 1) * LANES)

    def rows(p):
        return slice(2 * p * VT_ROWS, 2 * (p + 1) * VT_ROWS)

    def keys(t, p):
        start = t * tk if isinstance(t, int) else pl.multiple_of(t * tk, tk)
        ks = pl.ds(start, tk)
        return jnp.concatenate([kn_ref[ks, lanes(p)], kr_ref[ks, :]], axis=1)

    for p in range(pairs):
        sa_ref[p] = scores(p, keys(0, p))

    n_ctx = knc_ref.shape[1]

    def latent_next(t):
        def put(nxt_ref, p):
            nxt_ref[p] = scores(p, keys(t + 1, p))
        return put

    def ctx_next(nxt_ref, p):
        kk = jnp.concatenate([knc_ref[0, :, lanes(p)], krc_ref[0]], axis=1)
        nxt_ref[p, :n_ctx] = scores(p, kk)

    def tile_step(t, cur_ref, nxt_ref, put_next):
        ahead = 1
        for p in range(ahead):
            put_next(nxt_ref, p)
        new = []
        for p in range(pairs):
            new.append(update(p, cur_ref[p], [vt_ref[t * cpt + i, rows(p), :] for i in range(cpt)]))
            if p + ahead < pairs:
                put_next(nxt_ref, p + ahead)
        store(new)

    def body(i, c):
        tile_step(2 * i, sa_ref, sb_ref, latent_next(2 * i))
        tile_step(2 * i + 1, sb_ref, sa_ref, latent_next(2 * i + 1))
        return c

    lax.fori_loop(0, n_tiles // 2 - 1, body, 0)
    tile_step(n_tiles - 2, sa_ref, sb_ref, latent_next(n_tiles - 2))
    tile_step(n_tiles - 1, sb_ref, sa_ref, ctx_next)
    for p in range(pairs):
        _, accs = update(p, sa_ref[p, :n_ctx], [vtc_ref[0, rows(p), :]])
        o_ref[:, lanes(p)] = _finish_pair(*accs)


def _mla_lat(qn, qr, kn, kr, vt, knc, krc, vtc, *, batch, seq, tq, tk):
    n = qn.shape[0]
    steps = seq // tq
    chunks = seq // KV_CHUNK
    assert knc.shape[1] == KV_CHUNK
    qtok = lambda w: pl.BlockSpec((tq, w), lambda b, i: (b * steps + i, 0))
    ktok = lambda w: pl.BlockSpec((seq, w), lambda b, i: (b, 0))
    ctok = lambda a: pl.BlockSpec((1,) + a.shape[1:], lambda b, i: (b, 0, 0))
    pairs = MLA_HEADS // 2
    return pl.pallas_call(
        functools.partial(_mla_lat_kernel, tk=tk),
        grid=(batch, steps),
        in_specs=[qtok(512), qtok(256), ktok(512), ktok(LANES),
                  pl.BlockSpec((chunks, VT_ALL, KV_CHUNK), lambda b, i: (b, 0, 0)),
                  ctok(knc), ctok(krc), ctok(vtc)],
        out_specs=qtok(512),
        out_shape=jax.ShapeDtypeStruct((n, 512), F32),
        scratch_shapes=[pltpu.VMEM((pairs, 2 * LANES, 2 * tq), BF16),
                        pltpu.VMEM((pairs, 1, 2 * tq), F32),
                        pltpu.VMEM((pairs, 2, VT_ROWS, tq), F32),
                        pltpu.VMEM((pairs, tk, 2 * tq), F32),
                        pltpu.VMEM((pairs, tk, 2 * tq), F32)],
        compiler_params=_params(("arbitrary", "arbitrary")),
        name="mla_lat",
    )(qn, qr, kn, kr, vt, knc, krc, vtc)


def _cache_kv_kernel(ckv_ref, w_ref, wvt_ref, kn_ref, vt_ref):
    ckv = ckv_ref[0, 0].astype(BF16)
    kn_ref[0, 0] = _dot(ckv, w_ref[0]).astype(BF16)
    _store_vt(vt_ref.at[0], _dot_nt(wvt_ref[0], ckv))


def _cache_kv(ckv, wukv_k, wukv_vt):
    _, nb, rows, _ = ckv.shape
    assert rows == KV_CHUNK
    spec = lambda a, b: pl.BlockSpec((1, a, b), lambda l, j: (l, 0, 0))
    return pl.pallas_call(
        _cache_kv_kernel,
        grid=(DEPTH, nb),
        in_specs=[pl.BlockSpec((1, 1, rows, KV_LORA), lambda l, j: (l, j, 0, 0)),
                  spec(KV_LORA, 512), spec(512, KV_LORA)],
        out_specs=[pl.BlockSpec((1, 1, rows, 512), lambda l, j: (l, j, 0, 0)),
                   pl.BlockSpec((1, 1, VT_ALL, KV_CHUNK), lambda l, j: (l, j, 0, 0))],
        out_shape=[jax.ShapeDtypeStruct((DEPTH, nb, rows, 512), BF16),
                   jax.ShapeDtypeStruct((DEPTH, nb, VT_ALL, KV_CHUNK), BF16)],
        compiler_params=_params(("arbitrary", "arbitrary")),
        name="cache_kv",
    )(ckv, wukv_k, wukv_vt)


def _dft_mats(n, scale):
    k = np.arange(n)
    ang = 2.0 * np.pi * ((k[:, None] * k[None, :]) % n) / n
    return np.cos(ang) * scale, np.sin(ang) * scale


def _hi_lo_np(m):
    m = jnp.asarray(m, F32)
    hi = m.astype(BF16)
    lo = (m - hi.astype(F32)).astype(BF16)
    return hi, lo


def _chan_dft(u, wc_ref):
    uh, ul = _split(u)
    wh = wc_ref[0]
    wl = wc_ref[1]
    return _dot(uh, wh) + _dot(ul, wh) + _dot(uh, wl)


def _fn_ctx_kernel(u_ref, wc_ref, fp_ref, o_ref):
    his, los = [], []
    for g in range(FN_GROUPS):
        sl = slice(g * FN_GROUP_W, (g + 1) * FN_GROUP_W)
        a = _chan_dft(u_ref[:, sl], wc_ref)
        ar_h, ar_l = _split(a[:, :FN_GROUP_W])
        ai_h, ai_l = _split(a[:, FN_GROUP_W:])
        his.append(jnp.concatenate([ar_h, ai_h], axis=0))
        los.append(jnp.concatenate([ar_l, ai_l], axis=0))
    rh = jnp.concatenate(his, axis=1)
    rl = jnp.concatenate(los, axis=1)
    o_ref[...] = _dot(fp_ref[0], rh) + _dot(fp_ref[0], rl) + _dot(fp_ref[1], rh)


def _fn_ctx(u, *, seq):
    n = u.shape[0]
    cc, sc = _dft_mats(FN_GROUP_W, FN_GROUP_W ** -0.5)
    wc = jnp.stack(_hi_lo_np(np.concatenate([cc, -sc], axis=1)))
    cp, sp = _dft_mats(seq, seq ** -0.5)
    fp = jnp.stack(_hi_lo_np(np.concatenate([cp, sp], axis=1)))
    full = lambda a: pl.BlockSpec(a.shape, lambda b: (0,) * a.ndim)
    return pl.pallas_call(
        _fn_ctx_kernel,
        grid=(n // seq,),
        in_specs=[pl.BlockSpec((seq, FN_WIDTH), lambda b: (b, 0)), full(wc), full(fp)],
        out_specs=pl.BlockSpec((seq, FN_WIDTH), lambda b: (b, 0)),
        out_shape=jax.ShapeDtypeStruct((n, FN_WIDTH), F32),
        compiler_params=_params(("arbitrary",)),
        name="fn_ctx",
    )(u, wc, fp)


_FN_G = 32


def _fn_lat_kernel(u_ref, wc_ref, f1_ref, f2_ref, twc_ref, tws_ref, o_ref, a_ref, t_ref):
    n = GRID_W
    w = FN_GROUP_W

    def chan(i, c):
        rows = pl.ds(pl.multiple_of(i * 512, 512), 512)
        a = _chan_dft(u_ref[rows, :], wc_ref)
        a_ref[0, rows, :] = a[:, :FN_GROUP_W]
        a_ref[1, rows, :] = a[:, FN_GROUP_W:]
        return c

    lax.fori_loop(0, u_ref.shape[0] // 512, chan, 0)

    def gather(ref, base):
        cols = [jnp.concatenate([ref[0, pl.ds(base + j, n, stride=n), :],
                                 ref[1, pl.ds(base + j, n, stride=n), :]], axis=0) for j in range(_FN_G)]
        return jnp.concatenate(cols, axis=1)

    def dft(f_ref, d):
        rows = f_ref.shape[0] // 2
        dh, dl = _split(d)
        rh = _dot(f_ref[...], dh)
        return rh[:rows] + rh[rows:] + _dot(f_ref[:rows, :], dl)

    def stage1(i, c):
        base = i * _FN_G
        b = dft(f1_ref, gather(a_ref, base))
        for j in range(_FN_G):
            br = b[:n, j * w:(j + 1) * w]
            bi = b[n:, j * w:(j + 1) * w]
            rows = pl.ds(pl.multiple_of((base + j) * n, n), n)
            tc = twc_ref[rows, :]
            ts = tws_ref[rows, :]
            t_ref[0, rows, :] = br * tc + bi * ts
            t_ref[1, rows, :] = bi * tc - br * ts
        return c

    lax.fori_loop(0, n // _FN_G, stage1, 0)

    def stage2(i, c):
        base = i * _FN_G
        y = dft(f2_ref, gather(t_ref, base))
        for j in range(_FN_G):
            o_ref[pl.ds(base + j, n, stride=n), :] = y[:, j * w:(j + 1) * w]
        return c

    lax.fori_loop(0, n // _FN_G, stage2, 0)


def _fn_lat(u, *, batch, seq):
    n = GRID_W
    assert seq == n * n
    cc, sc = _dft_mats(FN_GROUP_W, FN_GROUP_W ** -0.5)
    wc = jnp.stack(_hi_lo_np(np.concatenate([cc, -sc], axis=1)))
    c1, s1 = _dft_mats(n, 1.0)
    c2, s2 = _dft_mats(n, 1.0 / n)
    f1 = jnp.concatenate(_hi_lo_np(np.block([[c1, s1], [-s1, c1]])), axis=0)
    f2 = jnp.concatenate(_hi_lo_np(np.concatenate([c2, s2], axis=1)), axis=0)
    n2 = np.arange(n)[:, None]
    k1 = np.arange(n)[None, :]
    ang = (2.0 * np.pi * ((n2 * k1) % seq) / seq).reshape(seq, 1)
    twc = jnp.asarray(np.broadcast_to(np.cos(ang), (seq, FN_GROUP_W)), F32)
    tws = jnp.asarray(np.broadcast_to(np.sin(ang), (seq, FN_GROUP_W)), F32)
    full = lambda a: pl.BlockSpec(a.shape, lambda b, g: (0,) * a.ndim)
    blk = pl.BlockSpec((seq, FN_GROUP_W), lambda b, g: (b, g))
    return pl.pallas_call(
        _fn_lat_kernel,
        grid=(batch, FN_GROUPS),
        in_specs=[blk, full(wc), full(f1), full(f2), full(twc), full(tws)],
        out_specs=blk,
        out_shape=jax.ShapeDtypeStruct(u.shape, F32),
        scratch_shapes=[pltpu.VMEM((2, seq, FN_GROUP_W), F32), pltpu.VMEM((2, seq, FN_GROUP_W), F32)],
        compiler_params=_params(("arbitrary", "arbitrary")),
        name="fn_lat",
    )(u, wc, f1, f2, twc, tws)


_E_GNA, _E_GMLA, _E_GFN, _E_MERGE = 0, 512, 1024, 1536


def _merge_kernel(*refs, final):
    (x_ref, mod_ref, ng_ref, wg_ref, ona_ref, omla_ref, ofn_ref, wona_ref, womla_ref, wofn_ref,
     wout_ref) = refs[:11]
    if final:
        fg_ref, o_ref = refs[11:13]
    else:
        o_ref = refs[11]
    x, xm = _modulated(x_ref, mod_ref, ng_ref)
    gate = mod_ref[0, 2:3, :]

    merged = None
    for i, (br_ref, wo_ref) in enumerate(((ona_ref, wona_ref), (omla_ref, womla_ref), (ofn_ref, wofn_ref))):
        g = _dot_nt(xm, wg_ref[0, i * 512:(i + 1) * 512, :])
        t = _dot((br_ref[...] * _silu(g)).astype(BF16), wo_ref[0])
        ml = _dot_nt(xm, wg_ref[0, _E_MERGE + i * D_MODEL:_E_MERGE + (i + 1) * D_MODEL, :])
        t = _sigmoid(ml) * t
        merged = t if merged is None else merged + t
    h = x + gate * _dot(merged.astype(BF16), wout_ref[0])
    if final:
        h = _rms(h, fg_ref[...])
    o_ref[...] = h


def _layer_spec(a, layer):
    return pl.BlockSpec((1,) + a.shape[1:], lambda *_: (layer,) + (0,) * (a.ndim - 1))


def _merge(x, mod, ng, w_g, ona, omla, ofn, wona, womla, wofn, wout, fg, *, tile, layer):
    n = x.shape[0]
    groups = mod.shape[0]
    steps = n // tile
    per_group = steps // groups
    final = fg is not None
    tok = lambda w: pl.BlockSpec((tile, w), lambda i: (i, 0))
    full = lambda a: pl.BlockSpec(a.shape, lambda i: (0,) * a.ndim)
    lay = lambda a: _layer_spec(a, layer)
    in_specs = [tok(D_MODEL), pl.BlockSpec((1, 3, D_MODEL), lambda i: (i // per_group, 0, 0)),
                full(ng), lay(w_g), tok(512), tok(512), tok(512),
                lay(wona), lay(womla), lay(wofn), lay(wout)]
    args = [x, mod, ng, w_g, ona, omla, ofn, wona, womla, wofn, wout]
    if final:
        in_specs.append(full(fg))
        args.append(fg)
    return pl.pallas_call(
        functools.partial(_merge_kernel, final=final),
        grid=(steps,),
        in_specs=in_specs,
        out_specs=tok(D_MODEL),
        out_shape=jax.ShapeDtypeStruct((n, D_MODEL), F32),
        compiler_params=_params(("arbitrary",)),
        name="merge",
    )(*args)


def _rope_rot(w, axis):
    q = MLA_ROPE // 4
    part = lambda i: lax.slice_in_dim(w, i * q, (i + 1) * q, axis=axis)
    return jnp.concatenate([-part(1), part(0), -part(3), part(2)], axis=axis)


def _prep_w_in(w_in):
    wt = jnp.swapaxes(w_in, 1, 2)
    rows = lambda start, n: wt[:, start:start + n]
    kr = rows(_O_KROPE, MLA_ROPE)
    kr_rot = _rope_rot(kr, 1)
    a = jnp.concatenate([rows(_O_QKV, 1536), rows(_O_QLAT, Q_LORA + KV_LORA),
                         kr, kr_rot, kr, kr_rot, rows(_O_UFN, FN_WIDTH)], axis=1)
    e = jnp.concatenate([rows(_O_GATE_NA, 512), rows(_O_GATE_MLA, 512), rows(_O_GATE_FN, 512),
                         rows(_O_MERGE, 3 * D_MODEL)], axis=1)
    return a.astype(BF16), e.astype(BF16)


def _prep_w_uq(w):
    w3 = w.reshape(Q_LORA, MLA_HEADS, MLA_NOPE + MLA_ROPE)
    nope = w3[:, :, :MLA_NOPE].reshape(Q_LORA, MLA_HEADS * MLA_NOPE)
    rope = w3[:, :, MLA_NOPE:]
    rot = _rope_rot(rope, 2)
    return jnp.concatenate([nope, rope.reshape(Q_LORA, -1), rot.reshape(Q_LORA, -1)], axis=1).astype(BF16)


def _prep_w_ukv(w):
    w3 = w.reshape(KV_LORA, MLA_HEADS, MLA_NOPE + MLA_V)
    return jnp.concatenate([w3[:, :, :MLA_NOPE].reshape(KV_LORA, -1),
                            w3[:, :, MLA_NOPE:].reshape(KV_LORA, -1)], axis=1).astype(BF16)


def _rope_tables(n):
    t = jnp.arange(n, dtype=jnp.int32)
    row = (t // GRID_W).astype(F32)
    col = (t % GRID_W).astype(F32)
    half = MLA_ROPE // 2
    inv_freq = ROPE_THETA ** (-jnp.arange(0, half, 2, dtype=F32) / half)
    ar = row[:, None] * inv_freq[None, :]
    ac = col[:, None] * inv_freq[None, :]
    ang = jnp.concatenate([ar, ar, ac, ac], axis=-1)
    return jnp.tile(jnp.cos(ang), (1, MLA_HEADS)), jnp.tile(jnp.sin(ang), (1, MLA_HEADS))


def kernel(x_prompt, x_sample, cache_na_k, cache_na_v, cache_mla_ckv, cache_mla_krope, c, c_ctx,
           w_ada, b_ada, norm_g, w_in, q_norm_g, kv_norm_g, w_uq, w_ukv, na_bias,
           w_o_na, w_o_mla, w_o_fourier, w_out, final_norm_g):
    batch, seq, _ = x_prompt.shape
    dbatch, dseq, _ = x_sample.shape
    past = cache_na_k.shape[2]

    cond = jnp.zeros((SUBLANES, D_MODEL), F32).at[0].set(c_ctx).at[1:1 + dbatch].set(c)
    mods = _ada_mods(cond, w_ada, b_ada).reshape(DEPTH, SUBLANES, 3, D_MODEL)

    wukv_all = jnp.stack([_prep_w_ukv(w_ukv[l]) for l in range(DEPTH)])
    wukv_vt_all = wukv_all[:, :, 512:].transpose(0, 2, 1)
    knc_all, mvtc_all = _cache_kv(cache_mla_ckv.transpose(1, 0, 2, 3), wukv_all[:, :, :512], wukv_vt_all)
    krc_all = jnp.tile(cache_mla_krope, (1, 1, 1, 4)).astype(BF16)
    nakc_all = cache_na_k.reshape(dbatch, DEPTH, past, NA_WIDTH).astype(BF16)
    navt = cache_na_v.transpose(0, 1, 3, 4, 2).astype(BF16)
    navtc_all = jnp.concatenate([navt, jnp.ones((dbatch, DEPTH, NA_HEADS, ONES_ROWS, past), BF16)],
                                axis=3).reshape(dbatch, DEPTH, VT_ALL, past)
    cos, sin = _rope_tables(dseq)

    h_ctx = x_prompt.reshape(batch * seq, D_MODEL)
    h_lat = x_sample.reshape(dbatch * dseq, D_MODEL)
    ks, vs, ckvs, krs = [], [], [], []
    row = lambda a: a.reshape(1, -1)
    w_a, w_e = _prep_w_in(w_in)
    wona, womla, wofn, wout = (w_o_na.astype(BF16), w_o_mla.astype(BF16),
                               w_o_fourier.astype(BF16), w_out.astype(BF16))
    for l in range(DEPTH):
        wuq = _prep_w_uq(w_uq[l])
        wukv = wukv_all[l]
        ng, qng, kvng = row(norm_g[l]), row(q_norm_g[l]), row(kv_norm_g[l])
        fg = row(final_norm_g) if l == DEPTH - 1 else None
        mod_ctx = mods[l, 0:1]
        mod_lat = mods[l, 1:1 + dbatch]

        (q, k, v, qn, qr, kn, kr, mv, u, kf, vf, ckvf, krf) = _inproj(
            h_ctx, mod_ctx, ng, w_a, qng, kvng, wuq, wukv, None, tile=INPROJ_TILE, layer=l, seq=seq)
        ks.append(kf)
        vs.append(vf)
        ckvs.append(ckvf)
        krs.append(krf)
        ona, omla = _ctx_attn(q, k, v, qn, qr, kn, kr, mv, seq=seq)
        ofn = _fn_ctx(u, seq=seq)
        h_ctx = _merge(h_ctx, mod_ctx, ng, w_e, ona, omla, ofn, wona, womla, wofn, wout, fg, tile=TOKEN_TILE, layer=l)

        (q, k, vt, qn, qr, kn, kr, mvt, u) = _inproj(
            h_lat, mod_lat, ng, w_a, qng, kvng, wuq, wukv, (wukv_vt_all[l], cos, sin), tile=INPROJ_TILE, layer=l)
        ona = _na_lat(q, k, vt, nakc_all[:, l], navtc_all[:, l], na_bias[l], batch=dbatch, seq=dseq)
        omla = _mla_lat(qn, qr, kn, kr, mvt, knc_all[l], krc_all[:, l], mvtc_all[l],
                        batch=dbatch, seq=dseq, tq=MLA_TQ, tk=MLA_TK)
        ofn = _fn_lat(u, batch=dbatch, seq=dseq)
        h_lat = _merge(h_lat, mod_lat, ng, w_e, ona, omla, ofn, wona, womla, wofn, wout, fg, tile=TOKEN_TILE, layer=l)

    y_prompt = h_ctx.reshape(batch, seq, D_MODEL)
    y_sample = h_lat.reshape(dbatch, dseq, D_MODEL)
    heads_t = lambda xs: jnp.stack(xs, axis=1).reshape(
        batch, DEPTH, NA_HEADS, NA_HEAD_DIM, seq).transpose(0, 1, 4, 2, 3)
    new_na_k = heads_t(ks)
    new_na_v = heads_t(vs)
    new_mla_ckv = jnp.stack([a.reshape(batch, seq, KV_LORA) for a in ckvs], axis=1)
    new_mla_krope = jnp.stack(krs, axis=1).transpose(0, 1, 3, 2)
    return (y_prompt, y_sample, new_na_k, new_na_v, new_mla_ckv, new_mla_krope)
```
